```python
import jax, jax.numpy as jnp
from jax import lax
import numpy as np

D_MODEL = 1024
BATCH = 8
SEQ = 2048
DEPTH = 1
DEC_BATCH = 8
DEC_SEQ = 16
PAST_LEN = 2048

CHUNK = 64
Q_BLOCK = 128
EPS = 1e-6
NEG_INF = -1e30
N_HEADS_A = 8
NOPE_DIM = 64
ROPE_DIM = 32
V_DIM_A = 64
QK_DIM_A = NOPE_DIM + ROPE_DIM
Q_LORA = 384
KV_LORA = 256
ROPE_BASE = 10000.0
N_HEADS_B = 8
HEAD_DIM_B = 64
LEFT_CHUNKS = 8
BAND_WINDOW = LEFT_CHUNKS * CHUNK
REL_CLIP = 128
N_REL = 2 * REL_CLIP + 1
D_FF = -(-(8 * D_MODEL) // (3 * 256)) * 256
N_BRANCH = 2
COL_QKV_B = 3 * N_HEADS_B * HEAD_DIM_B
COL_GATE = N_BRANCH * D_MODEL
IN_SPLITS = [Q_LORA, Q_LORA + KV_LORA, Q_LORA + KV_LORA + ROPE_DIM, Q_LORA + KV_LORA + ROPE_DIM + COL_QKV_B]
IN_COLS = Q_LORA + KV_LORA + ROPE_DIM + COL_QKV_B + COL_GATE

kernel_name = 'hybrid_mla_chunkband_adaln_stream_step'


def rmsnorm(x, g):
    xf = x.astype(jnp.float32)
    xf = xf * lax.rsqrt(jnp.mean(xf * xf, axis=-1, keepdims=True) + EPS)
    return (xf * g.astype(jnp.float32)).astype(x.dtype)


def modulate(h, shift, scale):
    return h * (1.0 + scale) + shift


def rope(x, pos):
    half = ROPE_DIM // 2
    inv_freq = ROPE_BASE ** (-jnp.arange(half, dtype=jnp.float32) / half)
    ang = pos.astype(jnp.float32)[:, None] * inv_freq[None, :]
    shape = (1, pos.shape[0]) + (1,) * (x.ndim - 3) + (half,)
    cos = jnp.cos(ang).reshape(shape)
    sin = jnp.sin(ang).reshape(shape)
    xf = x.astype(jnp.float32)
    x1, x2 = xf[..., :half], xf[..., half:]
    return jnp.concatenate([x1 * cos - x2 * sin, x2 * cos + x1 * sin], axis=-1).astype(x.dtype)


def softmax_attend(q, k, v, bias):
    scale = q.shape[-1] ** -0.5
    s = jnp.einsum('bqhd,bkhd->bhqk', q, k).astype(jnp.float32) * scale + bias
    p = jax.nn.softmax(s, axis=-1).astype(v.dtype)
    return jnp.einsum('bhqk,bkhd->bqhd', p, v)


def ada_terms(c, w_ada, b_ada):
    m = jax.nn.silu(c) @ w_ada + b_ada
    return jnp.split(m[:, None, :], 6, axis=-1)


def mixer_inputs(h, pos, w_in, g_q_lora, w_q_up, g_kv_lora, g_qn_a, g_qr_a, g_kr_a, g_q_b, g_k_b):
    B, S, _ = h.shape
    z = h @ w_in
    c_q, c_kv, k_pe, qkv_b, gate_logits = jnp.split(z, IN_SPLITS, axis=-1)
    q_a = (rmsnorm(c_q, g_q_lora) @ w_q_up).reshape(B, S, N_HEADS_A, QK_DIM_A)
    q_a = jnp.concatenate([rmsnorm(q_a[..., :NOPE_DIM], g_qn_a),
                           rope(rmsnorm(q_a[..., NOPE_DIM:], g_qr_a), pos)], axis=-1)
    latent = rmsnorm(c_kv, g_kv_lora)
    k_rope = rope(rmsnorm(k_pe, g_kr_a), pos)
    qkv_b = qkv_b.reshape(B, S, 3, N_HEADS_B, HEAD_DIM_B)
    q_b = rmsnorm(qkv_b[:, :, 0], g_q_b)
    k_b = rmsnorm(qkv_b[:, :, 1], g_k_b)
    v_b = qkv_b[:, :, 2]
    gates = jax.nn.sigmoid(gate_logits).reshape(B, S, N_BRANCH, D_MODEL)
    return q_a, latent, k_rope, q_b, k_b, v_b, gates


def mla_keys(latent, k_rope, w_kv_up, g_kn_a):
    B, T, _ = latent.shape
    kv = (latent @ w_kv_up).reshape(B, T, N_HEADS_A, NOPE_DIM + V_DIM_A)
    k_nope = rmsnorm(kv[..., :NOPE_DIM], g_kn_a)
    k = jnp.concatenate([k_nope, jnp.broadcast_to(k_rope[:, :, None, :], (B, T, N_HEADS_A, ROPE_DIM))], axis=-1)
    return k, kv[..., NOPE_DIM:]


def mla_prompt(q, k, v, pos):
    B, S, H, D = q.shape
    nb = S // Q_BLOCK
    qb = jnp.moveaxis(q.reshape(B, nb, Q_BLOCK, H, D), 1, 0)
    qchunk = (pos // CHUNK).reshape(nb, Q_BLOCK)
    kchunk = pos // CHUNK

    def block(args):
        qi, qc = args
        bias = jnp.where(qc[:, None] >= kchunk[None, :], 0.0, NEG_INF).astype(jnp.float32)
        return softmax_attend(qi, k, v, bias)

    o = lax.map(block, (qb, qchunk))
    return jnp.moveaxis(o, 0, 1).reshape(B, S, H, v.shape[-1])


def band_bias(rel_bias, qpos, kpos):
    idx = jnp.clip(qpos[:, None] - kpos[None, :], -REL_CLIP, REL_CLIP) + REL_CLIP
    return rel_bias[:, idx].astype(jnp.float32)


def band_prompt(q, k, v, rel_bias):
    B, S, H, Dh = q.shape
    nc = S // CHUNK
    nband = LEFT_CHUNKS + 1

    def band(t):
        tc = t.reshape(B, nc, CHUNK, H, Dh)
        tp = jnp.concatenate([jnp.zeros((B, LEFT_CHUNKS, CHUNK, H, Dh), t.dtype), tc], axis=1)
        return jnp.concatenate([tp[:, o:o + nc] for o in range(nband)], axis=2)

    kb, vb = band(k), band(v)
    kpos_rel = jnp.arange(nband * CHUNK) - LEFT_CHUNKS * CHUNK
    bias = band_bias(rel_bias, jnp.arange(CHUNK), kpos_rel)
    valid = (jnp.arange(nc)[:, None] * CHUNK + kpos_rel[None, :]) >= 0
    s = jnp.einsum('bnqhd,bnkhd->bnhqk', q.reshape(B, nc, CHUNK, H, Dh), kb).astype(jnp.float32) * (Dh ** -0.5)
    s = jnp.where(valid[None, :, None, None, :], s + bias[None, None], NEG_INF)
    p = jax.nn.softmax(s, axis=-1).astype(v.dtype)
    o = jnp.einsum('bnhqk,bnkhd->bnqhd', p, vb)
    return o.reshape(B, S, H, Dh)


def band_sample(q, k_new, v_new, k_buf, v_buf, rel_bias, past_len):
    Sd = q.shape[1]
    W = k_buf.shape[1]
    k = jnp.concatenate([k_buf, k_new], axis=1)
    v = jnp.concatenate([v_buf, v_new], axis=1)
    kpos = jnp.concatenate([past_len - W + jnp.arange(W), past_len + jnp.arange(Sd)])
    qpos = past_len + jnp.arange(Sd)
    return softmax_attend(q, k, v, band_bias(rel_bias, qpos, kpos)[None])


def layer_tail(x, ada, o_a, o_b, gates, w_o_a, w_o_b, w_out, g_norm_ffn, w_gate, w_up, w_down):
    shift1, scale1, gate1, shift2, scale2, gate2 = ada
    B, S = o_a.shape[:2]
    y_a = o_a.reshape(B, S, N_HEADS_A * V_DIM_A) @ w_o_a
    y_b = o_b.reshape(B, S, N_HEADS_B * HEAD_DIM_B) @ w_o_b
    x = x + gate1 * ((gates[:, :, 0] * y_a + gates[:, :, 1] * y_b) @ w_out)
    h = modulate(rmsnorm(x, g_norm_ffn), shift2, scale2)
    return x + gate2 * ((jax.nn.silu(h @ w_gate) * (h @ w_up)) @ w_down)


def setup_inputs(seed: int = 0) -> dict:
    key = jax.random.key(seed)
    ks = jax.random.split(key, 32)
    L = DEPTH

    def nrm(k, shape, scale):
        return scale * jax.random.normal(k, shape, dtype=jnp.float32)

    def gain(k, shape):
        return 1.0 + 0.05 * jax.random.normal(k, shape, dtype=jnp.float32)

    band_keep = min(BAND_WINDOW, PAST_LEN)
    return {
        'x_prompt': nrm(ks[0], (BATCH, SEQ, D_MODEL), 1.0),
        'x_sample': nrm(ks[1], (DEC_BATCH, DEC_SEQ, D_MODEL), 1.0),
        'c_prompt': nrm(ks[2], (BATCH, D_MODEL), 1.0),
        'c_sample': nrm(ks[3], (DEC_BATCH, D_MODEL), 1.0),
        'cache_kv_latent': nrm(ks[4], (L, DEC_BATCH, PAST_LEN, KV_LORA), 1.0),
        'cache_k_rope': nrm(ks[5], (L, DEC_BATCH, PAST_LEN, ROPE_DIM), 1.0),
        'cache_band_k': nrm(ks[6], (L, DEC_BATCH, band_keep, N_HEADS_B, HEAD_DIM_B), 1.0),
        'cache_band_v': nrm(ks[7], (L, DEC_BATCH, band_keep, N_HEADS_B, HEAD_DIM_B), 1.0),
        'w_ada': nrm(ks[8], (L, D_MODEL, 6 * D_MODEL), 0.5 * D_MODEL ** -0.5),
        'b_ada': nrm(ks[9], (L, 6 * D_MODEL), 0.02),
        'g_norm_mix': gain(ks[10], (L, D_MODEL)),
        'w_in': nrm(ks[11], (L, D_MODEL, IN_COLS), D_MODEL ** -0.5),
        'g_q_lora': gain(ks[12], (L, Q_LORA)),
        'w_q_up': nrm(ks[13], (L, Q_LORA, N_HEADS_A * QK_DIM_A), Q_LORA ** -0.5),
        'g_kv_lora': gain(ks[14], (L, KV_LORA)),
        'w_kv_up': nrm(ks[15], (L, KV_LORA, N_HEADS_A * (NOPE_DIM + V_DIM_A)), KV_LORA ** -0.5),
        'g_qn_a': gain(ks[16], (L, NOPE_DIM)),
        'g_kn_a': gain(ks[17], (L, NOPE_DIM)),
        'g_qr_a': gain(ks[18], (L, ROPE_DIM)),
        'g_kr_a': gain(ks[19], (L, ROPE_DIM)),
        'g_q_b': gain(ks[20], (L, HEAD_DIM_B)),
        'g_k_b': gain(ks[21], (L, HEAD_DIM_B)),
        'rel_bias': nrm(ks[22], (L, N_HEADS_B, N_REL), 0.1),
        'w_o_a': nrm(ks[23], (L, N_HEADS_A * V_DIM_A, D_MODEL), (N_HEADS_A * V_DIM_A) ** -0.5),
        'w_o_b': nrm(ks[24], (L, N_HEADS_B * HEAD_DIM_B, D_MODEL), (N_HEADS_B * HEAD_DIM_B) ** -0.5),
        'w_out': nrm(ks[25], (L, D_MODEL, D_MODEL), D_MODEL ** -0.5),
        'g_norm_ffn': gain(ks[26], (L, D_MODEL)),
        'w_gate': nrm(ks[27], (L, D_MODEL, D_FF), D_MODEL ** -0.5),
        'w_up': nrm(ks[28], (L, D_MODEL, D_FF), D_MODEL ** -0.5),
        'w_down': nrm(ks[29], (L, D_FF, D_MODEL), D_FF ** -0.5),
    }


def reference(x_prompt, x_sample, c_prompt, c_sample, cache_kv_latent, cache_k_rope, cache_band_k, cache_band_v,
              w_ada, b_ada, g_norm_mix, w_in, g_q_lora, w_q_up, g_kv_lora, w_kv_up, g_qn_a, g_kn_a, g_qr_a, g_kr_a,
              g_q_b, g_k_b, rel_bias, w_o_a, w_o_b, w_out, g_norm_ffn, w_gate, w_up, w_down):
    S = x_prompt.shape[1]
    Sd = x_sample.shape[1]
    past_len = cache_kv_latent.shape[2]
    keep = min(BAND_WINDOW, S)
    pos_p = jnp.arange(S)
    pos_s = past_len + jnp.arange(Sd)
    xp, xs = x_prompt, x_sample
    lat_p, kr_p, bk_p, bv_p = [], [], [], []
    lat_s, kr_s, bk_s, bv_s = [], [], [], []
    for l in range(DEPTH):
        mix_w = (w_in[l], g_q_lora[l], w_q_up[l], g_kv_lora[l], g_qn_a[l], g_qr_a[l], g_kr_a[l], g_q_b[l], g_k_b[l])
        out_w = (w_o_a[l], w_o_b[l], w_out[l], g_norm_ffn[l], w_gate[l], w_up[l], w_down[l])
        ada_p = ada_terms(c_prompt, w_ada[l], b_ada[l])
        h = modulate(rmsnorm(xp, g_norm_mix[l]), ada_p[0], ada_p[1])
        q_a, lat, kr, q_b, k_b, v_b, gates = mixer_inputs(h, pos_p, *mix_w)
        k_a, v_a = mla_keys(lat, kr, w_kv_up[l], g_kn_a[l])
        o_a = mla_prompt(q_a, k_a, v_a, pos_p)
        o_b = band_prompt(q_b, k_b, v_b, rel_bias[l])
        xp = layer_tail(xp, ada_p, o_a, o_b, gates, *out_w)
        lat_p.append(lat)
        kr_p.append(kr)
        bk_p.append(k_b[:, S - keep:])
        bv_p.append(v_b[:, S - keep:])
        ada_s = ada_terms(c_sample, w_ada[l], b_ada[l])
        h = modulate(rmsnorm(xs, g_norm_mix[l]), ada_s[0], ada_s[1])
        q_a, lat, kr, q_b, k_b, v_b, gates = mixer_inputs(h, pos_s, *mix_w)
        k_a, v_a = mla_keys(jnp.concatenate([cache_kv_latent[l], lat], axis=1),
                            jnp.concatenate([cache_k_rope[l], kr], axis=1), w_kv_up[l], g_kn_a[l])
        o_a = softmax_attend(q_a, k_a, v_a, 0.0)
        o_b = band_sample(q_b, k_b, v_b, cache_band_k[l], cache_band_v[l], rel_bias[l], past_len)
        xs = layer_tail(xs, ada_s, o_a, o_b, gates, *out_w)
        lat_s.append(lat)
        kr_s.append(kr)
        bk_s.append(k_b)
        bv_s.append(v_b)
    return (xp, xs, jnp.stack(lat_p), jnp.stack(kr_p), jnp.stack(bk_p), jnp.stack(bv_p),
            jnp.stack(lat_s), jnp.stack(kr_s), jnp.stack(bk_s), jnp.stack(bv_s))
```

```python
import functools

import jax
import jax.numpy as jnp
import numpy as np
from jax import lax
from jax.experimental import pallas as pl
from jax.experimental.pallas import tpu as pltpu

D_MODEL = 1024
CHUNK = 64
EPS = 1e-6
NEG_INF = -1e30
N_HEADS_A = 8
NOPE_DIM = 64
ROPE_DIM = 32
HALF_ROPE = ROPE_DIM // 2
V_DIM_A = 64
QK_DIM_A = NOPE_DIM + ROPE_DIM
Q_LORA = 384
KV_LORA = 256
ROPE_BASE = 10000.0
N_HEADS_B = 8
HEAD_DIM_B = 64
LEFT_CHUNKS = 8
BAND_WINDOW = LEFT_CHUNKS * CHUNK
REL_CLIP = 128
D_FF = -(-(8 * D_MODEL) // (3 * 256)) * 256
COL_QKV_B = 3 * N_HEADS_B * HEAD_DIM_B
N_GATE_COLS = 2 * D_MODEL

LANES = 128
HEAD_BLOCK = LANES
ROPE_LANE0 = NOPE_DIM
QA_COLS = N_HEADS_A * HEAD_BLOCK
VA_COLS = N_HEADS_A * V_DIM_A
B_COLS = N_HEADS_B * HEAD_DIM_B
MIX_COLS = Q_LORA + KV_LORA + LANES + COL_QKV_B
OFF_CKV = Q_LORA
OFF_KPE = Q_LORA + KV_LORA
OFF_QB = OFF_KPE + LANES
OFF_KB = OFF_QB + B_COLS
OFF_VB = OFF_KB + B_COLS
FF_CHUNK = 256
VMEM_LIMIT = 56 * 1024 * 1024

BF16 = jnp.bfloat16
F32 = jnp.float32


def _cparams(n_axes):
    return pltpu.CompilerParams(dimension_semantics=("arbitrary",) * n_axes, vmem_limit_bytes=VMEM_LIMIT)


def _const_spec(shape):
    nd = len(shape)
    return pl.BlockSpec(shape, lambda *_: (0,) * nd, pipeline_mode=pl.Buffered(1))


def _dot(a, b):
    return jnp.dot(a, b, preferred_element_type=F32)


def _dot_nt(a, b):
    return lax.dot_general(a, b, (((1,), (1,)), ((), ())), preferred_element_type=F32)


def _pad_rows(x, rows):
    return jnp.concatenate([x, jnp.zeros((rows - x.shape[0], x.shape[1]), x.dtype)], axis=0)


def _row_rms(x, inv_n):
    return lax.rsqrt(jnp.sum(x * x, axis=-1, keepdims=True) * inv_n + EPS)


def _group_rms(x, g_ref, e2_ref, invcnt_ref):
    s = _dot((x * x).astype(BF16), g_ref[...])
    r = lax.rsqrt(s * invcnt_ref[...] + EPS)
    r_hi = r.astype(BF16)
    r_lo = (r - r_hi.astype(F32)).astype(BF16)
    return _dot(jnp.concatenate([r_hi, r_lo], axis=1), e2_ref[...])


def _rope_block(x, cos_t, sin_a, sin_b):
    return x * cos_t + pltpu.roll(x, LANES - HALF_ROPE, 1) * sin_a + pltpu.roll(x, HALF_ROPE, 1) * sin_b


def _ada_kernel(c_ref, w_ref, b_ref, o_ref):
    c = c_ref[...]
    a = (c * jax.nn.sigmoid(c)).astype(BF16)
    o_ref[...] = _dot(a, w_ref[...].astype(BF16)) + b_ref[...]


def _ada_call(c_all, w_ada, b_ada):
    rows = c_all.shape[0]
    n_out = w_ada.shape[1]
    tn = D_MODEL
    return pl.pallas_call(
        _ada_kernel,
        grid=(n_out // tn,),
        in_specs=[pl.BlockSpec((rows, D_MODEL), lambda n: (0, 0)),
                  pl.BlockSpec((D_MODEL, tn), lambda n: (0, n)),
                  pl.BlockSpec((1, tn), lambda n: (0, n))],
        out_specs=pl.BlockSpec((rows, tn), lambda n: (0, n)),
        out_shape=jax.ShapeDtypeStruct((rows, n_out), F32),
        compiler_params=_cparams(1),
        name="ada",
    )(c_all, w_ada, b_ada.reshape(1, n_out))


def _mixer_in_kernel(first_tail,
                     x_ref, ada_ref, cos_ref, sina_ref, sinb_ref, gmix_ref, w_in_ref, gql_ref, wq_ref, gkv_ref,
                     gq_row_ref, gkr_row_ref, gqb_row_ref, gkb_row_ref,
                     gq_ref, e2q_ref, icq_ref, gb_ref, e2b_ref, icb_ref,
                     qa_ref, lat_ref, kr_ref, qb_ref, kb_ref, vb_ref, kbt_ref, vbt_ref):
    x = x_ref[0]
    shift = ada_ref[0, :, 0:D_MODEL]
    scale = ada_ref[0, :, D_MODEL:2 * D_MODEL]
    h = x * _row_rms(x, 1.0 / D_MODEL) * gmix_ref[...]
    h = h * (1.0 + scale) + shift
    z = _dot(h.astype(BF16), w_in_ref[...])

    cos_t = cos_ref[...]
    sin_a = sina_ref[...]
    sin_b = sinb_ref[...]

    c_q = z[:, 0:Q_LORA]
    cqn = c_q * _row_rms(c_q, 1.0 / Q_LORA) * gql_ref[...]
    q_raw = _dot(cqn.astype(BF16), wq_ref[...])
    qn = q_raw * _group_rms(q_raw, gq_ref, e2q_ref, icq_ref) * (gq_row_ref[...] * (QK_DIM_A ** -0.5))
    for hd in range(N_HEADS_A):
        lo = hd * HEAD_BLOCK
        qa_ref[0, :, lo:lo + HEAD_BLOCK] = _rope_block(qn[:, lo:lo + HEAD_BLOCK], cos_t, sin_a, sin_b).astype(BF16)

    c_kv = z[:, OFF_CKV:OFF_CKV + KV_LORA]
    lat_ref[0] = c_kv * _row_rms(c_kv, 1.0 / KV_LORA) * gkv_ref[...]

    kp = z[:, OFF_KPE:OFF_KPE + LANES]
    kr = _rope_block(kp * _row_rms(kp, 1.0 / ROPE_DIM) * gkr_row_ref[...], cos_t, sin_a, sin_b)
    kr_ref[0] = pltpu.roll(kr, LANES - ROPE_LANE0, 1)[:, 0:ROPE_DIM]

    zq = z[:, OFF_QB:OFF_QB + B_COLS]
    qb_ref[0] = (zq * _group_rms(zq, gb_ref, e2b_ref, icb_ref)
                 * (gqb_row_ref[...] * (HEAD_DIM_B ** -0.5))).astype(BF16)
    zk = z[:, OFF_KB:OFF_KB + B_COLS]
    k_b = zk * _group_rms(zk, gb_ref, e2b_ref, icb_ref) * gkb_row_ref[...]
    v_b = z[:, OFF_VB:OFF_VB + B_COLS]
    kb_ref[0] = k_b.astype(BF16)
    vb_ref[0] = v_b.astype(BF16)

    @pl.when(pl.program_id(1) >= first_tail)
    def _():
        kbt_ref[0] = k_b
        vbt_ref[0] = v_b


def _mixer_in_call(x, ada, tables, consts, weights, tm, keep):
    nb, sb, _ = x.shape
    nj = sb // tm
    first_tail = (sb - keep) // tm
    ada_rows = ada.shape[1]
    if ada_rows == 1:
        ada_spec = pl.BlockSpec((1, 1, 6 * D_MODEL), lambda b, j: (b, 0, 0))
    else:
        ada_spec = pl.BlockSpec((1, tm, 6 * D_MODEL), lambda b, j: (b, j, 0))
    tab_spec = pl.BlockSpec((tm, LANES), lambda b, j: (j, 0))
    tok = lambda c: pl.BlockSpec((1, tm, c), lambda b, j: (b, j, 0))
    tail = pl.BlockSpec((1, tm, B_COLS), lambda b, j: (b, jnp.maximum(j - first_tail, 0), 0))
    const_in = [weights["g_mix"], weights["w_in_mix"], weights["g_q_lora"], weights["w_q_up"], weights["g_kv_lora"],
                weights["gq_row"], weights["gkr_row"], weights["gqb_row"], weights["gkb_row"],
                consts["g_q"], consts["e2_q"], consts["ic_q"], consts["g_b"], consts["e2_b"], consts["ic_b"]]
    out_shape = [jax.ShapeDtypeStruct((nb, sb, QA_COLS), BF16),
                 jax.ShapeDtypeStruct((nb, sb, KV_LORA), F32),
                 jax.ShapeDtypeStruct((nb, sb, ROPE_DIM), F32),
                 jax.ShapeDtypeStruct((nb, sb, B_COLS), BF16),
                 jax.ShapeDtypeStruct((nb, sb, B_COLS), BF16),
                 jax.ShapeDtypeStruct((nb, sb, B_COLS), BF16),
                 jax.ShapeDtypeStruct((nb, keep, B_COLS), F32),
                 jax.ShapeDtypeStruct((nb, keep, B_COLS), F32)]
    return pl.pallas_call(
        functools.partial(_mixer_in_kernel, first_tail),
        grid=(nb, nj),
        in_specs=[tok(D_MODEL), ada_spec, tab_spec, tab_spec, tab_spec] + [_const_spec(a.shape) for a in const_in],
        out_specs=[tok(QA_COLS), tok(KV_LORA), tok(ROPE_DIM), tok(B_COLS), tok(B_COLS), tok(B_COLS), tail, tail],
        out_shape=out_shape,
        compiler_params=_cparams(2),
        name="mixer_in",
    )(x, ada, tables[0], tables[1], tables[2], *const_in)


def _kv_expand_kernel(lat_ref, kr_ref, wk_ref, wv_ref, gk_row_ref, gk_ref, e2k_ref, ick_ref, place_ref, k_ref, v_ref):
    lat = lat_ref[...].astype(BF16)
    kn_raw = _dot(lat, wk_ref[...])
    v_ref[...] = _dot(lat, wv_ref[...]).astype(BF16)
    kn = kn_raw * _group_rms(kn_raw, gk_ref, e2k_ref, ick_ref) * gk_row_ref[...]
    kr = kr_ref[...]
    kr_hi = kr.astype(BF16)
    kr_lo = (kr - kr_hi.astype(F32)).astype(BF16)
    placed = _dot(jnp.concatenate([kr_hi, kr_lo], axis=1), place_ref[...])
    k_ref[...] = (kn + placed).astype(BF16)


def _kv_expand_call(latent, k_rope, consts, weights, tm):
    n = latent.shape[0]
    const_in = [weights["w_k_up"], weights["w_v_up"], weights["gk_row"],
                consts["g_k"], consts["e2_k"], consts["ic_k"], consts["place2"]]
    return pl.pallas_call(
        _kv_expand_kernel,
        grid=(n // tm,),
        in_specs=[pl.BlockSpec((tm, KV_LORA), lambda i: (i, 0)),
                  pl.BlockSpec((tm, ROPE_DIM), lambda i: (i, 0))] + [_const_spec(a.shape) for a in const_in],
        out_specs=[pl.BlockSpec((tm, QA_COLS), lambda i: (i, 0)), pl.BlockSpec((tm, VA_COLS), lambda i: (i, 0))],
        out_shape=[jax.ShapeDtypeStruct((n, QA_COLS), BF16), jax.ShapeDtypeStruct((n, VA_COLS), BF16)],
        compiler_params=_cparams(1),
        name="kv_expand",
    )(latent, k_rope, *const_in)


def _softmax_update(s, v, carry):
    m, l, acc = carry
    m_new = jnp.maximum(m, jnp.max(s, axis=-1, keepdims=True))
    alpha = jnp.exp(m - m_new)
    p = jnp.exp(s - m_new)
    l_new = alpha * l + jnp.sum(p, axis=-1, keepdims=True)
    acc_new = alpha * acc + _dot(p.astype(BF16), v)
    return m_new, l_new, acc_new


def _mla_prompt_kernel(tq, q_ref, k_ref, v_ref, o_ref):
    seq = q_ref.shape[1]
    lane = lax.broadcasted_iota(jnp.int32, (tq, LANES), 1)
    row_c = lax.broadcasted_iota(jnp.int32, (tq, tq), 0) // CHUNK
    col_c = lax.broadcasted_iota(jnp.int32, (tq, tq), 1) // CHUNK
    diag_ok = row_c >= col_c

    def q_body(qi, _):
        q0 = pl.multiple_of(qi * tq, tq)
        outs = []
        for hd in range(2):
            lo = hd * HEAD_BLOCK
            q = q_ref[0, pl.ds(q0, tq), lo:lo + HEAD_BLOCK]

            def kv_body(kj, carry):
                k0 = pl.multiple_of(kj * tq, tq)
                s = _dot_nt(q, k_ref[0, pl.ds(k0, tq), lo:lo + HEAD_BLOCK])
                return _softmax_update(s, v_ref[0, pl.ds(k0, tq), :], carry)

            init = (jnp.full((tq, 1), NEG_INF, F32), jnp.zeros((tq, 1), F32), jnp.zeros((tq, LANES), F32))
            carry = lax.fori_loop(0, qi, kv_body, init)
            s = _dot_nt(q, k_ref[0, pl.ds(q0, tq), lo:lo + HEAD_BLOCK])
            s = jnp.where(diag_ok, s, NEG_INF)
            _, l, acc = _softmax_update(s, v_ref[0, pl.ds(q0, tq), :], carry)
            outs.append(acc / l)
        o_ref[0, pl.ds(q0, tq), :] = jnp.where(lane < V_DIM_A, outs[0], outs[1]).astype(BF16)
        return 0

    lax.fori_loop(0, seq // tq, q_body, 0)


def _mla_prompt_call(q_a, k_a, v_a, tq):
    nb, seq, _ = q_a.shape
    n_pairs = N_HEADS_A // 2
    return pl.pallas_call(
        functools.partial(_mla_prompt_kernel, tq),
        grid=(nb, n_pairs),
        in_specs=[pl.BlockSpec((1, seq, 2 * HEAD_BLOCK), lambda b, p: (b, 0, p)),
                  pl.BlockSpec((1, seq, 2 * HEAD_BLOCK), lambda b, p: (b, 0, p)),
                  pl.BlockSpec((1, seq, 2 * V_DIM_A), lambda b, p: (b, 0, p))],
        out_specs=pl.BlockSpec((1, seq, 2 * V_DIM_A), lambda b, p: (b, 0, p)),
        out_shape=jax.ShapeDtypeStruct((nb, seq, VA_COLS), BF16),
        compiler_params=_cparams(2),
        name="mla_prompt",
    )(q_a, k_a, v_a)


def _mla_sample_kernel(q_ref, kc_ref, vc_ref, kn_ref, vn_ref, o_ref):
    rows = q_ref.shape[1]
    lane = lax.broadcasted_iota(jnp.int32, (rows, LANES), 1)
    vn = _pad_rows(vn_ref[0], LANES)
    outs = []
    for hd in range(2):
        lo = hd * HEAD_BLOCK
        q = q_ref[0, :, lo:lo + HEAD_BLOCK]
        s_c = _dot_nt(q, kc_ref[0, :, lo:lo + HEAD_BLOCK])
        s_n = jnp.where(lane < rows, _dot_nt(q, _pad_rows(kn_ref[0, :, lo:lo + HEAD_BLOCK], LANES)), NEG_INF)
        m = jnp.maximum(jnp.max(s_c, axis=-1, keepdims=True), jnp.max(s_n, axis=-1, keepdims=True))
        p_c = jnp.exp(s_c - m)
        p_n = jnp.exp(s_n - m)
        l = jnp.sum(p_c, axis=-1, keepdims=True) + jnp.sum(p_n, axis=-1, keepdims=True)
        acc = _dot(p_c.astype(BF16), vc_ref[0]) + _dot(p_n.astype(BF16), vn)
        outs.append(acc / l)
    o_ref[0] = jnp.where(lane < V_DIM_A, outs[0], outs[1]).astype(BF16)


def _mla_sample_call(q_a, k_cache, v_cache, k_new, v_new):
    nb, rows, _ = q_a.shape
    past = k_cache.shape[1]
    n_pairs = N_HEADS_A // 2
    kspec = lambda s: pl.BlockSpec((1, s, 2 * HEAD_BLOCK), lambda b, p: (b, 0, p))
    vspec = lambda s: pl.BlockSpec((1, s, 2 * V_DIM_A), lambda b, p: (b, 0, p))
    return pl.pallas_call(
        _mla_sample_kernel,
        grid=(nb, n_pairs),
        in_specs=[kspec(rows), kspec(past), vspec(past), kspec(rows), vspec(rows)],
        out_specs=vspec(rows),
        out_shape=jax.ShapeDtypeStruct((nb, rows, VA_COLS), BF16),
        compiler_params=_cparams(2),
        name="mla_sample",
    )(q_a, k_cache, v_cache, k_new, v_new)


def _toeplitz_bias(g0, rows):
    far = g0[:, 0:1]
    x0 = jnp.broadcast_to(g0[:, 0:LANES], (rows, LANES))
    x1 = jnp.broadcast_to(g0[:, LANES:2 * LANES], (rows, LANES))
    row = lax.broadcasted_iota(jnp.int32, (rows, LANES), 0)
    lane = lax.broadcasted_iota(jnp.int32, (rows, LANES), 1)
    step = 1
    while step < rows:
        r0 = pltpu.roll(x0, step, 1)
        r1 = pltpu.roll(x1, step, 1)
        keep = lane >= step
        take = (row & step) != 0
        x0, x1 = jnp.where(take, jnp.where(keep, r0, r1), x0), jnp.where(take, jnp.where(keep, r1, r0), x1)
        step *= 2
    return jnp.where(lane < row, far, x0), x1, far


def _band_prompt_kernel(tq, q_ref, k_ref, v_ref, rb_ref, o_ref, bias_ref, kpad_ref, vpad_ref):
    seq = q_ref.shape[1]
    win = BAND_WINDOW + tq
    n_far = (win - 2 * LANES) // LANES
    pair = pl.program_id(1)

    @pl.when((pl.program_id(0) == 0) & (pair == 0))
    def _():
        row_c = lax.broadcasted_iota(jnp.int32, (tq, LANES), 0) // CHUNK
        lane = lax.broadcasted_iota(jnp.int32, (tq, LANES), 1)
        for hd in range(N_HEADS_B):
            tw0, tw1, far = _toeplitz_bias(rb_ref[hd:hd + 1, :], tq)
            blocks = [jnp.broadcast_to(far, (tq, LANES))] * n_far + [tw0, tw1]
            for cb, blk in enumerate(blocks):
                col_c = (cb * LANES + lane) // CHUNK
                ok = (col_c >= row_c) & (col_c <= row_c + LEFT_CHUNKS)
                bias_ref[hd, :, cb * LANES:(cb + 1) * LANES] = jnp.where(ok, blk, NEG_INF)

    kpad_ref[0:BAND_WINDOW, :] = jnp.zeros((BAND_WINDOW, LANES), BF16)
    vpad_ref[0:BAND_WINDOW, :] = jnp.zeros((BAND_WINDOW, LANES), BF16)
    kpad_ref[BAND_WINDOW:BAND_WINDOW + seq, :] = k_ref[0]
    vpad_ref[BAND_WINDOW:BAND_WINDOW + seq, :] = v_ref[0]

    lane_q = lax.broadcasted_iota(jnp.int32, (tq, LANES), 1)
    col = lax.broadcasted_iota(jnp.int32, (tq, win), 1)

    def t_body(t, _):
        t0 = pl.multiple_of(t * tq, tq)
        q = q_ref[0, pl.ds(t0, tq), :]
        kw = kpad_ref[pl.ds(t0, win), :]
        vw = vpad_ref[pl.ds(t0, win), :]
        in_seq = col + (t0 - BAND_WINDOW) >= 0
        outs = []
        for hd in range(2):
            head_lanes = (lane_q < HEAD_DIM_B) if hd == 0 else (lane_q >= HEAD_DIM_B)
            qh = jnp.where(head_lanes, q, jnp.zeros_like(q))
            s = _dot_nt(qh, kw) + bias_ref[2 * pair + hd]
            s = jnp.where(in_seq, s, NEG_INF)
            p = jnp.exp(s - jnp.max(s, axis=-1, keepdims=True))
            l = jnp.sum(p, axis=-1, keepdims=True)
            outs.append(_dot(p.astype(BF16), vw) / l)
        o_ref[0, pl.ds(t0, tq), :] = jnp.where(lane_q < HEAD_DIM_B, outs[0], outs[1]).astype(BF16)
        return 0

    lax.fori_loop(0, seq // tq, t_body, 0)


def _band_prompt_call(q_b, k_b, v_b, rb_rev, tq):
    nb, seq, _ = q_b.shape
    n_pairs = N_HEADS_B // 2
    win = BAND_WINDOW + tq
    spec = pl.BlockSpec((1, seq, LANES), lambda b, p: (b, 0, p))
    return pl.pallas_call(
        functools.partial(_band_prompt_kernel, tq),
        grid=(nb, n_pairs),
        in_specs=[spec, spec, spec, _const_spec(rb_rev.shape)],
        out_specs=spec,
        out_shape=jax.ShapeDtypeStruct((nb, seq, B_COLS), BF16),
        scratch_shapes=[pltpu.VMEM((N_HEADS_B, tq, win), F32),
                        pltpu.VMEM((BAND_WINDOW + seq, LANES), BF16),
                        pltpu.VMEM((BAND_WINDOW + seq, LANES), BF16)],
        compiler_params=_cparams(2),
        name="band_prompt",
    )(q_b, k_b, v_b, rb_rev)


def _band_sample_kernel(q_ref, kc_ref, vc_ref, kn_ref, vn_ref, rb_ref, o_ref):
    rows = q_ref.shape[1]
    n_cache = kc_ref.shape[1]
    pair = pl.program_id(1)
    lane_q = lax.broadcasted_iota(jnp.int32, (rows, LANES), 1)
    q = q_ref[0]
    kc = kc_ref[0].astype(BF16)
    vc = vc_ref[0].astype(BF16)
    kn = _pad_rows(kn_ref[0], LANES)
    vn = _pad_rows(vn_ref[0], LANES)
    outs = []
    for hd in range(2):
        tw0, tw1, far = _toeplitz_bias(rb_ref[pl.ds(2 * pair + hd, 1), :], rows)
        head_lanes = (lane_q < HEAD_DIM_B) if hd == 0 else (lane_q >= HEAD_DIM_B)
        qh = jnp.where(head_lanes, q, jnp.zeros_like(q))
        bias_c = jnp.concatenate([jnp.broadcast_to(far, (rows, n_cache - LANES)), tw0], axis=1)
        s_c = _dot_nt(qh, kc) + bias_c
        s_n = jnp.where(lane_q < rows, _dot_nt(qh, kn) + tw1, NEG_INF)
        m = jnp.maximum(jnp.max(s_c, axis=-1, keepdims=True), jnp.max(s_n, axis=-1, keepdims=True))
        p_c = jnp.exp(s_c - m)
        p_n = jnp.exp(s_n - m)
        l = jnp.sum(p_c, axis=-1, keepdims=True) + jnp.sum(p_n, axis=-1, keepdims=True)
        outs.append((_dot(p_c.astype(BF16), vc) + _dot(p_n.astype(BF16), vn)) / l)
    o_ref[0] = jnp.where(lane_q < HEAD_DIM_B, outs[0], outs[1]).astype(BF16)


def _band_sample_call(q_b, k_cache, v_cache, k_new, v_new, rb_rev):
    nb, rows, _ = q_b.shape
    n_cache = k_cache.shape[1]
    n_pairs = N_HEADS_B // 2
    spec = lambda s: pl.BlockSpec((1, s, LANES), lambda b, p: (b, 0, p))
    return pl.pallas_call(
        _band_sample_kernel,
        grid=(nb, n_pairs),
        in_specs=[spec(rows), spec(n_cache), spec(n_cache), spec(rows), spec(rows), _const_spec(rb_rev.shape)],
        out_specs=spec(rows),
        out_shape=jax.ShapeDtypeStruct((nb, rows, B_COLS), BF16),
        compiler_params=_cparams(2),
        name="band_sample",
    )(q_b, k_cache, v_cache, k_new, v_new, rb_rev)


def _tail_kernel(x_ref, oa_ref, ob_ref, ada_ref, gmix_ref, wg_ref, woa_ref, wob_ref, wout_ref, gffn_ref,
                 wgate_ref, wup_ref, wdown_ref, y_ref):
    x = x_ref[0]
    ada = lambda k: ada_ref[0, :, k * D_MODEL:(k + 1) * D_MODEL]
    h = x * _row_rms(x, 1.0 / D_MODEL) * gmix_ref[...]
    h = (h * (1.0 + ada(1)) + ada(0)).astype(BF16)
    gates = jax.nn.sigmoid(_dot(h, wg_ref[...]))
    y_a = _dot(oa_ref[0], woa_ref[...])
    y_b = _dot(ob_ref[0], wob_ref[...])
    mixed = gates[:, 0:D_MODEL] * y_a + gates[:, D_MODEL:2 * D_MODEL] * y_b
    x1 = x + ada(2) * _dot(mixed.astype(BF16), wout_ref[...])
    h2 = x1 * _row_rms(x1, 1.0 / D_MODEL) * gffn_ref[...]
    h2 = (h2 * (1.0 + ada(4)) + ada(3)).astype(BF16)
    acc = jnp.zeros_like(x1)
    for c in range(D_FF // FF_CHUNK):
        cols = slice(c * FF_CHUNK, (c + 1) * FF_CHUNK)
        g = _dot(h2, wgate_ref[:, cols])
        u = _dot(h2, wup_ref[:, cols])
        act = (g * jax.nn.sigmoid(g) * u).astype(BF16)
        acc = acc + _dot(act, wdown_ref[cols, :])
    y_ref[0] = x1 + ada(5) * acc


def _tail_call(x, o_a, o_b, ada, weights, tm):
    nb, sb, _ = x.shape
    if ada.shape[1] == 1:
        ada_spec = pl.BlockSpec((1, 1, 6 * D_MODEL), lambda b, j: (b, 0, 0))
    else:
        ada_spec = pl.BlockSpec((1, tm, 6 * D_MODEL), lambda b, j: (b, j, 0))
    tok = lambda c: pl.BlockSpec((1, tm, c), lambda b, j: (b, j, 0))
    const_in = [weights["g_mix"], weights["w_in_gate"], weights["w_o_a"], weights["w_o_b"], weights["w_out"],
                weights["g_ffn"], weights["w_gate"], weights["w_up"], weights["w_down"]]
    return pl.pallas_call(
        _tail_kernel,
        grid=(nb, sb // tm),
        in_specs=[tok(D_MODEL), tok(VA_COLS), tok(B_COLS), ada_spec] + [_const_spec(a.shape) for a in const_in],
        out_specs=tok(D_MODEL),
        out_shape=jax.ShapeDtypeStruct((nb, sb, D_MODEL), F32),
        compiler_params=_cparams(2),
        name="tail",
    )(x, o_a, o_b, ada, *const_in)


def _group_constants():
    def pack(g, inv_cnt):
        e = g.T
        ic = np.ones((1, LANES), np.float32)
        ic[0, :len(inv_cnt)] = inv_cnt
        return jnp.asarray(g, BF16), jnp.asarray(np.concatenate([e, e], axis=0), BF16), jnp.asarray(ic)

    g_q = np.zeros((QA_COLS, LANES), np.float32)
    g_k = np.zeros((QA_COLS, LANES), np.float32)
    for hd in range(N_HEADS_A):
        g_q[hd * HEAD_BLOCK:hd * HEAD_BLOCK + NOPE_DIM, hd] = 1.0
        g_q[hd * HEAD_BLOCK + ROPE_LANE0:hd * HEAD_BLOCK + ROPE_LANE0 + ROPE_DIM, N_HEADS_A + hd] = 1.0
        g_k[hd * HEAD_BLOCK:hd * HEAD_BLOCK + NOPE_DIM, hd] = 1.0
    g_b = np.zeros((B_COLS, LANES), np.float32)
    for hd in range(N_HEADS_B):
        g_b[hd * HEAD_DIM_B:(hd + 1) * HEAD_DIM_B, hd] = 1.0
    place = np.zeros((ROPE_DIM, QA_COLS), np.float32)
    for hd in range(N_HEADS_A):
        for i in range(ROPE_DIM):
            place[i, hd * HEAD_BLOCK + ROPE_LANE0 + i] = 1.0
    c = {}
    c["g_q"], c["e2_q"], c["ic_q"] = pack(g_q, [1.0 / NOPE_DIM] * N_HEADS_A + [1.0 / ROPE_DIM] * N_HEADS_A)
    c["g_k"], c["e2_k"], c["ic_k"] = pack(g_k, [1.0 / NOPE_DIM] * N_HEADS_A)
    c["g_b"], c["e2_b"], c["ic_b"] = pack(g_b, [1.0 / HEAD_DIM_B] * N_HEADS_B)
    c["place2"] = jnp.asarray(np.concatenate([place, place], axis=0), BF16)
    return c


def _rope_tables(pos):
    inv_freq = ROPE_BASE ** (-jnp.arange(HALF_ROPE, dtype=F32) / HALF_ROPE)
    ang = pos.astype(F32)[:, None] * inv_freq[None, :]
    cos, sin = jnp.cos(ang), jnp.sin(ang)
    n = pos.shape[0]
    ones = jnp.ones((n, ROPE_LANE0), F32)
    zeros = jnp.zeros((n, ROPE_LANE0), F32)
    pad1 = jnp.ones((n, LANES - ROPE_LANE0 - ROPE_DIM), F32)
    pad0 = jnp.zeros((n, LANES - ROPE_LANE0 - ROPE_DIM), F32)
    z16 = jnp.zeros((n, HALF_ROPE), F32)
    cos_t = jnp.concatenate([ones, cos, cos, pad1], axis=1)
    sin_a = jnp.concatenate([zeros, -sin, z16, pad0], axis=1)
    sin_b = jnp.concatenate([zeros, z16, sin, pad0], axis=1)
    return cos_t, sin_a, sin_b


def _layer_weights(l, w_in, g_norm_mix, g_q_lora, w_q_up, g_kv_lora, w_kv_up, g_qn_a, g_kn_a, g_qr_a, g_kr_a,
                   g_q_b, g_k_b, w_o_a, w_o_b, w_out, g_norm_ffn, w_gate, w_up, w_down):
    wi = w_in[l]
    kpe_block = jnp.pad(wi[:, OFF_KPE:OFF_KPE + ROPE_DIM], ((0, 0), (ROPE_LANE0, LANES - ROPE_LANE0 - ROPE_DIM)))
    n_mix = OFF_KPE + ROPE_DIM + COL_QKV_B
    w = {}
    w["w_in_mix"] = jnp.concatenate([wi[:, 0:OFF_KPE], kpe_block, wi[:, OFF_KPE + ROPE_DIM:n_mix]], axis=1).astype(BF16)
    w["w_in_gate"] = wi[:, n_mix:n_mix + N_GATE_COLS].astype(BF16)
    wq3 = w_q_up[l].reshape(Q_LORA, N_HEADS_A, QK_DIM_A)
    w["w_q_up"] = jnp.pad(wq3, ((0, 0), (0, 0), (0, HEAD_BLOCK - QK_DIM_A))).reshape(Q_LORA, QA_COLS).astype(BF16)
    wkv3 = w_kv_up[l].reshape(KV_LORA, N_HEADS_A, NOPE_DIM + V_DIM_A)
    w["w_k_up"] = jnp.pad(wkv3[..., :NOPE_DIM], ((0, 0), (0, 0), (0, HEAD_BLOCK - NOPE_DIM))).reshape(KV_LORA, QA_COLS).astype(BF16)
    w["w_v_up"] = wkv3[..., NOPE_DIM:].reshape(KV_LORA, VA_COLS).astype(BF16)
    zpad = jnp.zeros((HEAD_BLOCK - QK_DIM_A,), F32)
    w["gq_row"] = jnp.tile(jnp.concatenate([g_qn_a[l], g_qr_a[l], zpad]), N_HEADS_A).reshape(1, QA_COLS)
    w["gk_row"] = jnp.tile(jnp.concatenate([g_kn_a[l], jnp.zeros((HEAD_BLOCK - NOPE_DIM,), F32)]), N_HEADS_A).reshape(1, QA_COLS)
    w["gkr_row"] = jnp.concatenate([jnp.zeros((ROPE_LANE0,), F32), g_kr_a[l], zpad]).reshape(1, LANES)
    w["gqb_row"] = jnp.tile(g_q_b[l], N_HEADS_B).reshape(1, B_COLS)
    w["gkb_row"] = jnp.tile(g_k_b[l], N_HEADS_B).reshape(1, B_COLS)
    w["g_mix"] = g_norm_mix[l].reshape(1, D_MODEL)
    w["g_q_lora"] = g_q_lora[l].reshape(1, Q_LORA)
    w["g_kv_lora"] = g_kv_lora[l].reshape(1, KV_LORA)
    w["g_ffn"] = g_norm_ffn[l].reshape(1, D_MODEL)
    w["w_o_a"] = w_o_a[l].astype(BF16)
    w["w_o_b"] = w_o_b[l].astype(BF16)
    w["w_out"] = w_out[l].astype(BF16)
    w["w_gate"] = w_gate[l].astype(BF16)
    w["w_up"] = w_up[l].astype(BF16)
    w["w_down"] = w_down[l].astype(BF16)
    return w


def kernel(x_prompt, x_sample, c_prompt, c_sample, cache_kv_latent, cache_k_rope, cache_band_k, cache_band_v, w_ada, b_ada, g_norm_mix, w_in, g_q_lora, w_q_up, g_kv_lora, w_kv_up, g_qn_a, g_kn_a, g_qr_a, g_kr_a, g_q_b, g_k_b, rel_bias, w_o_a, w_o_b, w_out, g_norm_ffn, w_gate, w_up, w_down):
    depth = w_in.shape[0]
    nb, seq, _ = x_prompt.shape
    nbs, sd, _ = x_sample.shape
    past = cache_kv_latent.shape[2]
    n_buf = cache_band_k.shape[2]
    keep = min(BAND_WINDOW, seq)
    assert depth == 1 and nbs * sd == LANES and n_buf == BAND_WINDOW and seq % 512 == 0 and past % 512 == 0
    tm = 512
    rows_s = nbs * sd

    consts = _group_constants()
    tab_p = _rope_tables(jnp.arange(seq))
    tab_s = _rope_tables(past + (jnp.arange(rows_s) % sd))
    xs = x_sample.reshape(1, rows_s, D_MODEL)

    l = 0
    wts = _layer_weights(l, w_in, g_norm_mix, g_q_lora, w_q_up, g_kv_lora, w_kv_up, g_qn_a, g_kn_a, g_qr_a, g_kr_a,
                         g_q_b, g_k_b, w_o_a, w_o_b, w_out, g_norm_ffn, w_gate, w_up, w_down)
    rb_rev = rel_bias[l][:, 2 * REL_CLIP:0:-1]

    ada = _ada_call(jnp.concatenate([c_prompt, c_sample], axis=0), w_ada[l], b_ada[l])
    ada_p = ada[:nb].reshape(nb, 1, 6 * D_MODEL)
    ada_s = jnp.repeat(ada[nb:], sd, axis=0).reshape(1, rows_s, 6 * D_MODEL)

    qa, lat, kr, qb, kb, vb, kb_tail, vb_tail = _mixer_in_call(x_prompt, ada_p, tab_p, consts, wts, tm, keep)
    k_a, v_a = _kv_expand_call(lat.reshape(nb * seq, KV_LORA), kr.reshape(nb * seq, ROPE_DIM), consts, wts, tm)
    o_a = _mla_prompt_call(qa, k_a.reshape(nb, seq, QA_COLS), v_a.reshape(nb, seq, VA_COLS), 256)
    o_b = _band_prompt_call(qb, kb, vb, rb_rev, 128)
    y_p = _tail_call(x_prompt, o_a, o_b, ada_p, wts, tm)

    qa_s, lat_s, kr_s, qb_s, kb_s, vb_s, kb_s32, vb_s32 = _mixer_in_call(xs, ada_s, tab_s, consts, wts, rows_s, rows_s)
    kc, vc = _kv_expand_call(cache_kv_latent[l].reshape(nbs * past, KV_LORA),
                             cache_k_rope[l].reshape(nbs * past, ROPE_DIM), consts, wts, tm)
    kn, vn = _kv_expand_call(lat_s.reshape(rows_s, KV_LORA), kr_s.reshape(rows_s, ROPE_DIM), consts, wts, rows_s)
    o_a_s = _mla_sample_call(qa_s.reshape(nbs, sd, QA_COLS), kc.reshape(nbs, past, QA_COLS),
                             vc.reshape(nbs, past, VA_COLS), kn.reshape(nbs, sd, QA_COLS), vn.reshape(nbs, sd, VA_COLS))
    o_b_s = _band_sample_call(qb_s.reshape(nbs, sd, B_COLS), cache_band_k[l].reshape(nbs, n_buf, B_COLS),
                              cache_band_v[l].reshape(nbs, n_buf, B_COLS), kb_s.reshape(nbs, sd, B_COLS),
                              vb_s.reshape(nbs, sd, B_COLS), rb_rev)
    y_s = _tail_call(xs, o_a_s.reshape(1, rows_s, VA_COLS), o_b_s.reshape(1, rows_s, B_COLS), ada_s, wts, rows_s)

    return (y_p, y_s.reshape(nbs, sd, D_MODEL),
            lat.reshape(1, nb, seq, KV_LORA), kr.reshape(1, nb, seq, ROPE_DIM),
            kb_tail.reshape(1, nb, keep, N_HEADS_B, HEAD_DIM_B), vb_tail.reshape(1, nb, keep, N_HEADS_B, HEAD_DIM_B),
            lat_s.reshape(1, nbs, sd, KV_LORA), kr_s.reshape(1, nbs, sd, ROPE_DIM),
            kb_s32.reshape(1, nbs, sd, N_HEADS_B, HEAD_DIM_B), vb_s32.reshape(1, nbs, sd, N_HEADS_B, HEAD_DIM_B))
```

```python
import functools

import jax
import jax.numpy as jnp
import numpy as np
from jax import lax
from jax.experimental import pallas as pl
from jax.experimental.pallas import tpu as pltpu

D_MODEL = 1024
CHUNK = 64
EPS = 1e-6
NEG_INF = -1e30
N_HEADS_A = 8
NOPE_DIM = 64
ROPE_DIM = 32
HALF_ROPE = ROPE_DIM // 2
V_DIM_A = 64
QK_DIM_A = NOPE_DIM + ROPE_DIM
Q_LORA = 384
KV_LORA = 256
ROPE_BASE = 10000.0
N_HEADS_B = 8
HEAD_DIM_B = 64
LEFT_CHUNKS = 8
BAND_WINDOW = LEFT_CHUNKS * CHUNK
REL_CLIP = 128
D_FF = -(-(8 * D_MODEL) // (3 * 256)) * 256
COL_QKV_B = 3 * N_HEADS_B * HEAD_DIM_B
N_GATE_COLS = 2 * D_MODEL

LANES = 128
HEAD_BLOCK = LANES
ROPE_LANE0 = NOPE_DIM
QA_COLS = N_HEADS_A * HEAD_BLOCK
VA_COLS = N_HEADS_A * V_DIM_A
B_COLS = N_HEADS_B * HEAD_DIM_B
MIX_COLS = Q_LORA + KV_LORA + LANES + COL_QKV_B
OFF_CKV = Q_LORA
OFF_KPE = Q_LORA + KV_LORA
OFF_QB = OFF_KPE + LANES
OFF_KB = OFF_QB + B_COLS
OFF_VB = OFF_KB + B_COLS
FF_CHUNK = 256
LOG2E = 1.4426950408889634
BAND_TQ = 256
BAND_WIN = BAND_WINDOW + BAND_TQ
VMEM_LIMIT = 56 * 1024 * 1024

BF16 = jnp.bfloat16
F32 = jnp.float32


def _cparams(n_axes):
    return pltpu.CompilerParams(dimension_semantics=("arbitrary",) * n_axes, vmem_limit_bytes=VMEM_LIMIT)


def _const_spec(shape):
    nd = len(shape)
    return pl.BlockSpec(shape, lambda *_: (0,) * nd, pipeline_mode=pl.Buffered(1))


def _dot(a, b):
    return jnp.dot(a, b, preferred_element_type=F32)


def _dot_nt(a, b):
    return lax.dot_general(a, b, (((1,), (1,)), ((), ())), preferred_element_type=F32)


def _pad_rows(x, rows):
    return jnp.concatenate([x, jnp.zeros((rows - x.shape[0], x.shape[1]), x.dtype)], axis=0)


def _row_rms(x, inv_n):
    return lax.rsqrt(jnp.sum(x * x, axis=-1, keepdims=True) * inv_n + EPS)


def _group_rms(x, g_ref, e2_ref, invcnt_ref):
    s = _dot((x * x).astype(BF16), g_ref[...])
    r = lax.rsqrt(s * invcnt_ref[...] + EPS)
    r_hi = r.astype(BF16)
    r_lo = (r - r_hi.astype(F32)).astype(BF16)
    return _dot(jnp.concatenate([r_hi, r_lo], axis=1), e2_ref[...])


def _rope_block(x, cos_t, sin_a, sin_b):
    return x * cos_t + pltpu.roll(x, LANES - HALF_ROPE, 1) * sin_a + pltpu.roll(x, HALF_ROPE, 1) * sin_b


def _ada_kernel(c_ref, w_ref, b_ref, o_ref):
    c = c_ref[...]
    a = (c * jax.nn.sigmoid(c)).astype(BF16)
    o_ref[...] = _dot(a, w_ref[...].astype(BF16)) + b_ref[...]


def _ada_call(c_all, w_ada, b_ada):
    rows = c_all.shape[0]
    n_out = w_ada.shape[1]
    tn = D_MODEL
    return pl.pallas_call(
        _ada_kernel,
        grid=(n_out // tn,),
        in_specs=[pl.BlockSpec((rows, D_MODEL), lambda n: (0, 0)),
                  pl.BlockSpec((D_MODEL, tn), lambda n: (0, n)),
                  pl.BlockSpec((1, tn), lambda n: (0, n))],
        out_specs=pl.BlockSpec((rows, tn), lambda n: (0, n)),
        out_shape=jax.ShapeDtypeStruct((rows, n_out), F32),
        compiler_params=_cparams(1),
        name="ada",
    )(c_all, w_ada, b_ada.reshape(1, n_out))


def _mixer_in_kernel(first_tail,
                     x_ref, ada_ref, cos_ref, sina_ref, sinb_ref, gmix_ref, w_in_ref, gql_ref, wq_ref, gkv_ref,
                     gq_row_ref, gkr_row_ref, gqb_row_ref, gkb_row_ref,
                     gq_ref, e2q_ref, icq_ref, gb_ref, e2b_ref, icb_ref,
                     qa_ref, lat_ref, kr_ref, qb_ref, kb_ref, vb_ref, kbt_ref, vbt_ref):
    x = x_ref[0]
    shift = ada_ref[0, :, 0:D_MODEL]
    scale = ada_ref[0, :, D_MODEL:2 * D_MODEL]
    h = x * _row_rms(x, 1.0 / D_MODEL) * gmix_ref[...]
    h = h * (1.0 + scale) + shift
    z = _dot(h.astype(BF16), w_in_ref[...])

    cos_t = cos_ref[...]
    sin_a = sina_ref[...]
    sin_b = sinb_ref[...]

    c_q = z[:, 0:Q_LORA]
    cqn = c_q * _row_rms(c_q, 1.0 / Q_LORA) * gql_ref[...]
    q_raw = _dot(cqn.astype(BF16), wq_ref[...])
    qn = q_raw * _group_rms(q_raw, gq_ref, e2q_ref, icq_ref) * (gq_row_ref[...] * (QK_DIM_A ** -0.5 * LOG2E))
    for hd in range(N_HEADS_A):
        lo = hd * HEAD_BLOCK
        qa_ref[0, :, lo:lo + HEAD_BLOCK] = _rope_block(qn[:, lo:lo + HEAD_BLOCK], cos_t, sin_a, sin_b).astype(BF16)

    c_kv = z[:, OFF_CKV:OFF_CKV + KV_LORA]
    lat_ref[0] = c_kv * _row_rms(c_kv, 1.0 / KV_LORA) * gkv_ref[...]

    kp = z[:, OFF_KPE:OFF_KPE + LANES]
    kr = _rope_block(kp * _row_rms(kp, 1.0 / ROPE_DIM) * gkr_row_ref[...], cos_t, sin_a, sin_b)
    kr_ref[0] = pltpu.roll(kr, LANES - ROPE_LANE0, 1)[:, 0:ROPE_DIM]

    zq = z[:, OFF_QB:OFF_QB + B_COLS]
    qb_ref[0] = (zq * _group_rms(zq, gb_ref, e2b_ref, icb_ref)
                 * (gqb_row_ref[...] * (HEAD_DIM_B ** -0.5 * LOG2E))).astype(BF16)
    zk = z[:, OFF_KB:OFF_KB + B_COLS]
    k_b = zk * _group_rms(zk, gb_ref, e2b_ref, icb_ref) * gkb_row_ref[...]
    v_b = z[:, OFF_VB:OFF_VB + B_COLS]
    kb_ref[0] = k_b.astype(BF16)
    vb_ref[0] = v_b.astype(BF16)

    @pl.when(pl.program_id(1) >= first_tail)
    def _():
        kbt_ref[0] = k_b
        vbt_ref[0] = v_b


def _mixer_in_call(x, ada, tables, consts, weights, tm, keep):
    nb, sb, _ = x.shape
    nj = sb // tm
    first_tail = (sb - keep) // tm
    ada_rows = ada.shape[1]
    if ada_rows == 1:
        ada_spec = pl.BlockSpec((1, 1, 6 * D_MODEL), lambda b, j: (b, 0, 0))
    else:
        ada_spec = pl.BlockSpec((1, tm, 6 * D_MODEL), lambda b, j: (b, j, 0))
    tab_spec = pl.BlockSpec((tm, LANES), lambda b, j: (j, 0))
    tok = lambda c: pl.BlockSpec((1, tm, c), lambda b, j: (b, j, 0))
    tail = pl.BlockSpec((1, tm, B_COLS), lambda b, j: (b, jnp.maximum(j - first_tail, 0), 0))
    const_in = [weights["g_mix"], weights["w_in_mix"], weights["g_q_lora"], weights["w_q_up"], weights["g_kv_lora"],
                weights["gq_row"], weights["gkr_row"], weights["gqb_row"], weights["gkb_row"],
                consts["g_q"], consts["e2_q"], consts["ic_q"], consts["g_b"], consts["e2_b"], consts["ic_b"]]
    out_shape = [jax.ShapeDtypeStruct((nb, sb, QA_COLS), BF16),
                 jax.ShapeDtypeStruct((nb, sb, KV_LORA), F32),
                 jax.ShapeDtypeStruct((nb, sb, ROPE_DIM), F32),
                 jax.ShapeDtypeStruct((nb, sb, B_COLS), BF16),
                 jax.ShapeDtypeStruct((nb, sb, B_COLS), BF16),
                 jax.ShapeDtypeStruct((nb, sb, B_COLS), BF16),
                 jax.ShapeDtypeStruct((nb, keep, B_COLS), F32),
                 jax.ShapeDtypeStruct((nb, keep, B_COLS), F32)]
    return pl.pallas_call(
        functools.partial(_mixer_in_kernel, first_tail),
        grid=(nb, nj),
        in_specs=[tok(D_MODEL), ada_spec, tab_spec, tab_spec, tab_spec] + [_const_spec(a.shape) for a in const_in],
        out_specs=[tok(QA_COLS), tok(KV_LORA), tok(ROPE_DIM), tok(B_COLS), tok(B_COLS), tok(B_COLS), tail, tail],
        out_shape=out_shape,
        compiler_params=_cparams(2),
        name="mixer_in",
    )(x, ada, tables[0], tables[1], tables[2], *const_in)


def _kv_expand_kernel(lat_ref, kr_ref, wk_ref, wv_ref, gk_row_ref, gk_ref, e2k_ref, ick_ref, place_ref, ones_ref,
                      k_ref, v_ref):
    lat = lat_ref[...].astype(BF16)
    kn_raw = _dot(lat, wk_ref[...])
    v_ref[...] = (_dot(lat, wv_ref[...]) + ones_ref[...]).astype(BF16)
    kn = kn_raw * _group_rms(kn_raw, gk_ref, e2k_ref, ick_ref) * gk_row_ref[...]
    kr = kr_ref[...]
    kr_hi = kr.astype(BF16)
    kr_lo = (kr - kr_hi.astype(F32)).astype(BF16)
    placed = _dot(jnp.concatenate([kr_hi, kr_lo], axis=1), place_ref[...])
    k_ref[...] = (kn + placed).astype(BF16)


def _kv_expand_call(latent, k_rope, consts, weights, tm):
    n = latent.shape[0]
    const_in = [weights["w_k_up"], weights["w_v_up"], weights["gk_row"],
                consts["g_k"], consts["e2_k"], consts["ic_k"], consts["place2"], consts["v_ones"]]
    return pl.pallas_call(
        _kv_expand_kernel,
        grid=(n // tm,),
        in_specs=[pl.BlockSpec((tm, KV_LORA), lambda i: (i, 0)),
                  pl.BlockSpec((tm, ROPE_DIM), lambda i: (i, 0))] + [_const_spec(a.shape) for a in const_in],
        out_specs=[pl.BlockSpec((tm, QA_COLS), lambda i: (i, 0)), pl.BlockSpec((tm, QA_COLS), lambda i: (i, 0))],
        out_shape=[jax.ShapeDtypeStruct((n, QA_COLS), BF16), jax.ShapeDtypeStruct((n, QA_COLS), BF16)],
        compiler_params=_cparams(1),
        name="kv_expand",
    )(latent, k_rope, *const_in)


def _normalize_heads(acc0, acc1, lane):
    o0 = acc0 / pltpu.roll(acc0, V_DIM_A, 1)
    o1 = acc1 / pltpu.roll(acc1, V_DIM_A, 1)
    return jnp.where(lane < V_DIM_A, o0, pltpu.roll(o1, V_DIM_A, 1))


def _mla_prompt_kernel(tq, q_ref, k_ref, v_ref, o_ref):
    seq = q_ref.shape[1]
    lane = lax.broadcasted_iota(jnp.int32, (tq, LANES), 1)
    row_c = lax.broadcasted_iota(jnp.int32, (tq, tq), 0) // CHUNK
    col_c = lax.broadcasted_iota(jnp.int32, (tq, tq), 1) // CHUNK
    diag_ok = row_c >= col_c
    for qi in range(seq // tq):
        r0 = qi * tq
        accs = []
        for hd in range(2):
            lo = hd * HEAD_BLOCK
            q = q_ref[0, r0:r0 + tq, lo:lo + HEAD_BLOCK]
            s_d = jnp.where(diag_ok, _dot_nt(q, k_ref[0, r0:r0 + tq, lo:lo + HEAD_BLOCK]), NEG_INF)
            m = jnp.max(s_d, axis=-1, keepdims=True)
            if qi > 0:
                s_f = _dot_nt(q, k_ref[0, 0:r0, lo:lo + HEAD_BLOCK])
                m = jnp.maximum(m, jnp.max(s_f, axis=-1, keepdims=True))
            acc = _dot(jnp.exp2(s_d - m).astype(BF16), v_ref[0, r0:r0 + tq, lo:lo + HEAD_BLOCK])
            if qi > 0:
                acc = acc + _dot(jnp.exp2(s_f - m).astype(BF16), v_ref[0, 0:r0, lo:lo + HEAD_BLOCK])
            accs.append(acc)
        o_ref[0, r0:r0 + tq, :] = _normalize_heads(accs[0], accs[1], lane).astype(BF16)


def _mla_prompt_call(q_a, k_a, v_a, tq):
    nb, seq, _ = q_a.shape
    n_pairs = N_HEADS_A // 2
    pair = pl.BlockSpec((1, seq, 2 * HEAD_BLOCK), lambda b, p: (b, 0, p))
    return pl.pallas_call(
        functools.partial(_mla_prompt_kernel, tq),
        grid=(nb, n_pairs),
        in_specs=[pair, pair, pair],
        out_specs=pl.BlockSpec((1, seq, 2 * V_DIM_A), lambda b, p: (b, 0, p)),
        out_shape=jax.ShapeDtypeStruct((nb, seq, VA_COLS), BF16),
        compiler_params=_cparams(2),
        name="mla_prompt",
    )(q_a, k_a, v_a)


def _mla_sample_kernel(q_ref, kc_ref, vc_ref, kn_ref, vn_ref, o_ref):
    rows = q_ref.shape[1]
    lane = lax.broadcasted_iota(jnp.int32, (rows, LANES), 1)
    accs = []
    for hd in range(2):
        lo = hd * HEAD_BLOCK
        q = q_ref[0, :, lo:lo + HEAD_BLOCK]
        s_c = _dot_nt(q, kc_ref[0, :, lo:lo + HEAD_BLOCK])
        s_n = jnp.where(lane < rows, _dot_nt(q, _pad_rows(kn_ref[0, :, lo:lo + HEAD_BLOCK], LANES)), NEG_INF)
        m = jnp.maximum(jnp.max(s_c, axis=-1, keepdims=True), jnp.max(s_n, axis=-1, keepdims=True))
        accs.append(_dot(jnp.exp2(s_c - m).astype(BF16), vc_ref[0, :, lo:lo + HEAD_BLOCK])
                    + _dot(jnp.exp2(s_n - m).astype(BF16), _pad_rows(vn_ref[0, :, lo:lo + HEAD_BLOCK], LANES)))
    o_ref[0] = _normalize_heads(accs[0], accs[1], lane).astype(BF16)


def _mla_sample_call(q_a, k_cache, v_cache, k_new, v_new):
    nb, rows, _ = q_a.shape
    past = k_cache.shape[1]
    n_pairs = N_HEADS_A // 2
    kspec = lambda s: pl.BlockSpec((1, s, 2 * HEAD_BLOCK), lambda b, p: (b, 0, p))
    vspec = lambda s: pl.BlockSpec((1, s, 2 * V_DIM_A), lambda b, p: (b, 0, p))
    return pl.pallas_call(
        _mla_sample_kernel,
        grid=(nb, n_pairs),
        in_specs=[kspec(rows), kspec(past), kspec(past), kspec(rows), kspec(rows)],
        out_specs=vspec(rows),
        out_shape=jax.ShapeDtypeStruct((nb, rows, VA_COLS), BF16),
        compiler_params=_cparams(2),
        name="mla_sample",
    )(q_a, k_cache, v_cache, k_new, v_new)


def _toeplitz_bias(g0, rows):
    far = g0[:, 0:1]
    x0 = jnp.broadcast_to(g0[:, 0:LANES], (rows, LANES))
    x1 = jnp.broadcast_to(g0[:, LANES:2 * LANES], (rows, LANES))
    row = lax.broadcasted_iota(jnp.int32, (rows, LANES), 0)
    lane = lax.broadcasted_iota(jnp.int32, (rows, LANES), 1)
    step = 1
    while step < rows:
        r0 = pltpu.roll(x0, step, 1)
        r1 = pltpu.roll(x1, step, 1)
        keep = lane >= step
        take = (row & step) != 0
        x0, x1 = jnp.where(take, jnp.where(keep, r0, r1), x0), jnp.where(take, jnp.where(keep, r1, r0), x1)
        step *= 2
    return jnp.where(lane < row, far, x0), x1, far


def _band_bias_kernel(rb_ref, bias_ref):
    hd = pl.program_id(0)
    tw0, tw1, far = _toeplitz_bias(rb_ref[pl.ds(hd, 1), :], LANES)
    far_blk = jnp.broadcast_to(far, (LANES, LANES))
    n_blk = BAND_WIN // LANES
    row_c = lax.broadcasted_iota(jnp.int32, (LANES, LANES), 0) // CHUNK
    lane = lax.broadcasted_iota(jnp.int32, (LANES, LANES), 1)
    for half in range(BAND_TQ // LANES):
        first_tw = BAND_WINDOW // LANES - 1 + half
        for cb in range(n_blk):
            blk = tw0 if cb == first_tw else (tw1 if cb == first_tw + 1 else far_blk)
            q_c = row_c + half * (LANES // CHUNK)
            col_c = (cb * LANES + lane) // CHUNK
            ok = (col_c >= q_c) & (col_c <= q_c + LEFT_CHUNKS)
            bias_ref[0, half * LANES:(half + 1) * LANES, cb * LANES:(cb + 1) * LANES] = jnp.where(ok, blk * LOG2E, NEG_INF)


def _band_bias_call(rb_rev):
    return pl.pallas_call(
        _band_bias_kernel,
        grid=(N_HEADS_B,),
        in_specs=[_const_spec(rb_rev.shape)],
        out_specs=pl.BlockSpec((1, BAND_TQ, BAND_WIN), lambda h: (h, 0, 0)),
        out_shape=jax.ShapeDtypeStruct((N_HEADS_B, BAND_TQ, BAND_WIN), F32),
        compiler_params=_cparams(1),
        name="band_bias",
    )(rb_rev)


def _band_prompt_kernel(q_ref, k_ref, v_ref, bias_ref, o_ref):
    seq = q_ref.shape[1]
    lane_q = lax.broadcasted_iota(jnp.int32, (BAND_TQ, LANES), 1)
    for t in range(seq // BAND_TQ):
        t0 = t * BAND_TQ
        k_lo = max(t0 - BAND_WINDOW, 0)
        w = t0 + BAND_TQ - k_lo
        q = q_ref[0, t0:t0 + BAND_TQ, :]
        kw = k_ref[0, k_lo:t0 + BAND_TQ, :]
        vw = v_ref[0, k_lo:t0 + BAND_TQ, :]
        outs = []
        for hd in range(2):
            head_lanes = (lane_q < HEAD_DIM_B) if hd == 0 else (lane_q >= HEAD_DIM_B)
            qh = jnp.where(head_lanes, q, jnp.zeros_like(q))
            s = _dot_nt(qh, kw) + bias_ref[hd, :, BAND_WIN - w:BAND_WIN]
            p = jnp.exp2(s - jnp.max(s, axis=-1, keepdims=True))
            l = jnp.sum(p, axis=-1, keepdims=True)
            outs.append(_dot(p.astype(BF16), vw) / l)
        o_ref[0, t0:t0 + BAND_TQ, :] = jnp.where(lane_q < HEAD_DIM_B, outs[0], outs[1]).astype(BF16)


def _band_prompt_call(q_b, k_b, v_b, bias):
    nb, seq, _ = q_b.shape
    n_pairs = N_HEADS_B // 2
    spec = pl.BlockSpec((1, seq, LANES), lambda b, p: (b, 0, p))
    return pl.pallas_call(
        _band_prompt_kernel,
        grid=(nb, n_pairs),
        in_specs=[spec, spec, spec, pl.BlockSpec((2, BAND_TQ, BAND_WIN), lambda b, p: (p, 0, 0))],
        out_specs=spec,
        out_shape=jax.ShapeDtypeStruct((nb, seq, B_COLS), BF16),
        compiler_params=_cparams(2),
        name="band_prompt",
    )(q_b, k_b, v_b, bias)


def _band_sample_kernel(q_ref, kc_ref, vc_ref, kn_ref, vn_ref, bias_ref, o_ref):
    rows = q_ref.shape[1]
    n_cache = kc_ref.shape[1]
    lane_q = lax.broadcasted_iota(jnp.int32, (rows, LANES), 1)
    q = q_ref[0]
    kc = kc_ref[0].astype(BF16)
    vc = vc_ref[0].astype(BF16)
    kn = _pad_rows(kn_ref[0], LANES)
    vn = _pad_rows(vn_ref[0], LANES)
    outs = []
    for hd in range(2):
        head_lanes = (lane_q < HEAD_DIM_B) if hd == 0 else (lane_q >= HEAD_DIM_B)
        qh = jnp.where(head_lanes, q, jnp.zeros_like(q))
        s_c = _dot_nt(qh, kc) + bias_ref[hd, 0:rows, 0:n_cache]
        s_n = jnp.where(lane_q < rows, _dot_nt(qh, kn) + bias_ref[hd, 0:rows, n_cache:n_cache + LANES], NEG_INF)
        m = jnp.maximum(jnp.max(s_c, axis=-1, keepdims=True), jnp.max(s_n, axis=-1, keepdims=True))
        p_c = jnp.exp2(s_c - m)
        p_n = jnp.exp2(s_n - m)
        l = jnp.sum(p_c, axis=-1, keepdims=True) + jnp.sum(p_n, axis=-1, keepdims=True)
        outs.append((_dot(p_c.astype(BF16), vc) + _dot(p_n.astype(BF16), vn)) / l)
    o_ref[0] = jnp.where(lane_q < HEAD_DIM_B, outs[0], outs[1]).astype(BF16)


def _band_sample_call(q_b, k_cache, v_cache, k_new, v_new, bias):
    nb, rows, _ = q_b.shape
    n_cache = k_cache.shape[1]
    n_pairs = N_HEADS_B // 2
    spec = lambda s: pl.BlockSpec((1, s, LANES), lambda b, p: (b, 0, p))
    return pl.pallas_call(
        _band_sample_kernel,
        grid=(nb, n_pairs),
        in_specs=[spec(rows), spec(n_cache), spec(n_cache), spec(rows), spec(rows),
                  pl.BlockSpec((2, BAND_TQ, BAND_WIN), lambda b, p: (p, 0, 0))],
        out_specs=spec(rows),
        out_shape=jax.ShapeDtypeStruct((nb, rows, B_COLS), BF16),
        compiler_params=_cparams(2),
        name="band_sample",
    )(q_b, k_cache, v_cache, k_new, v_new, bias)


def _tail_kernel(x_ref, oa_ref, ob_ref, ada_ref, gmix_ref, wg_ref, woa_ref, wob_ref, wout_ref, gffn_ref,
                 wgate_ref, wup_ref, wdown_ref, y_ref):
    x = x_ref[0]
    ada = lambda k: ada_ref[0, :, k * D_MODEL:(k + 1) * D_MODEL]
    h = x * _row_rms(x, 1.0 / D_MODEL) * gmix_ref[...]
    h = (h * (1.0 + ada(1)) + ada(0)).astype(BF16)
    gates = jax.nn.sigmoid(_dot(h, wg_ref[...]))
    y_a = _dot(oa_ref[0], woa_ref[...])
    y_b = _dot(ob_ref[0], wob_ref[...])
    mixed = gates[:, 0:D_MODEL] * y_a + gates[:, D_MODEL:2 * D_MODEL] * y_b
    x1 = x + ada(2) * _dot(mixed.astype(BF16), wout_ref[...])
    h2 = x1 * _row_rms(x1, 1.0 / D_MODEL) * gffn_ref[...]
    h2 = (h2 * (1.0 + ada(4)) + ada(3)).astype(BF16)
    acc = jnp.zeros_like(x1)
    for c in range(D_FF // FF_CHUNK):
        cols = slice(c * FF_CHUNK, (c + 1) * FF_CHUNK)
        g = _dot(h2, wgate_ref[:, cols])
        u = _dot(h2, wup_ref[:, cols])
        act = (g * jax.nn.sigmoid(g) * u).astype(BF16)
        acc = acc + _dot(act, wdown_ref[cols, :])
    y_ref[0] = x1 + ada(5) * acc


def _tail_call(x, o_a, o_b, ada, weights, tm):
    nb, sb, _ = x.shape
    if ada.shape[1] == 1:
        ada_spec = pl.BlockSpec((1, 1, 6 * D_MODEL), lambda b, j: (b, 0, 0))
    else:
        ada_spec = pl.BlockSpec((1, tm, 6 * D_MODEL), lambda b, j: (b, j, 0))
    tok = lambda c: pl.BlockSpec((1, tm, c), lambda b, j: (b, j, 0))
    const_in = [weights["g_mix"], weights["w_in_gate"], weights["w_o_a"], weights["w_o_b"], weights["w_out"],
                weights["g_ffn"], weights["w_gate"], weights["w_up"], weights["w_down"]]
    return pl.pallas_call(
        _tail_kernel,
        grid=(nb, sb // tm),
        in_specs=[tok(D_MODEL), tok(VA_COLS), tok(B_COLS), ada_spec] + [_const_spec(a.shape) for a in const_in],
        out_specs=tok(D_MODEL),
        out_shape=jax.ShapeDtypeStruct((nb, sb, D_MODEL), F32),
        compiler_params=_cparams(2),
        name="tail",
    )(x, o_a, o_b, ada, *const_in)


def _group_constants():
    def pack(g, inv_cnt):
        e = g.T
        ic = np.ones((1, LANES), np.float32)
        ic[0, :len(inv_cnt)] = inv_cnt
        return jnp.asarray(g, BF16), jnp.asarray(np.concatenate([e, e], axis=0), BF16), jnp.asarray(ic)

    g_q = np.zeros((QA_COLS, LANES), np.float32)
    g_k = np.zeros((QA_COLS, LANES), np.float32)
    for hd in range(N_HEADS_A):
        g_q[hd * HEAD_BLOCK:hd * HEAD_BLOCK + NOPE_DIM, hd] = 1.0
        g_q[hd * HEAD_BLOCK + ROPE_LANE0:hd * HEAD_BLOCK + ROPE_LANE0 + ROPE_DIM, N_HEADS_A + hd] = 1.0
        g_k[hd * HEAD_BLOCK:hd * HEAD_BLOCK + NOPE_DIM, hd] = 1.0
    g_b = np.zeros((B_COLS, LANES), np.float32)
    for hd in range(N_HEADS_B):
        g_b[hd * HEAD_DIM_B:(hd + 1) * HEAD_DIM_B, hd] = 1.0
    place = np.zeros((ROPE_DIM, QA_COLS), np.float32)
    for hd in range(N_HEADS_A):
        for i in range(ROPE_DIM):
            place[i, hd * HEAD_BLOCK + ROPE_LANE0 + i] = 1.0
    c = {}
    c["g_q"], c["e2_q"], c["ic_q"] = pack(g_q, [1.0 / NOPE_DIM] * N_HEADS_A + [1.0 / ROPE_DIM] * N_HEADS_A)
    c["g_k"], c["e2_k"], c["ic_k"] = pack(g_k, [1.0 / NOPE_DIM] * N_HEADS_A)
    c["g_b"], c["e2_b"], c["ic_b"] = pack(g_b, [1.0 / HEAD_DIM_B] * N_HEADS_B)
    c["place2"] = jnp.asarray(np.concatenate([place, place], axis=0), BF16)
    v_ones = np.zeros((N_HEADS_A, HEAD_BLOCK), np.float32)
    v_ones[:, V_DIM_A:] = 1.0
    c["v_ones"] = jnp.asarray(v_ones.reshape(1, QA_COLS))
    return c


def _rope_tables(pos):
    inv_freq = ROPE_BASE ** (-jnp.arange(HALF_ROPE, dtype=F32) / HALF_ROPE)
    ang = pos.astype(F32)[:, None] * inv_freq[None, :]
    cos, sin = jnp.cos(ang), jnp.sin(ang)
    n = pos.shape[0]
    ones = jnp.ones((n, ROPE_LANE0), F32)
    zeros = jnp.zeros((n, ROPE_LANE0), F32)
    pad1 = jnp.ones((n, LANES - ROPE_LANE0 - ROPE_DIM), F32)
    pad0 = jnp.zeros((n, LANES - ROPE_LANE0 - ROPE_DIM), F32)
    z16 = jnp.zeros((n, HALF_ROPE), F32)
    cos_t = jnp.concatenate([ones, cos, cos, pad1], axis=1)
    sin_a = jnp.concatenate([zeros, -sin, z16, pad0], axis=1)
    sin_b = jnp.concatenate([zeros, z16, sin, pad0], axis=1)
    return cos_t, sin_a, sin_b


def _layer_weights(l, w_in, g_norm_mix, g_q_lora, w_q_up, g_kv_lora, w_kv_up, g_qn_a, g_kn_a, g_qr_a, g_kr_a,
                   g_q_b, g_k_b, w_o_a, w_o_b, w_out, g_norm_ffn, w_gate, w_up, w_down):
    wi = w_in[l]
    kpe_block = jnp.pad(wi[:, OFF_KPE:OFF_KPE + ROPE_DIM], ((0, 0), (ROPE_LANE0, LANES - ROPE_LANE0 - ROPE_DIM)))
    n_mix = OFF_KPE + ROPE_DIM + COL_QKV_B
    w = {}
    w["w_in_mix"] = jnp.concatenate([wi[:, 0:OFF_KPE], kpe_block, wi[:, OFF_KPE + ROPE_DIM:n_mix]], axis=1).astype(BF16)
    w["w_in_gate"] = wi[:, n_mix:n_mix + N_GATE_COLS].astype(BF16)
    wq3 = w_q_up[l].reshape(Q_LORA, N_HEADS_A, QK_DIM_A)
    w["w_q_up"] = jnp.pad(wq3, ((0, 0), (0, 0), (0, HEAD_BLOCK - QK_DIM_A))).reshape(Q_LORA, QA_COLS).astype(BF16)
    wkv3 = w_kv_up[l].reshape(KV_LORA, N_HEADS_A, NOPE_DIM + V_DIM_A)
    w["w_k_up"] = jnp.pad(wkv3[..., :NOPE_DIM], ((0, 0), (0, 0), (0, HEAD_BLOCK - NOPE_DIM))).reshape(KV_LORA, QA_COLS).astype(BF16)
    w["w_v_up"] = jnp.pad(wkv3[..., NOPE_DIM:], ((0, 0), (0, 0), (0, HEAD_BLOCK - V_DIM_A))).reshape(KV_LORA, QA_COLS).astype(BF16)
    zpad = jnp.zeros((HEAD_BLOCK - QK_DIM_A,), F32)
    w["gq_row"] = jnp.tile(jnp.concatenate([g_qn_a[l], g_qr_a[l], zpad]), N_HEADS_A).reshape(1, QA_COLS)
    w["gk_row"] = jnp.tile(jnp.concatenate([g_kn_a[l], jnp.zeros((HEAD_BLOCK - NOPE_DIM,), F32)]), N_HEADS_A).reshape(1, QA_COLS)
    w["gkr_row"] = jnp.concatenate([jnp.zeros((ROPE_LANE0,), F32), g_kr_a[l], zpad]).reshape(1, LANES)
    w["gqb_row"] = jnp.tile(g_q_b[l], N_HEADS_B).reshape(1, B_COLS)
    w["gkb_row"] = jnp.tile(g_k_b[l], N_HEADS_B).reshape(1, B_COLS)
    w["g_mix"] = g_norm_mix[l].reshape(1, D_MODEL)
    w["g_q_lora"] = g_q_lora[l].reshape(1, Q_LORA)
    w["g_kv_lora"] = g_kv_lora[l].reshape(1, KV_LORA)
    w["g_ffn"] = g_norm_ffn[l].reshape(1, D_MODEL)
    w["w_o_a"] = w_o_a[l].astype(BF16)
    w["w_o_b"] = w_o_b[l].astype(BF16)
    w["w_out"] = w_out[l].astype(BF16)
    w["w_gate"] = w_gate[l].astype(BF16)
    w["w_up"] = w_up[l].astype(BF16)
    w["w_down"] = w_down[l].astype(BF16)
    return w


def kernel(x_prompt, x_sample, c_prompt, c_sample, cache_kv_latent, cache_k_rope, cache_band_k, cache_band_v, w_ada, b_ada, g_norm_mix, w_in, g_q_lora, w_q_up, g_kv_lora, w_kv_up, g_qn_a, g_kn_a, g_qr_a, g_kr_a, g_q_b, g_k_b, rel_bias, w_o_a, w_o_b, w_out, g_norm_ffn, w_gate, w_up, w_down):
    depth = w_in.shape[0]
    nb, seq, _ = x_prompt.shape
    nbs, sd, _ = x_sample.shape
    past = cache_kv_latent.shape[2]
    n_buf = cache_band_k.shape[2]
    keep = min(BAND_WINDOW, seq)
    assert depth == 1 and nbs * sd == LANES and n_buf == BAND_WINDOW and seq % 512 == 0 and past % 512 == 0
    tm = 512
    rows_s = nbs * sd

    consts = _group_constants()
    tab_p = _rope_tables(jnp.arange(seq))
    tab_s = _rope_tables(past + (jnp.arange(rows_s) % sd))
    xs = x_sample.reshape(1, rows_s, D_MODEL)

    l = 0
    wts = _layer_weights(l, w_in, g_norm_mix, g_q_lora, w_q_up, g_kv_lora, w_kv_up, g_qn_a, g_kn_a, g_qr_a, g_kr_a,
                         g_q_b, g_k_b, w_o_a, w_o_b, w_out, g_norm_ffn, w_gate, w_up, w_down)
    band_bias = _band_bias_call(rel_bias[l][:, 2 * REL_CLIP:0:-1])

    ada = _ada_call(jnp.concatenate([c_prompt, c_sample], axis=0), w_ada[l], b_ada[l])
    ada_p = ada[:nb].reshape(nb, 1, 6 * D_MODEL)
    ada_s = jnp.repeat(ada[nb:], sd, axis=0).reshape(1, rows_s, 6 * D_MODEL)

    qa, lat, kr, qb, kb, vb, kb_tail, vb_tail = _mixer_in_call(x_prompt, ada_p, tab_p, consts, wts, tm, keep)
    k_a, v_a = _kv_expand_call(lat.reshape(nb * seq, KV_LORA), kr.reshape(nb * seq, ROPE_DIM), consts, wts, tm)
    o_a = _mla_prompt_call(qa, k_a.reshape(nb, seq, QA_COLS), v_a.reshape(nb, seq, QA_COLS), 512)
    o_b = _band_prompt_call(qb, kb, vb, band_bias)
    y_p = _tail_call(x_prompt, o_a, o_b, ada_p, wts, tm)

    qa_s, lat_s, kr_s, qb_s, kb_s, vb_s, kb_s32, vb_s32 = _mixer_in_call(xs, ada_s, tab_s, consts, wts, rows_s, rows_s)
    kc, vc = _kv_expand_call(cache_kv_latent[l].reshape(nbs * past, KV_LORA),
                             cache_k_rope[l].reshape(nbs * past, ROPE_DIM), consts, wts, tm)
    kn, vn = _kv_expand_call(lat_s.reshape(rows_s, KV_LORA), kr_s.reshape(rows_s, ROPE_DIM), consts, wts, rows_s)
    o_a_s = _mla_sample_call(qa_s.reshape(nbs, sd, QA_COLS), kc.reshape(nbs, past, QA_COLS),
                             vc.reshape(nbs, past, QA_COLS), kn.reshape(nbs, sd, QA_COLS), vn.reshape(nbs, sd, QA_COLS))
    o_b_s = _band_sample_call(qb_s.reshape(nbs, sd, B_COLS), cache_band_k[l].reshape(nbs, n_buf, B_COLS),
                              cache_band_v[l].reshape(nbs, n_buf, B_COLS), kb_s.reshape(nbs, sd, B_COLS),
                              vb_s.reshape(nbs, sd, B_COLS), band_bias)
    y_s = _tail_call(xs, o_a_s.reshape(1, rows_s, VA_COLS), o_b_s.reshape(1, rows_s, B_COLS), ada_s, wts, rows_s)

    return (y_p, y_s.reshape(nbs, sd, D_MODEL),
            lat.reshape(1, nb, seq, KV_LORA), kr.reshape(1, nb, seq, ROPE_DIM),
            kb_tail.reshape(1, nb, keep, N_HEADS_B, HEAD_DIM_B), vb_tail.reshape(1, nb, keep, N_HEADS_B, HEAD_DIM_B),
            lat_s.reshape(1, nbs, sd, KV_LORA), kr_s.reshape(1, nbs, sd, ROPE_DIM),
            kb_s32.reshape(1, nbs, sd, N_HEADS_B, HEAD_DIM_B), vb_s32.reshape(1, nbs, sd, N_HEADS_B, HEAD_DIM_B))
```

```python
import functools

import jax
import jax.numpy as jnp
import numpy as np
from jax import lax
from jax.experimental import pallas as pl
from jax.experimental.pallas import tpu as pltpu

D_MODEL = 1024
CHUNK = 64
EPS = 1e-6
NEG_INF = -1e30
N_HEADS_A = 8
NOPE_DIM = 64
ROPE_DIM = 32
HALF_ROPE = ROPE_DIM // 2
V_DIM_A = 64
QK_DIM_A = NOPE_DIM + ROPE_DIM
Q_LORA = 384
KV_LORA = 256
ROPE_BASE = 10000.0
N_HEADS_B = 8
HEAD_DIM_B = 64
LEFT_CHUNKS = 8
BAND_WINDOW = LEFT_CHUNKS * CHUNK
REL_CLIP = 128
D_FF = -(-(8 * D_MODEL) // (3 * 256)) * 256
COL_QKV_B = 3 * N_HEADS_B * HEAD_DIM_B
N_GATE_COLS = 2 * D_MODEL

LANES = 128
HEAD_BLOCK = LANES
ROPE_LANE0 = NOPE_DIM
QA_COLS = N_HEADS_A * HEAD_BLOCK
VA_COLS = N_HEADS_A * V_DIM_A
B_COLS = N_HEADS_B * HEAD_DIM_B
MIX_COLS = Q_LORA + KV_LORA + LANES + COL_QKV_B
OFF_CKV = Q_LORA
OFF_KPE = Q_LORA + KV_LORA
OFF_QB = OFF_KPE + LANES
OFF_KB = OFF_QB + B_COLS
OFF_VB = OFF_KB + B_COLS
FF_CHUNK = 256
LOG2E = 1.4426950408889634
BAND_TQ = 256
BAND_WIN = BAND_WINDOW + BAND_TQ
VMEM_LIMIT = 56 * 1024 * 1024

BF16 = jnp.bfloat16
F32 = jnp.float32


def _cparams(n_axes):
    return pltpu.CompilerParams(dimension_semantics=("arbitrary",) * n_axes, vmem_limit_bytes=VMEM_LIMIT)


def _const_spec(shape):
    nd = len(shape)
    return pl.BlockSpec(shape, lambda *_: (0,) * nd, pipeline_mode=pl.Buffered(1))


def _dot(a, b):
    return jnp.dot(a, b, preferred_element_type=F32)


def _dot_nt(a, b):
    return lax.dot_general(a, b, (((1,), (1,)), ((), ())), preferred_element_type=F32)


def _pad_rows(x, rows):
    return jnp.concatenate([x, jnp.zeros((rows - x.shape[0], x.shape[1]), x.dtype)], axis=0)


def _row_rms(x, inv_n):
    return lax.rsqrt(jnp.sum(x * x, axis=-1, keepdims=True) * inv_n + EPS)


def _group_rms(x, g_ref, e2_ref, invcnt_ref):
    s = _dot((x * x).astype(BF16), g_ref[...])
    r = lax.rsqrt(s * invcnt_ref[...] + EPS)
    r_hi = r.astype(BF16)
    r_lo = (r - r_hi.astype(F32)).astype(BF16)
    return _dot(jnp.concatenate([r_hi, r_lo], axis=1), e2_ref[...])


def _rope_block(x, cos_t, sin_a, sin_b):
    return x * cos_t + pltpu.roll(x, LANES - HALF_ROPE, 1) * sin_a + pltpu.roll(x, HALF_ROPE, 1) * sin_b


def _ada_kernel(c_ref, w_ref, b_ref, o_ref):
    c = c_ref[...]
    a = (c * jax.nn.sigmoid(c)).astype(BF16)
    o_ref[...] = _dot(a, w_ref[...].astype(BF16)) + b_ref[...]


def _ada_call(c_all, w_ada, b_ada):
    rows = c_all.shape[0]
    n_out = w_ada.shape[1]
    tn = D_MODEL
    return pl.pallas_call(
        _ada_kernel,
        grid=(n_out // tn,),
        in_specs=[pl.BlockSpec((rows, D_MODEL), lambda n: (0, 0)),
                  pl.BlockSpec((D_MODEL, tn), lambda n: (0, n)),
                  pl.BlockSpec((1, tn), lambda n: (0, n))],
        out_specs=pl.BlockSpec((rows, tn), lambda n: (0, n)),
        out_shape=jax.ShapeDtypeStruct((rows, n_out), F32),
        compiler_params=_cparams(1),
        name="ada",
    )(c_all, w_ada, b_ada.reshape(1, n_out))


def _mixer_in_kernel(first_tail, tail_feat_major,
                     x_ref, ada_ref, cos_ref, sina_ref, sinb_ref, gmix_ref, w_in_ref, gql_ref, wq_ref, gkv_ref,
                     gq_row_ref, gkr_row_ref, gqb_row_ref, gkb_row_ref,
                     gq_ref, e2q_ref, icq_ref, gb_ref, e2b_ref, icb_ref,
                     qa_ref, lat_ref, kr_ref, krt_ref, qb_ref, kb_ref, vb_ref, kbt_ref, vbt_ref):
    x = x_ref[0]
    shift = ada_ref[0, :, 0:D_MODEL]
    scale = ada_ref[0, :, D_MODEL:2 * D_MODEL]
    h = x * _row_rms(x, 1.0 / D_MODEL) * gmix_ref[...]
    h = h * (1.0 + scale) + shift
    z = _dot(h.astype(BF16), w_in_ref[...])

    cos_t = cos_ref[...]
    sin_a = sina_ref[...]
    sin_b = sinb_ref[...]

    c_q = z[:, 0:Q_LORA]
    cqn = c_q * _row_rms(c_q, 1.0 / Q_LORA) * gql_ref[...]
    q_raw = _dot(cqn.astype(BF16), wq_ref[...])
    qn = q_raw * _group_rms(q_raw, gq_ref, e2q_ref, icq_ref) * (gq_row_ref[...] * (QK_DIM_A ** -0.5 * LOG2E))
    for hd in range(N_HEADS_A):
        lo = hd * HEAD_BLOCK
        qa_ref[0, :, lo:lo + HEAD_BLOCK] = _rope_block(qn[:, lo:lo + HEAD_BLOCK], cos_t, sin_a, sin_b).astype(BF16)

    c_kv = z[:, OFF_CKV:OFF_CKV + KV_LORA]
    lat_ref[0] = c_kv * _row_rms(c_kv, 1.0 / KV_LORA) * gkv_ref[...]

    kp = z[:, OFF_KPE:OFF_KPE + LANES]
    kr = _rope_block(kp * _row_rms(kp, 1.0 / ROPE_DIM) * gkr_row_ref[...], cos_t, sin_a, sin_b)
    kr_ref[0] = pltpu.roll(kr, LANES - ROPE_LANE0, 1)[:, 0:ROPE_DIM]
    krt_ref[0] = kr.T[ROPE_LANE0:ROPE_LANE0 + ROPE_DIM, :]

    zq = z[:, OFF_QB:OFF_QB + B_COLS]
    qb_ref[0] = (zq * _group_rms(zq, gb_ref, e2b_ref, icb_ref)
                 * (gqb_row_ref[...] * (HEAD_DIM_B ** -0.5 * LOG2E))).astype(BF16)
    zk = z[:, OFF_KB:OFF_KB + B_COLS]
    k_b = zk * _group_rms(zk, gb_ref, e2b_ref, icb_ref) * gkb_row_ref[...]
    v_b = z[:, OFF_VB:OFF_VB + B_COLS]
    kb_ref[0] = k_b.astype(BF16)
    vb_ref[0] = v_b.astype(BF16)

    @pl.when(pl.program_id(1) >= first_tail)
    def _():
        kbt_ref[0] = k_b.T if tail_feat_major else k_b
        vbt_ref[0] = v_b.T if tail_feat_major else v_b


def _mixer_in_call(x, ada, tables, consts, weights, tm, keep, tail_feat_major):
    nb, sb, _ = x.shape
    nj = sb // tm
    first_tail = (sb - keep) // tm
    ada_rows = ada.shape[1]
    if ada_rows == 1:
        ada_spec = pl.BlockSpec((1, 1, 6 * D_MODEL), lambda b, j: (b, 0, 0))
    else:
        ada_spec = pl.BlockSpec((1, tm, 6 * D_MODEL), lambda b, j: (b, j, 0))
    tab_spec = pl.BlockSpec((tm, LANES), lambda b, j: (j, 0))
    tok = lambda c: pl.BlockSpec((1, tm, c), lambda b, j: (b, j, 0))
    tail_idx = lambda j: jnp.maximum(j - first_tail, 0)
    if tail_feat_major:
        tail = pl.BlockSpec((1, B_COLS, tm), lambda b, j: (b, 0, tail_idx(j)))
        tail_shape = jax.ShapeDtypeStruct((nb, B_COLS, keep), F32)
    else:
        tail = pl.BlockSpec((1, tm, B_COLS), lambda b, j: (b, tail_idx(j), 0))
        tail_shape = jax.ShapeDtypeStruct((nb, keep, B_COLS), F32)
    const_in = [weights["g_mix"], weights["w_in_mix"], weights["g_q_lora"], weights["w_q_up"], weights["g_kv_lora"],
                weights["gq_row"], weights["gkr_row"], weights["gqb_row"], weights["gkb_row"],
                consts["g_q"], consts["e2_q"], consts["ic_q"], consts["g_b"], consts["e2_b"], consts["ic_b"]]
    out_shape = [jax.ShapeDtypeStruct((nb, sb, QA_COLS), BF16),
                 jax.ShapeDtypeStruct((nb, sb, KV_LORA), F32),
                 jax.ShapeDtypeStruct((nb, sb, ROPE_DIM), F32),
                 jax.ShapeDtypeStruct((nb, ROPE_DIM, sb), F32),
                 jax.ShapeDtypeStruct((nb, sb, B_COLS), BF16),
                 jax.ShapeDtypeStruct((nb, sb, B_COLS), BF16),
                 jax.ShapeDtypeStruct((nb, sb, B_COLS), BF16),
                 tail_shape, tail_shape]
    rope_t = pl.BlockSpec((1, ROPE_DIM, tm), lambda b, j: (b, 0, j))
    return pl.pallas_call(
        functools.partial(_mixer_in_kernel, first_tail, tail_feat_major),
        grid=(nb, nj),
        in_specs=[tok(D_MODEL), ada_spec, tab_spec, tab_spec, tab_spec] + [_const_spec(a.shape) for a in const_in],
        out_specs=[tok(QA_COLS), tok(KV_LORA), tok(ROPE_DIM), rope_t, tok(B_COLS), tok(B_COLS), tok(B_COLS), tail, tail],
        out_shape=out_shape,
        compiler_params=_cparams(2),
        name="mixer_in",
    )(x, ada, tables[0], tables[1], tables[2], *const_in)


def _kv_expand_kernel(lat_ref, krt_ref, wk_ref, wv_ref, gk_row_ref, gk_ref, e2k_ref, ick_ref, ones_ref, k_ref, v_ref):
    lat = lat_ref[0].astype(BF16)
    tm = lat.shape[0]
    kn_raw = _dot(lat, wk_ref[...])
    v_ref[0] = (_dot(lat, wv_ref[...]) + ones_ref[...]).astype(BF16)
    kn = kn_raw * _group_rms(kn_raw, gk_ref, e2k_ref, ick_ref) * gk_row_ref[...]
    kr_tok = jnp.concatenate([krt_ref[0], jnp.zeros((LANES - ROPE_DIM, tm), F32)], axis=0).T
    slot = pltpu.roll(kr_tok, ROPE_LANE0, 1)
    for hd in range(N_HEADS_A):
        lo = hd * HEAD_BLOCK
        k_ref[0, :, lo:lo + HEAD_BLOCK] = (kn[:, lo:lo + HEAD_BLOCK] + slot).astype(BF16)


def _kv_expand_call(latent, k_rope_t, consts, weights, tm):
    nb, s, _ = latent.shape
    const_in = [weights["w_k_up"], weights["w_v_up"], weights["gk_row"],
                consts["g_k"], consts["e2_k"], consts["ic_k"], consts["v_ones"]]
    out = pl.BlockSpec((1, tm, QA_COLS), lambda b, j: (b, j, 0))
    return pl.pallas_call(
        _kv_expand_kernel,
        grid=(nb, s // tm),
        in_specs=[pl.BlockSpec((1, tm, KV_LORA), lambda b, j: (b, j, 0)),
                  pl.BlockSpec((1, ROPE_DIM, tm), lambda b, j: (b, 0, j))] + [_const_spec(a.shape) for a in const_in],
        out_specs=[out, out],
        out_shape=[jax.ShapeDtypeStruct((nb, s, QA_COLS), BF16), jax.ShapeDtypeStruct((nb, s, QA_COLS), BF16)],
        compiler_params=_cparams(2),
        name="kv_expand",
    )(latent, k_rope_t, *const_in)


def _normalize_heads(acc0, acc1, lane):
    o0 = acc0 / pltpu.roll(acc0, V_DIM_A, 1)
    o1 = acc1 / pltpu.roll(acc1, V_DIM_A, 1)
    return jnp.where(lane < V_DIM_A, o0, pltpu.roll(o1, V_DIM_A, 1))


def _mla_prompt_kernel(tq, q_ref, k_ref, v_ref, o_ref):
    seq = q_ref.shape[1]
    lane = lax.broadcasted_iota(jnp.int32, (tq, LANES), 1)
    row_c = lax.broadcasted_iota(jnp.int32, (tq, tq), 0) // CHUNK
    col_c = lax.broadcasted_iota(jnp.int32, (tq, tq), 1) // CHUNK
    diag_ok = row_c >= col_c
    for qi in range(seq // tq):
        r0 = qi * tq
        accs = []
        for hd in range(2):
            lo = hd * HEAD_BLOCK
            q = q_ref[0, r0:r0 + tq, lo:lo + HEAD_BLOCK]
            s_d = jnp.where(diag_ok, _dot_nt(q, k_ref[0, r0:r0 + tq, lo:lo + HEAD_BLOCK]), NEG_INF)
            m = jnp.max(s_d, axis=-1, keepdims=True)
            if qi > 0:
                s_f = _dot_nt(q, k_ref[0, 0:r0, lo:lo + HEAD_BLOCK])
                m = jnp.maximum(m, jnp.max(s_f, axis=-1, keepdims=True))
            acc = _dot(jnp.exp2(s_d - m).astype(BF16), v_ref[0, r0:r0 + tq, lo:lo + HEAD_BLOCK])
            if qi > 0:
                acc = acc + _dot(jnp.exp2(s_f - m).astype(BF16), v_ref[0, 0:r0, lo:lo + HEAD_BLOCK])
            accs.append(acc)
        o_ref[0, r0:r0 + tq, :] = _normalize_heads(accs[0], accs[1], lane).astype(BF16)


def _mla_prompt_call(q_a, k_a, v_a, tq):
    nb, seq, _ = q_a.shape
    n_pairs = N_HEADS_A // 2
    pair = pl.BlockSpec((1, seq, 2 * HEAD_BLOCK), lambda b, p: (b, 0, p))
    return pl.pallas_call(
        functools.partial(_mla_prompt_kernel, tq),
        grid=(nb, n_pairs),
        in_specs=[pair, pair, pair],
        out_specs=pl.BlockSpec((1, seq, 2 * V_DIM_A), lambda b, p: (b, 0, p)),
        out_shape=jax.ShapeDtypeStruct((nb, seq, VA_COLS), BF16),
        compiler_params=_cparams(2),
        name="mla_prompt",
    )(q_a, k_a, v_a)


def _mla_sample_kernel(q_ref, kc_ref, vc_ref, kn_ref, vn_ref, o_ref):
    rows = q_ref.shape[1]
    lane = lax.broadcasted_iota(jnp.int32, (rows, LANES), 1)
    accs = []
    for hd in range(2):
        lo = hd * HEAD_BLOCK
        q = q_ref[0, :, lo:lo + HEAD_BLOCK]
        s_c = _dot_nt(q, kc_ref[0, :, lo:lo + HEAD_BLOCK])
        s_n = jnp.where(lane < rows, _dot_nt(q, _pad_rows(kn_ref[0, :, lo:lo + HEAD_BLOCK], LANES)), NEG_INF)
        m = jnp.maximum(jnp.max(s_c, axis=-1, keepdims=True), jnp.max(s_n, axis=-1, keepdims=True))
        accs.append(_dot(jnp.exp2(s_c - m).astype(BF16), vc_ref[0, :, lo:lo + HEAD_BLOCK])
                    + _dot(jnp.exp2(s_n - m).astype(BF16), _pad_rows(vn_ref[0, :, lo:lo + HEAD_BLOCK], LANES)))
    o_ref[0] = _normalize_heads(accs[0], accs[1], lane).astype(BF16)


def _mla_sample_call(q_a, k_cache, v_cache, k_new, v_new):
    nb, rows, _ = q_a.shape
    past = k_cache.shape[1]
    n_pairs = N_HEADS_A // 2
    kspec = lambda s: pl.BlockSpec((1, s, 2 * HEAD_BLOCK), lambda b, p: (b, 0, p))
    vspec = lambda s: pl.BlockSpec((1, s, 2 * V_DIM_A), lambda b, p: (b, 0, p))
    return pl.pallas_call(
        _mla_sample_kernel,
        grid=(nb, n_pairs),
        in_specs=[kspec(rows), kspec(past), kspec(past), kspec(rows), kspec(rows)],
        out_specs=vspec(rows),
        out_shape=jax.ShapeDtypeStruct((nb, rows, VA_COLS), BF16),
        compiler_params=_cparams(2),
        name="mla_sample",
    )(q_a, k_cache, v_cache, k_new, v_new)


def _toeplitz_bias(g0, rows):
    far = g0[:, 0:1]
    x0 = jnp.broadcast_to(g0[:, 0:LANES], (rows, LANES))
    x1 = jnp.broadcast_to(g0[:, LANES:2 * LANES], (rows, LANES))
    row = lax.broadcasted_iota(jnp.int32, (rows, LANES), 0)
    lane = lax.broadcasted_iota(jnp.int32, (rows, LANES), 1)
    step = 1
    while step < rows:
        r0 = pltpu.roll(x0, step, 1)
        r1 = pltpu.roll(x1, step, 1)
        keep = lane >= step
        take = (row & step) != 0
        x0, x1 = jnp.where(take, jnp.where(keep, r0, r1), x0), jnp.where(take, jnp.where(keep, r1, r0), x1)
        step *= 2
    return jnp.where(lane < row, far, x0), x1, far


def _band_bias_kernel(rb_ref, bias_ref):
    hd = pl.program_id(0)
    tw0, tw1, far = _toeplitz_bias(rb_ref[pl.ds(hd, 1), :], LANES)
    far_blk = jnp.broadcast_to(far, (LANES, LANES))
    n_blk = BAND_WIN // LANES
    row_c = lax.broadcasted_iota(jnp.int32, (LANES, LANES), 0) // CHUNK
    lane = lax.broadcasted_iota(jnp.int32, (LANES, LANES), 1)
    for half in range(BAND_TQ // LANES):
        first_tw = BAND_WINDOW // LANES - 1 + half
        for cb in range(n_blk):
            blk = tw0 if cb == first_tw else (tw1 if cb == first_tw + 1 else far_blk)
            q_c = row_c + half * (LANES // CHUNK)
            col_c = (cb * LANES + lane) // CHUNK
            ok = (col_c >= q_c) & (col_c <= q_c + LEFT_CHUNKS)
            bias_ref[0, half * LANES:(half + 1) * LANES, cb * LANES:(cb + 1) * LANES] = jnp.where(ok, blk * LOG2E, NEG_INF)


def _band_bias_call(rb_rev):
    return pl.pallas_call(
        _band_bias_kernel,
        grid=(N_HEADS_B,),
        in_specs=[_const_spec(rb_rev.shape)],
        out_specs=pl.BlockSpec((1, BAND_TQ, BAND_WIN), lambda h: (h, 0, 0)),
        out_shape=jax.ShapeDtypeStruct((N_HEADS_B, BAND_TQ, BAND_WIN), F32),
        compiler_params=_cparams(1),
        name="band_bias",
    )(rb_rev)


def _split_heads(q, lane):
    zero = jnp.zeros_like(q)
    return jnp.concatenate([jnp.where(lane < HEAD_DIM_B, q, zero), jnp.where(lane >= HEAD_DIM_B, q, zero)], axis=0)


def _band_prompt_kernel(q_ref, k_ref, v_ref, bias_ref, o_ref, vext_ref):
    seq = q_ref.shape[1]
    lane_q = lax.broadcasted_iota(jnp.int32, (BAND_TQ, LANES), 1)
    vext_ref[:, 0:LANES] = v_ref[0]
    vext_ref[:, LANES:2 * LANES] = jnp.ones((seq, LANES), BF16)
    for t in range(seq // BAND_TQ):
        t0 = t * BAND_TQ
        k_lo = max(t0 - BAND_WINDOW, 0)
        w = t0 + BAND_TQ - k_lo
        q2 = _split_heads(q_ref[0, t0:t0 + BAND_TQ, :], lane_q)
        bias2 = jnp.concatenate([bias_ref[0, :, BAND_WIN - w:BAND_WIN], bias_ref[1, :, BAND_WIN - w:BAND_WIN]], axis=0)
        s = _dot_nt(q2, k_ref[0, k_lo:t0 + BAND_TQ, :]) + bias2
        p = jnp.exp2(s - jnp.max(s, axis=-1, keepdims=True))
        acc = _dot(p.astype(BF16), vext_ref[k_lo:t0 + BAND_TQ, :])
        o2 = acc[:, 0:LANES] / acc[:, LANES:2 * LANES]
        o_ref[0, t0:t0 + BAND_TQ, :] = jnp.where(lane_q < HEAD_DIM_B, o2[0:BAND_TQ], o2[BAND_TQ:2 * BAND_TQ]).astype(BF16)


def _band_prompt_call(q_b, k_b, v_b, bias):
    nb, seq, _ = q_b.shape
    n_pairs = N_HEADS_B // 2
    spec = pl.BlockSpec((1, seq, LANES), lambda b, p: (b, 0, p))
    return pl.pallas_call(
        _band_prompt_kernel,
        grid=(nb, n_pairs),
        in_specs=[spec, spec, spec, pl.BlockSpec((2, BAND_TQ, BAND_WIN), lambda b, p: (p, 0, 0))],
        out_specs=spec,
        out_shape=jax.ShapeDtypeStruct((nb, seq, B_COLS), BF16),
        scratch_shapes=[pltpu.VMEM((seq, 2 * LANES), BF16)],
        compiler_params=_cparams(2),
        name="band_prompt",
    )(q_b, k_b, v_b, bias)


def _band_sample_kernel(q_ref, kct_ref, vct_ref, kn_ref, vn_ref, bias_ref, o_ref):
    rows = q_ref.shape[1]
    n_cache = kct_ref.shape[2]
    lane = lax.broadcasted_iota(jnp.int32, (rows, LANES), 1)
    lane2 = lax.broadcasted_iota(jnp.int32, (2 * rows, LANES), 1)
    for pair in range(N_HEADS_B // 2):
        cols = slice(pair * LANES, (pair + 1) * LANES)
        q2 = _split_heads(q_ref[0, :, cols], lane)
        kct = kct_ref[0, cols, :].astype(BF16)
        vct = vct_ref[0, cols, :].astype(BF16)
        kn = _pad_rows(kn_ref[0, :, cols], LANES)
        vn = _pad_rows(vn_ref[0, :, cols], LANES)
        bias_c = jnp.concatenate([bias_ref[2 * pair, :, 0:n_cache], bias_ref[2 * pair + 1, :, 0:n_cache]], axis=0)
        bias_n = jnp.concatenate([bias_ref[2 * pair, :, n_cache:n_cache + LANES],
                                  bias_ref[2 * pair + 1, :, n_cache:n_cache + LANES]], axis=0)
        s_c = _dot(q2, kct) + bias_c
        s_n = jnp.where(lane2 < rows, _dot_nt(q2, kn) + bias_n, NEG_INF)
        m = jnp.maximum(jnp.max(s_c, axis=-1, keepdims=True), jnp.max(s_n, axis=-1, keepdims=True))
        p_c = jnp.exp2(s_c - m)
        p_n = jnp.exp2(s_n - m)
        l = jnp.sum(p_c, axis=-1, keepdims=True) + jnp.sum(p_n, axis=-1, keepdims=True)
        o2 = (_dot_nt(p_c.astype(BF16), vct) + _dot(p_n.astype(BF16), vn)) / l
        o_ref[0, :, cols] = jnp.where(lane < HEAD_DIM_B, o2[0:rows], o2[rows:2 * rows]).astype(BF16)


def _band_sample_call(q_b, k_cache_t, v_cache_t, k_new, v_new, bias):
    nb, rows, _ = q_b.shape
    n_cache = k_cache_t.shape[2]
    tok = pl.BlockSpec((1, rows, B_COLS), lambda b: (b, 0, 0))
    cache = pl.BlockSpec((1, B_COLS, n_cache), lambda b: (b, 0, 0))
    return pl.pallas_call(
        _band_sample_kernel,
        grid=(nb,),
        in_specs=[tok, cache, cache, tok, tok, pl.BlockSpec((N_HEADS_B, rows, BAND_WIN), lambda b: (0, 0, 0))],
        out_specs=tok,
        out_shape=jax.ShapeDtypeStruct((nb, rows, B_COLS), BF16),
        compiler_params=_cparams(1),
        name="band_sample",
    )(q_b, k_cache_t, v_cache_t, k_new, v_new, bias)


def _tail_kernel(x_ref, oa_ref, ob_ref, ada_ref, gmix_ref, wg_ref, woa_ref, wob_ref, wout_ref, gffn_ref,
                 wgate_ref, wup_ref, wdown_ref, y_ref):
    x = x_ref[0]
    ada = lambda k: ada_ref[0, :, k * D_MODEL:(k + 1) * D_MODEL]
    h = x * _row_rms(x, 1.0 / D_MODEL) * gmix_ref[...]
    h = (h * (1.0 + ada(1)) + ada(0)).astype(BF16)
    gates = jax.nn.sigmoid(_dot(h, wg_ref[...]))
    y_a = _dot(oa_ref[0], woa_ref[...])
    y_b = _dot(ob_ref[0], wob_ref[...])
    mixed = gates[:, 0:D_MODEL] * y_a + gates[:, D_MODEL:2 * D_MODEL] * y_b
    x1 = x + ada(2) * _dot(mixed.astype(BF16), wout_ref[...])
    h2 = x1 * _row_rms(x1, 1.0 / D_MODEL) * gffn_ref[...]
    h2 = (h2 * (1.0 + ada(4)) + ada(3)).astype(BF16)
    acc = jnp.zeros_like(x1)
    for c in range(D_FF // FF_CHUNK):
        cols = slice(c * FF_CHUNK, (c + 1) * FF_CHUNK)
        g = _dot(h2, wgate_ref[:, cols])
        u = _dot(h2, wup_ref[:, cols])
        act = (g * jax.nn.sigmoid(g) * u).astype(BF16)
        acc = acc + _dot(act, wdown_ref[cols, :])
    y_ref[0] = x1 + ada(5) * acc


def _tail_call(x, o_a, o_b, ada, weights, tm):
    nb, sb, _ = x.shape
    if ada.shape[1] == 1:
        ada_spec = pl.BlockSpec((1, 1, 6 * D_MODEL), lambda b, j: (b, 0, 0))
    else:
        ada_spec = pl.BlockSpec((1, tm, 6 * D_MODEL), lambda b, j: (b, j, 0))
    tok = lambda c: pl.BlockSpec((1, tm, c), lambda b, j: (b, j, 0))
    const_in = [weights["g_mix"], weights["w_in_gate"], weights["w_o_a"], weights["w_o_b"], weights["w_out"],
                weights["g_ffn"], weights["w_gate"], weights["w_up"], weights["w_down"]]
    return pl.pallas_call(
        _tail_kernel,
        grid=(nb, sb // tm),
        in_specs=[tok(D_MODEL), tok(VA_COLS), tok(B_COLS), ada_spec] + [_const_spec(a.shape) for a in const_in],
        out_specs=tok(D_MODEL),
        out_shape=jax.ShapeDtypeStruct((nb, sb, D_MODEL), F32),
        compiler_params=_cparams(2),
        name="tail",
    )(x, o_a, o_b, ada, *const_in)


def _group_constants():
    def pack(g, inv_cnt):
        e = g.T
        ic = np.ones((1, LANES), np.float32)
        ic[0, :len(inv_cnt)] = inv_cnt
        return jnp.asarray(g, BF16), jnp.asarray(np.concatenate([e, e], axis=0), BF16), jnp.asarray(ic)

    g_q = np.zeros((QA_COLS, LANES), np.float32)
    g_k = np.zeros((QA_COLS, LANES), np.float32)
    for hd in range(N_HEADS_A):
        g_q[hd * HEAD_BLOCK:hd * HEAD_BLOCK + NOPE_DIM, hd] = 1.0
        g_q[hd * HEAD_BLOCK + ROPE_LANE0:hd * HEAD_BLOCK + ROPE_LANE0 + ROPE_DIM, N_HEADS_A + hd] = 1.0
        g_k[hd * HEAD_BLOCK:hd * HEAD_BLOCK + NOPE_DIM, hd] = 1.0
    g_b = np.zeros((B_COLS, LANES), np.float32)
    for hd in range(N_HEADS_B):
        g_b[hd * HEAD_DIM_B:(hd + 1) * HEAD_DIM_B, hd] = 1.0
    c = {}
    c["g_q"], c["e2_q"], c["ic_q"] = pack(g_q, [1.0 / NOPE_DIM] * N_HEADS_A + [1.0 / ROPE_DIM] * N_HEADS_A)
    c["g_k"], c["e2_k"], c["ic_k"] = pack(g_k, [1.0 / NOPE_DIM] * N_HEADS_A)
    c["g_b"], c["e2_b"], c["ic_b"] = pack(g_b, [1.0 / HEAD_DIM_B] * N_HEADS_B)
    v_ones = np.zeros((N_HEADS_A, HEAD_BLOCK), np.float32)
    v_ones[:, V_DIM_A:] = 1.0
    c["v_ones"] = jnp.asarray(v_ones.reshape(1, QA_COLS))
    return c


def _rope_tables(pos):
    inv_freq = ROPE_BASE ** (-jnp.arange(HALF_ROPE, dtype=F32) / HALF_ROPE)
    ang = pos.astype(F32)[:, None] * inv_freq[None, :]
    cos, sin = jnp.cos(ang), jnp.sin(ang)
    n = pos.shape[0]
    ones = jnp.ones((n, ROPE_LANE0), F32)
    zeros = jnp.zeros((n, ROPE_LANE0), F32)
    pad1 = jnp.ones((n, LANES - ROPE_LANE0 - ROPE_DIM), F32)
    pad0 = jnp.zeros((n, LANES - ROPE_LANE0 - ROPE_DIM), F32)
    z16 = jnp.zeros((n, HALF_ROPE), F32)
    cos_t = jnp.concatenate([ones, cos, cos, pad1], axis=1)
    sin_a = jnp.concatenate([zeros, -sin, z16, pad0], axis=1)
    sin_b = jnp.concatenate([zeros, z16, sin, pad0], axis=1)
    return cos_t, sin_a, sin_b


def _layer_weights(l, w_in, g_norm_mix, g_q_lora, w_q_up, g_kv_lora, w_kv_up, g_qn_a, g_kn_a, g_qr_a, g_kr_a,
                   g_q_b, g_k_b, w_o_a, w_o_b, w_out, g_norm_ffn, w_gate, w_up, w_down):
    wi = w_in[l]
    kpe_block = jnp.pad(wi[:, OFF_KPE:OFF_KPE + ROPE_DIM], ((0, 0), (ROPE_LANE0, LANES - ROPE_LANE0 - ROPE_DIM)))
    n_mix = OFF_KPE + ROPE_DIM + COL_QKV_B
    w = {}
    w["w_in_mix"] = jnp.concatenate([wi[:, 0:OFF_KPE], kpe_block, wi[:, OFF_KPE + ROPE_DIM:n_mix]], axis=1).astype(BF16)
    w["w_in_gate"] = wi[:, n_mix:n_mix + N_GATE_COLS].astype(BF16)
    wq3 = w_q_up[l].reshape(Q_LORA, N_HEADS_A, QK_DIM_A)
    w["w_q_up"] = jnp.pad(wq3, ((0, 0), (0, 0), (0, HEAD_BLOCK - QK_DIM_A))).reshape(Q_LORA, QA_COLS).astype(BF16)
    wkv3 = w_kv_up[l].reshape(KV_LORA, N_HEADS_A, NOPE_DIM + V_DIM_A)
    w["w_k_up"] = jnp.pad(wkv3[..., :NOPE_DIM], ((0, 0), (0, 0), (0, HEAD_BLOCK - NOPE_DIM))).reshape(KV_LORA, QA_COLS).astype(BF16)
    w["w_v_up"] = jnp.pad(wkv3[..., NOPE_DIM:], ((0, 0), (0, 0), (0, HEAD_BLOCK - V_DIM_A))).reshape(KV_LORA, QA_COLS).astype(BF16)
    zpad = jnp.zeros((HEAD_BLOCK - QK_DIM_A,), F32)
    w["gq_row"] = jnp.tile(jnp.concatenate([g_qn_a[l], g_qr_a[l], zpad]), N_HEADS_A).reshape(1, QA_COLS)
    w["gk_row"] = jnp.tile(jnp.concatenate([g_kn_a[l], jnp.zeros((HEAD_BLOCK - NOPE_DIM,), F32)]), N_HEADS_A).reshape(1, QA_COLS)
    w["gkr_row"] = jnp.concatenate([jnp.zeros((ROPE_LANE0,), F32), g_kr_a[l], zpad]).reshape(1, LANES)
    w["gqb_row"] = jnp.tile(g_q_b[l], N_HEADS_B).reshape(1, B_COLS)
    w["gkb_row"] = jnp.tile(g_k_b[l], N_HEADS_B).reshape(1, B_COLS)
    w["g_mix"] = g_norm_mix[l].reshape(1, D_MODEL)
    w["g_q_lora"] = g_q_lora[l].reshape(1, Q_LORA)
    w["g_kv_lora"] = g_kv_lora[l].reshape(1, KV_LORA)
    w["g_ffn"] = g_norm_ffn[l].reshape(1, D_MODEL)
    w["w_o_a"] = w_o_a[l].astype(BF16)
    w["w_o_b"] = w_o_b[l].astype(BF16)
    w["w_out"] = w_out[l].astype(BF16)
    w["w_gate"] = w_gate[l].astype(BF16)
    w["w_up"] = w_up[l].astype(BF16)
    w["w_down"] = w_down[l].astype(BF16)
    return w


def kernel(x_prompt, x_sample, c_prompt, c_sample, cache_kv_latent, cache_k_rope, cache_band_k, cache_band_v, w_ada, b_ada, g_norm_mix, w_in, g_q_lora, w_q_up, g_kv_lora, w_kv_up, g_qn_a, g_kn_a, g_qr_a, g_kr_a, g_q_b, g_k_b, rel_bias, w_o_a, w_o_b, w_out, g_norm_ffn, w_gate, w_up, w_down):
    depth = w_in.shape[0]
    nb, seq, _ = x_prompt.shape
    nbs, sd, _ = x_sample.shape
    past = cache_kv_latent.shape[2]
    n_buf = cache_band_k.shape[2]
    keep = min(BAND_WINDOW, seq)
    assert depth == 1 and nbs * sd == LANES and n_buf == BAND_WINDOW and seq % 512 == 0 and past % 512 == 0
    tm = 512
    rows_s = nbs * sd

    consts = _group_constants()
    tab_p = _rope_tables(jnp.arange(seq))
    tab_s = _rope_tables(past + (jnp.arange(rows_s) % sd))
    xs = x_sample.reshape(1, rows_s, D_MODEL)

    l = 0
    wts = _layer_weights(l, w_in, g_norm_mix, g_q_lora, w_q_up, g_kv_lora, w_kv_up, g_qn_a, g_kn_a, g_qr_a, g_kr_a,
                         g_q_b, g_k_b, w_o_a, w_o_b, w_out, g_norm_ffn, w_gate, w_up, w_down)
    band_bias = _band_bias_call(rel_bias[l][:, 2 * REL_CLIP:0:-1])

    ada = _ada_call(jnp.concatenate([c_prompt, c_sample], axis=0), w_ada[l], b_ada[l])
    ada_p = ada[:nb].reshape(nb, 1, 6 * D_MODEL)
    ada_s = jnp.repeat(ada[nb:], sd, axis=0).reshape(1, rows_s, 6 * D_MODEL)

    qa, lat, _, krt, qb, kb, vb, kbt_tail, vbt_tail = _mixer_in_call(x_prompt, ada_p, tab_p, consts, wts, tm, keep, True)
    k_a, v_a = _kv_expand_call(lat, krt, consts, wts, tm)
    o_a = _mla_prompt_call(qa, k_a, v_a, 512)
    o_b = _band_prompt_call(qb, kb, vb, band_bias)
    y_p = _tail_call(x_prompt, o_a, o_b, ada_p, wts, tm)

    qa_s, lat_s, kr_s, krt_s, qb_s, kb_s, vb_s, kb_s32, vb_s32 = _mixer_in_call(xs, ada_s, tab_s, consts, wts,
                                                                                rows_s, rows_s, False)
    kc, vc = _kv_expand_call(cache_kv_latent[l], jnp.transpose(cache_k_rope[l], (0, 2, 1)), consts, wts, tm)
    kn, vn = _kv_expand_call(lat_s, krt_s, consts, wts, rows_s)
    o_a_s = _mla_sample_call(qa_s.reshape(nbs, sd, QA_COLS), kc, vc,
                             kn.reshape(nbs, sd, QA_COLS), vn.reshape(nbs, sd, QA_COLS))
    feat_major = lambda c: jnp.transpose(c, (0, 2, 3, 1)).reshape(nbs, B_COLS, n_buf)
    o_b_s = _band_sample_call(qb_s.reshape(nbs, sd, B_COLS), feat_major(cache_band_k[l]), feat_major(cache_band_v[l]),
                              kb_s.reshape(nbs, sd, B_COLS), vb_s.reshape(nbs, sd, B_COLS), band_bias)
    y_s = _tail_call(xs, o_a_s.reshape(1, rows_s, VA_COLS), o_b_s.reshape(1, rows_s, B_COLS), ada_s, wts, rows_s)

    tok_major = lambda t: jnp.transpose(t.reshape(nb, N_HEADS_B, HEAD_DIM_B, keep), (0, 3, 1, 2))[None]
    return (y_p, y_s.reshape(nbs, sd, D_MODEL),
            lat.reshape(1, nb, seq, KV_LORA), jnp.transpose(krt, (0, 2, 1))[None],
            tok_major(kbt_tail), tok_major(vbt_tail),
            lat_s.reshape(1, nbs, sd, KV_LORA), kr_s.reshape(1, nbs, sd, ROPE_DIM),
            kb_s32.reshape(1, nbs, sd, N_HEADS_B, HEAD_DIM_B), vb_s32.reshape(1, nbs, sd, N_HEADS_B, HEAD_DIM_B))
```

```python
import functools

import jax
import jax.numpy as jnp
import numpy as np
from jax import lax
from jax.experimental import pallas as pl
from jax.experimental.pallas import tpu as pltpu

D_MODEL = 1024
CHUNK = 64
EPS = 1e-6
NEG_INF = -1e30
N_HEADS_A = 8
NOPE_DIM = 64
ROPE_DIM = 32
HALF_ROPE = ROPE_DIM // 2
V_DIM_A = 64
QK_DIM_A = NOPE_DIM + ROPE_DIM
Q_LORA = 384
KV_LORA = 256
ROPE_BASE = 10000.0
N_HEADS_B = 8
HEAD_DIM_B = 64
LEFT_CHUNKS = 8
BAND_WINDOW = LEFT_CHUNKS * CHUNK
REL_CLIP = 128
D_FF = -(-(8 * D_MODEL) // (3 * 256)) * 256
COL_QKV_B = 3 * N_HEADS_B * HEAD_DIM_B
N_GATE_COLS = 2 * D_MODEL

LANES = 128
HEAD_BLOCK = LANES
ROPE_LANE0 = NOPE_DIM
QA_COLS = N_HEADS_A * HEAD_BLOCK
VA_COLS = N_HEADS_A * V_DIM_A
B_COLS = N_HEADS_B * HEAD_DIM_B
MIX_COLS = Q_LORA + KV_LORA + LANES + COL_QKV_B
OFF_CKV = Q_LORA
OFF_KPE = Q_LORA + KV_LORA
OFF_QB = OFF_KPE + LANES
OFF_KB = OFF_QB + B_COLS
OFF_VB = OFF_KB + B_COLS
FF_CHUNK = 256
LOG2E = 1.4426950408889634
BAND_TQ = 256
BAND_WIN = BAND_WINDOW + BAND_TQ
VMEM_LIMIT = 56 * 1024 * 1024

BF16 = jnp.bfloat16
F32 = jnp.float32


def _cparams(n_axes):
    return pltpu.CompilerParams(dimension_semantics=("arbitrary",) * n_axes, vmem_limit_bytes=VMEM_LIMIT)


def _const_spec(shape):
    nd = len(shape)
    return pl.BlockSpec(shape, lambda *_: (0,) * nd, pipeline_mode=pl.Buffered(1))


def _dot(a, b):
    return jnp.dot(a, b, preferred_element_type=F32)


def _dot_nt(a, b):
    return lax.dot_general(a, b, (((1,), (1,)), ((), ())), preferred_element_type=F32)


def _pad_rows(x, rows):
    return jnp.concatenate([x, jnp.zeros((rows - x.shape[0], x.shape[1]), x.dtype)], axis=0)


def _row_rms(x, inv_n):
    return lax.rsqrt(jnp.sum(x * x, axis=-1, keepdims=True) * inv_n + EPS)


def _group_rms(x, g_ref, e2_ref, invcnt_ref):
    s = _dot((x * x).astype(BF16), g_ref[...])
    r = lax.rsqrt(s * invcnt_ref[...] + EPS)
    r_hi = r.astype(BF16)
    r_lo = (r - r_hi.astype(F32)).astype(BF16)
    return _dot(jnp.concatenate([r_hi, r_lo], axis=1), e2_ref[...])


def _rope_block(x, cos_t, sin_a, sin_b):
    return x * cos_t + pltpu.roll(x, LANES - HALF_ROPE, 1) * sin_a + pltpu.roll(x, HALF_ROPE, 1) * sin_b


def _ada_kernel(c_ref, w_ref, b_ref, o_ref):
    c = c_ref[...]
    a = (c * jax.nn.sigmoid(c)).astype(BF16)
    o_ref[...] = _dot(a, w_ref[...].astype(BF16)) + b_ref[...]


def _ada_call(c_all, w_ada, b_ada):
    rows = c_all.shape[0]
    n_out = w_ada.shape[1]
    tn = D_MODEL
    return pl.pallas_call(
        _ada_kernel,
        grid=(n_out // tn,),
        in_specs=[pl.BlockSpec((rows, D_MODEL), lambda n: (0, 0)),
                  pl.BlockSpec((D_MODEL, tn), lambda n: (0, n)),
                  pl.BlockSpec((1, tn), lambda n: (0, n))],
        out_specs=pl.BlockSpec((rows, tn), lambda n: (0, n)),
        out_shape=jax.ShapeDtypeStruct((rows, n_out), F32),
        compiler_params=_cparams(1),
        name="ada",
    )(c_all, w_ada, b_ada.reshape(1, n_out))


def _mixer_in_kernel(first_tail, tail_feat_major,
                     x_ref, ada_ref, cos_ref, sina_ref, sinb_ref, gmix_ref, w_in_ref, gql_ref, wq_ref, gkv_ref,
                     gq_row_ref, gkr_row_ref, gqb_row_ref, gkb_row_ref,
                     gq_ref, e2q_ref, icq_ref, gb_ref, e2b_ref, icb_ref,
                     qa_ref, lat_ref, kr_ref, krt_ref, qb_ref, kb_ref, vb_ref, kbt_ref, vbt_ref):
    x = x_ref[0]
    shift = ada_ref[0, :, 0:D_MODEL]
    scale = ada_ref[0, :, D_MODEL:2 * D_MODEL]
    h = x * _row_rms(x, 1.0 / D_MODEL) * gmix_ref[...]
    h = h * (1.0 + scale) + shift
    z = _dot(h.astype(BF16), w_in_ref[...])

    cos_t = cos_ref[...]
    sin_a = sina_ref[...]
    sin_b = sinb_ref[...]

    c_q = z[:, 0:Q_LORA]
    cqn = c_q * _row_rms(c_q, 1.0 / Q_LORA) * gql_ref[...]
    q_raw = _dot(cqn.astype(BF16), wq_ref[...])
    r_full = _group_rms(q_raw, gq_ref, e2q_ref, icq_ref)
    q_gain = gq_row_ref[...] * (QK_DIM_A ** -0.5 * LOG2E)
    for hd in range(N_HEADS_A):
        hb = slice(hd * HEAD_BLOCK, (hd + 1) * HEAD_BLOCK)
        qa_ref[0, :, hb] = _rope_block(q_raw[:, hb] * r_full[:, hb] * q_gain, cos_t, sin_a, sin_b).astype(BF16)

    c_kv = z[:, OFF_CKV:OFF_CKV + KV_LORA]
    lat_ref[0] = c_kv * _row_rms(c_kv, 1.0 / KV_LORA) * gkv_ref[...]

    kp = z[:, OFF_KPE:OFF_KPE + LANES]
    kr = _rope_block(kp * _row_rms(kp, 1.0 / ROPE_DIM) * gkr_row_ref[...], cos_t, sin_a, sin_b)
    kr_ref[0] = pltpu.roll(kr, LANES - ROPE_LANE0, 1)[:, 0:ROPE_DIM]
    krt_ref[0] = kr.T[ROPE_LANE0:ROPE_LANE0 + ROPE_DIM, :]

    zq = z[:, OFF_QB:OFF_QB + B_COLS]
    qb_ref[0] = (zq * _group_rms(zq, gb_ref, e2b_ref, icb_ref)
                 * (gqb_row_ref[...] * (HEAD_DIM_B ** -0.5 * LOG2E))).astype(BF16)
    zk = z[:, OFF_KB:OFF_KB + B_COLS]
    k_b = zk * _group_rms(zk, gb_ref, e2b_ref, icb_ref) * gkb_row_ref[...]
    v_b = z[:, OFF_VB:OFF_VB + B_COLS]
    kb_ref[0] = k_b.astype(BF16)
    vb_ref[0] = v_b.astype(BF16)

    @pl.when(pl.program_id(1) >= first_tail)
    def _():
        kbt_ref[0] = k_b.T if tail_feat_major else k_b
        vbt_ref[0] = v_b.T if tail_feat_major else v_b


def _mixer_in_call(x, ada, tables, consts, weights, tm, keep, tail_feat_major):
    nb, sb, _ = x.shape
    nj = sb // tm
    first_tail = (sb - keep) // tm
    ada_rows = ada.shape[1]
    if ada_rows == 1:
        ada_spec = pl.BlockSpec((1, 1, 6 * D_MODEL), lambda b, j: (b, 0, 0))
    else:
        ada_spec = pl.BlockSpec((1, tm, 6 * D_MODEL), lambda b, j: (b, j, 0))
    tab_spec = pl.BlockSpec((tm, LANES), lambda b, j: (j, 0))
    tok = lambda c: pl.BlockSpec((1, tm, c), lambda b, j: (b, j, 0))
    tail_idx = lambda j: jnp.maximum(j - first_tail, 0)
    if tail_feat_major:
        tail = pl.BlockSpec((1, B_COLS, tm), lambda b, j: (b, 0, tail_idx(j)))
        tail_shape = jax.ShapeDtypeStruct((nb, B_COLS, keep), F32)
    else:
        tail = pl.BlockSpec((1, tm, B_COLS), lambda b, j: (b, tail_idx(j), 0))
        tail_shape = jax.ShapeDtypeStruct((nb, keep, B_COLS), F32)
    const_in = [weights["g_mix"], weights["w_in_mix"], weights["g_q_lora"], weights["w_q_up"], weights["g_kv_lora"],
                weights["gq_row"], weights["gkr_row"], weights["gqb_row"], weights["gkb_row"],
                consts["g_q"], consts["e2_q"], consts["ic_q"], consts["g_b"], consts["e2_b"], consts["ic_b"]]
    out_shape = [jax.ShapeDtypeStruct((nb, sb, QA_COLS), BF16),
                 jax.ShapeDtypeStruct((nb, sb, KV_LORA), F32),
                 jax.ShapeDtypeStruct((nb, sb, ROPE_DIM), F32),
                 jax.ShapeDtypeStruct((nb, ROPE_DIM, sb), F32),
                 jax.ShapeDtypeStruct((nb, sb, B_COLS), BF16),
                 jax.ShapeDtypeStruct((nb, sb, B_COLS), BF16),
                 jax.ShapeDtypeStruct((nb, sb, B_COLS), BF16),
                 tail_shape, tail_shape]
    rope_t = pl.BlockSpec((1, ROPE_DIM, tm), lambda b, j: (b, 0, j))
    return pl.pallas_call(
        functools.partial(_mixer_in_kernel, first_tail, tail_feat_major),
        grid=(nb, nj),
        in_specs=[tok(D_MODEL), ada_spec, tab_spec, tab_spec, tab_spec] + [_const_spec(a.shape) for a in const_in],
        out_specs=[tok(QA_COLS), tok(KV_LORA), tok(ROPE_DIM), rope_t, tok(B_COLS), tok(B_COLS), tok(B_COLS), tail, tail],
        out_shape=out_shape,
        compiler_params=_cparams(2),
        name="mixer_in",
    )(x, ada, tables[0], tables[1], tables[2], *const_in)


def _kv_expand_kernel(lat_ref, krt_ref, wkv_ref, gk_row_ref, k_ref, v_ref):
    lat = lat_ref[0].astype(BF16)
    tm = lat.shape[0]
    kv = _dot(lat, wkv_ref[...])
    kr_tok = jnp.concatenate([krt_ref[0], jnp.zeros((LANES - ROPE_DIM, tm), F32)], axis=0).T
    slot = pltpu.roll(kr_tok, ROPE_LANE0, 1)
    nope = lax.broadcasted_iota(jnp.int32, (tm, LANES), 1) < NOPE_DIM
    for hd in range(N_HEADS_A):
        lo = hd * HEAD_BLOCK
        blk = kv[:, lo:lo + HEAD_BLOCK]
        ssq = jnp.sum(jnp.where(nope, blk * blk, 0.0), axis=-1, keepdims=True)
        r = lax.rsqrt(ssq * (1.0 / NOPE_DIM) + EPS)
        k_ref[0, :, lo:lo + HEAD_BLOCK] = jnp.where(nope, blk * r * gk_row_ref[...], slot).astype(BF16)
        v_ref[0, :, lo:lo + HEAD_BLOCK] = jnp.where(nope, 1.0, blk).astype(BF16)


def _kv_expand_call(latent, k_rope_t, weights, tm):
    nb, s, _ = latent.shape
    const_in = [weights["w_kv_up"], weights["gk_row"]]
    out = pl.BlockSpec((1, tm, QA_COLS), lambda b, j: (b, j, 0))
    return pl.pallas_call(
        _kv_expand_kernel,
        grid=(nb, s // tm),
        in_specs=[pl.BlockSpec((1, tm, KV_LORA), lambda b, j: (b, j, 0)),
                  pl.BlockSpec((1, ROPE_DIM, tm), lambda b, j: (b, 0, j))] + [_const_spec(a.shape) for a in const_in],
        out_specs=[out, out],
        out_shape=[jax.ShapeDtypeStruct((nb, s, QA_COLS), BF16), jax.ShapeDtypeStruct((nb, s, QA_COLS), BF16)],
        compiler_params=_cparams(2),
        name="kv_expand",
    )(latent, k_rope_t, *const_in)


def _normalize_heads(acc0, acc1, lane):
    o0 = acc0 / pltpu.roll(acc0, V_DIM_A, 1)
    o1 = acc1 / pltpu.roll(acc1, V_DIM_A, 1)
    return jnp.where(lane < V_DIM_A, pltpu.roll(o0, V_DIM_A, 1), o1)


def _mla_prompt_kernel(tq, q_ref, k_ref, v_ref, o_ref):
    seq = q_ref.shape[1]
    n_heads = q_ref.shape[2] // HEAD_BLOCK
    lane = lax.broadcasted_iota(jnp.int32, (tq, LANES), 1)
    row_c = lax.broadcasted_iota(jnp.int32, (tq, tq), 0) // CHUNK
    col_c = lax.broadcasted_iota(jnp.int32, (tq, tq), 1) // CHUNK
    diag_ok = row_c >= col_c
    for qi in reversed(range(seq // tq)):
        r0 = qi * tq
        accs = []
        for hd in range(n_heads):
            hb = slice(hd * HEAD_BLOCK, (hd + 1) * HEAD_BLOCK)
            q = q_ref[0, r0:r0 + tq, hb]
            s_d = jnp.where(diag_ok, _dot_nt(q, k_ref[0, r0:r0 + tq, hb]), NEG_INF)
            m = jnp.max(s_d, axis=-1, keepdims=True)
            if qi > 0:
                s_f = _dot_nt(q, k_ref[0, 0:r0, hb])
                m = jnp.maximum(m, jnp.max(s_f, axis=-1, keepdims=True))
            acc = _dot(jnp.exp2(s_d - m).astype(BF16), v_ref[0, r0:r0 + tq, hb])
            if qi > 0:
                acc = acc + _dot(jnp.exp2(s_f - m).astype(BF16), v_ref[0, 0:r0, hb])
            accs.append(acc)
        for pr in range(n_heads // 2):
            o_ref[0, r0:r0 + tq, pr * LANES:(pr + 1) * LANES] = _normalize_heads(
                accs[2 * pr], accs[2 * pr + 1], lane).astype(BF16)


def _mla_prompt_call(q_a, k_a, v_a, tq, heads_per_step):
    nb, seq, _ = q_a.shape
    n_groups = N_HEADS_A // heads_per_step
    pair = pl.BlockSpec((1, seq, heads_per_step * HEAD_BLOCK), lambda b, p: (b, 0, p))
    return pl.pallas_call(
        functools.partial(_mla_prompt_kernel, tq),
        grid=(nb, n_groups),
        in_specs=[pair, pair, pair],
        out_specs=pl.BlockSpec((1, seq, heads_per_step * V_DIM_A), lambda b, p: (b, 0, p)),
        out_shape=jax.ShapeDtypeStruct((nb, seq, VA_COLS), BF16),
        compiler_params=_cparams(2),
        name="mla_prompt",
    )(q_a, k_a, v_a)


def _mla_sample_kernel(q_ref, kc_ref, vc_ref, kn_ref, vn_ref, o_ref):
    rows = q_ref.shape[1]
    lane = lax.broadcasted_iota(jnp.int32, (rows, LANES), 1)
    accs = []
    for hd in range(2):
        lo = hd * HEAD_BLOCK
        q = q_ref[0, :, lo:lo + HEAD_BLOCK]
        s_c = _dot_nt(q, kc_ref[0, :, lo:lo + HEAD_BLOCK])
        s_n = jnp.where(lane < rows, _dot_nt(q, _pad_rows(kn_ref[0, :, lo:lo + HEAD_BLOCK], LANES)), NEG_INF)
        m = jnp.maximum(jnp.max(s_c, axis=-1, keepdims=True), jnp.max(s_n, axis=-1, keepdims=True))
        accs.append(_dot(jnp.exp2(s_c - m).astype(BF16), vc_ref[0, :, lo:lo + HEAD_BLOCK])
                    + _dot(jnp.exp2(s_n - m).astype(BF16), _pad_rows(vn_ref[0, :, lo:lo + HEAD_BLOCK], LANES)))
    o_ref[0] = _normalize_heads(accs[0], accs[1], lane).astype(BF16)


def _mla_sample_call(q_a, k_cache, v_cache, k_new, v_new):
    nb, rows, _ = q_a.shape
    past = k_cache.shape[1]
    n_pairs = N_HEADS_A // 2
    kspec = lambda s: pl.BlockSpec((1, s, 2 * HEAD_BLOCK), lambda b, p: (b, 0, p))
    vspec = lambda s: pl.BlockSpec((1, s, 2 * V_DIM_A), lambda b, p: (b, 0, p))
    return pl.pallas_call(
        _mla_sample_kernel,
        grid=(nb, n_pairs),
        in_specs=[kspec(rows), kspec(past), kspec(past), kspec(rows), kspec(rows)],
        out_specs=vspec(rows),
        out_shape=jax.ShapeDtypeStruct((nb, rows, VA_COLS), BF16),
        compiler_params=_cparams(2),
        name="mla_sample",
    )(q_a, k_cache, v_cache, k_new, v_new)


def _toeplitz_bias(g0, rows):
    far = g0[:, 0:1]
    x0 = jnp.broadcast_to(g0[:, 0:LANES], (rows, LANES))
    x1 = jnp.broadcast_to(g0[:, LANES:2 * LANES], (rows, LANES))
    row = lax.broadcasted_iota(jnp.int32, (rows, LANES), 0)
    lane = lax.broadcasted_iota(jnp.int32, (rows, LANES), 1)
    step = 1
    while step < rows:
        r0 = pltpu.roll(x0, step, 1)
        r1 = pltpu.roll(x1, step, 1)
        keep = lane >= step
        take = (row & step) != 0
        x0, x1 = jnp.where(take, jnp.where(keep, r0, r1), x0), jnp.where(take, jnp.where(keep, r1, r0), x1)
        step *= 2
    return jnp.where(lane < row, far, x0), x1, far


def _band_bias_kernel(rb_ref, bias_ref):
    hd = pl.program_id(0)
    tw0, tw1, far = _toeplitz_bias(rb_ref[pl.ds(hd, 1), :], LANES)
    far_blk = jnp.broadcast_to(far, (LANES, LANES))
    n_blk = BAND_WIN // LANES
    row_c = lax.broadcasted_iota(jnp.int32, (LANES, LANES), 0) // CHUNK
    lane = lax.broadcasted_iota(jnp.int32, (LANES, LANES), 1)
    for half in range(BAND_TQ // LANES):
        first_tw = BAND_WINDOW // LANES - 1 + half
        for cb in range(n_blk):
            blk = tw0 if cb == first_tw else (tw1 if cb == first_tw + 1 else far_blk)
            q_c = row_c + half * (LANES // CHUNK)
            col_c = (cb * LANES + lane) // CHUNK
            ok = (col_c >= q_c) & (col_c <= q_c + LEFT_CHUNKS)
            bias_ref[0, half * LANES:(half + 1) * LANES, cb * LANES:(cb + 1) * LANES] = jnp.where(ok, blk * LOG2E, NEG_INF)


def _band_bias_call(rb_rev):
    return pl.pallas_call(
        _band_bias_kernel,
        grid=(N_HEADS_B,),
        in_specs=[_const_spec(rb_rev.shape)],
        out_specs=pl.BlockSpec((1, BAND_TQ, BAND_WIN), lambda h: (h, 0, 0)),
        out_shape=jax.ShapeDtypeStruct((N_HEADS_B, BAND_TQ, BAND_WIN), F32),
        compiler_params=_cparams(1),
        name="band_bias",
    )(rb_rev)


def _split_heads(q, lane):
    zero = jnp.zeros_like(q)
    return jnp.concatenate([jnp.where(lane < HEAD_DIM_B, q, zero), jnp.where(lane >= HEAD_DIM_B, q, zero)], axis=0)


def _band_prompt_kernel(q_ref, k_ref, v_ref, bias_ref, o_ref, vext_ref):
    seq = q_ref.shape[1]
    n_pairs = q_ref.shape[2] // LANES
    lane_q = lax.broadcasted_iota(jnp.int32, (BAND_TQ, LANES), 1)
    for pr in range(n_pairs):
        vext_ref[pr, :, 0:LANES] = v_ref[0, :, pr * LANES:(pr + 1) * LANES]
        vext_ref[pr, :, LANES:2 * LANES] = jnp.ones((seq, LANES), BF16)
    for t in reversed(range(seq // BAND_TQ)):
        t0 = t * BAND_TQ
        k_lo = max(t0 - BAND_WINDOW, 0)
        w = t0 + BAND_TQ - k_lo
        for pr in range(n_pairs):
            cols = slice(pr * LANES, (pr + 1) * LANES)
            q2 = _split_heads(q_ref[0, t0:t0 + BAND_TQ, cols], lane_q)
            bias2 = jnp.concatenate([bias_ref[2 * pr, :, BAND_WIN - w:BAND_WIN],
                                     bias_ref[2 * pr + 1, :, BAND_WIN - w:BAND_WIN]], axis=0)
            s = _dot_nt(q2, k_ref[0, k_lo:t0 + BAND_TQ, cols]) + bias2
            p = jnp.exp2(s - jnp.max(s, axis=-1, keepdims=True))
            acc = _dot(p.astype(BF16), vext_ref[pr, k_lo:t0 + BAND_TQ, :])
            o2 = acc[:, 0:LANES] / acc[:, LANES:2 * LANES]
            o_ref[0, t0:t0 + BAND_TQ, cols] = jnp.where(lane_q < HEAD_DIM_B, o2[0:BAND_TQ],
                                                        o2[BAND_TQ:2 * BAND_TQ]).astype(BF16)


def _band_prompt_call(q_b, k_b, v_b, bias, pairs_per_step):
    nb, seq, _ = q_b.shape
    n_groups = N_HEADS_B // (2 * pairs_per_step)
    spec = pl.BlockSpec((1, seq, pairs_per_step * LANES), lambda b, p: (b, 0, p))
    return pl.pallas_call(
        _band_prompt_kernel,
        grid=(nb, n_groups),
        in_specs=[spec, spec, spec, pl.BlockSpec((2 * pairs_per_step, BAND_TQ, BAND_WIN), lambda b, p: (p, 0, 0))],
        out_specs=spec,
        out_shape=jax.ShapeDtypeStruct((nb, seq, B_COLS), BF16),
        scratch_shapes=[pltpu.VMEM((pairs_per_step, seq, 2 * LANES), BF16)],
        compiler_params=_cparams(2),
        name="band_prompt",
    )(q_b, k_b, v_b, bias)


def _band_sample_kernel(q_ref, kct_ref, vct_ref, kn_ref, vn_ref, bias_ref, o_ref):
    rows = q_ref.shape[1]
    n_cache = kct_ref.shape[2]
    lane = lax.broadcasted_iota(jnp.int32, (rows, LANES), 1)
    lane2 = lax.broadcasted_iota(jnp.int32, (2 * rows, LANES), 1)
    for pair in range(N_HEADS_B // 2):
        cols = slice(pair * LANES, (pair + 1) * LANES)
        q2 = _split_heads(q_ref[0, :, cols], lane)
        kct = kct_ref[0, cols, :].astype(BF16)
        vct = vct_ref[0, cols, :].astype(BF16)
        kn = _pad_rows(kn_ref[0, :, cols], LANES)
        vn = _pad_rows(vn_ref[0, :, cols], LANES)
        bias_c = jnp.concatenate([bias_ref[2 * pair, :, 0:n_cache], bias_ref[2 * pair + 1, :, 0:n_cache]], axis=0)
        bias_n = jnp.concatenate([bias_ref[2 * pair, :, n_cache:n_cache + LANES],
                                  bias_ref[2 * pair + 1, :, n_cache:n_cache + LANES]], axis=0)
        s_c = _dot(q2, kct) + bias_c
        s_n = jnp.where(lane2 < rows, _dot_nt(q2, kn) + bias_n, NEG_INF)
        m = jnp.maximum(jnp.max(s_c, axis=-1, keepdims=True), jnp.max(s_n, axis=-1, keepdims=True))
        p_c = jnp.exp2(s_c - m)
        p_n = jnp.exp2(s_n - m)
        l = jnp.sum(p_c, axis=-1, keepdims=True) + jnp.sum(p_n, axis=-1, keepdims=True)
        o2 = (_dot_nt(p_c.astype(BF16), vct) + _dot(p_n.astype(BF16), vn)) / l
        o_ref[0, :, cols] = jnp.where(lane < HEAD_DIM_B, o2[0:rows], o2[rows:2 * rows]).astype(BF16)


def _band_sample_call(q_b, k_cache_t, v_cache_t, k_new, v_new, bias):
    nb, rows, _ = q_b.shape
    n_cache = k_cache_t.shape[2]
    tok = pl.BlockSpec((1, rows, B_COLS), lambda b: (b, 0, 0))
    cache = pl.BlockSpec((1, B_COLS, n_cache), lambda b: (b, 0, 0))
    return pl.pallas_call(
        _band_sample_kernel,
        grid=(nb,),
        in_specs=[tok, cache, cache, tok, tok, pl.BlockSpec((N_HEADS_B, rows, BAND_WIN), lambda b: (0, 0, 0))],
        out_specs=tok,
        out_shape=jax.ShapeDtypeStruct((nb, rows, B_COLS), BF16),
        compiler_params=_cparams(1),
        name="band_sample",
    )(q_b, k_cache_t, v_cache_t, k_new, v_new, bias)


def _tail_kernel(x_ref, oa_ref, ob_ref, ada_ref, gmix_ref, wg_ref, woa_ref, wob_ref, wout_ref, gffn_ref,
                 wgate_ref, wup_ref, wdown_ref, y_ref):
    x = x_ref[0]
    ada = lambda k: ada_ref[0, :, k * D_MODEL:(k + 1) * D_MODEL]
    h = x * _row_rms(x, 1.0 / D_MODEL) * gmix_ref[...]
    h = (h * (1.0 + ada(1)) + ada(0)).astype(BF16)
    gates = jax.nn.sigmoid(_dot(h, wg_ref[...]))
    y_a = _dot(oa_ref[0], woa_ref[...])
    y_b = _dot(ob_ref[0], wob_ref[...])
    mixed = gates[:, 0:D_MODEL] * y_a + gates[:, D_MODEL:2 * D_MODEL] * y_b
    x1 = x + ada(2) * _dot(mixed.astype(BF16), wout_ref[...])
    h2 = x1 * _row_rms(x1, 1.0 / D_MODEL) * gffn_ref[...]
    h2 = (h2 * (1.0 + ada(4)) + ada(3)).astype(BF16)
    acc = jnp.zeros_like(x1)
    for c in range(D_FF // FF_CHUNK):
        cols = slice(c * FF_CHUNK, (c + 1) * FF_CHUNK)
        g = _dot(h2, wgate_ref[:, cols])
        u = _dot(h2, wup_ref[:, cols])
        act = (g * jax.nn.sigmoid(g) * u).astype(BF16)
        acc = acc + _dot(act, wdown_ref[cols, :])
    y_ref[0] = x1 + ada(5) * acc


def _tail_call(x, o_a, o_b, ada, weights, tm):
    nb, sb, _ = x.shape
    if ada.shape[1] == 1:
        ada_spec = pl.BlockSpec((1, 1, 6 * D_MODEL), lambda b, j: (b, 0, 0))
    else:
        ada_spec = pl.BlockSpec((1, tm, 6 * D_MODEL), lambda b, j: (b, j, 0))
    tok = lambda c: pl.BlockSpec((1, tm, c), lambda b, j: (b, j, 0))
    const_in = [weights["g_mix"], weights["w_in_gate"], weights["w_o_a"], weights["w_o_b"], weights["w_out"],
                weights["g_ffn"], weights["w_gate"], weights["w_up"], weights["w_down"]]
    return pl.pallas_call(
        _tail_kernel,
        grid=(nb, sb // tm),
        in_specs=[tok(D_MODEL), tok(VA_COLS), tok(B_COLS), ada_spec] + [_const_spec(a.shape) for a in const_in],
        out_specs=tok(D_MODEL),
        out_shape=jax.ShapeDtypeStruct((nb, sb, D_MODEL), F32),
        compiler_params=_cparams(2),
        name="tail",
    )(x, o_a, o_b, ada, *const_in)


def _group_constants():
    def pack(g, inv_cnt):
        ic = np.ones((1, LANES), np.float32)
        ic[0, :len(inv_cnt)] = inv_cnt
        return jnp.asarray(g, BF16), jnp.asarray(np.concatenate([g.T, g.T], axis=0), BF16), jnp.asarray(ic)

    g_q = np.zeros((QA_COLS, LANES), np.float32)
    for hd in range(N_HEADS_A):
        g_q[hd * HEAD_BLOCK:hd * HEAD_BLOCK + NOPE_DIM, hd] = 1.0
        g_q[hd * HEAD_BLOCK + ROPE_LANE0:hd * HEAD_BLOCK + ROPE_LANE0 + ROPE_DIM, N_HEADS_A + hd] = 1.0
    g_b = np.zeros((B_COLS, LANES), np.float32)
    for hd in range(N_HEADS_B):
        g_b[hd * HEAD_DIM_B:(hd + 1) * HEAD_DIM_B, hd] = 1.0
    c = {}
    c["g_q"], c["e2_q"], c["ic_q"] = pack(g_q, [1.0 / NOPE_DIM] * N_HEADS_A + [1.0 / ROPE_DIM] * N_HEADS_A)
    c["g_b"], c["e2_b"], c["ic_b"] = pack(g_b, [1.0 / HEAD_DIM_B] * N_HEADS_B)
    return c


def _rope_tables(pos):
    inv_freq = ROPE_BASE ** (-jnp.arange(HALF_ROPE, dtype=F32) / HALF_ROPE)
    ang = pos.astype(F32)[:, None] * inv_freq[None, :]
    cos, sin = jnp.cos(ang), jnp.sin(ang)
    n = pos.shape[0]
    ones = jnp.ones((n, ROPE_LANE0), F32)
    zeros = jnp.zeros((n, ROPE_LANE0), F32)
    pad1 = jnp.ones((n, LANES - ROPE_LANE0 - ROPE_DIM), F32)
    pad0 = jnp.zeros((n, LANES - ROPE_LANE0 - ROPE_DIM), F32)
    z16 = jnp.zeros((n, HALF_ROPE), F32)
    cos_t = jnp.concatenate([ones, cos, cos, pad1], axis=1)
    sin_a = jnp.concatenate([zeros, -sin, z16, pad0], axis=1)
    sin_b = jnp.concatenate([zeros, z16, sin, pad0], axis=1)
    return cos_t, sin_a, sin_b


def _layer_weights(l, w_in, g_norm_mix, g_q_lora, w_q_up, g_kv_lora, w_kv_up, g_qn_a, g_kn_a, g_qr_a, g_kr_a,
                   g_q_b, g_k_b, w_o_a, w_o_b, w_out, g_norm_ffn, w_gate, w_up, w_down):
    wi = w_in[l]
    kpe_block = jnp.pad(wi[:, OFF_KPE:OFF_KPE + ROPE_DIM], ((0, 0), (ROPE_LANE0, LANES - ROPE_LANE0 - ROPE_DIM)))
    n_mix = OFF_KPE + ROPE_DIM + COL_QKV_B
    w = {}
    w["w_in_mix"] = jnp.concatenate([wi[:, 0:OFF_KPE], kpe_block, wi[:, OFF_KPE + ROPE_DIM:n_mix]], axis=1).astype(BF16)
    w["w_in_gate"] = wi[:, n_mix:n_mix + N_GATE_COLS].astype(BF16)
    wq3 = w_q_up[l].reshape(Q_LORA, N_HEADS_A, QK_DIM_A)
    w["w_q_up"] = jnp.pad(wq3, ((0, 0), (0, 0), (0, HEAD_BLOCK - QK_DIM_A))).reshape(Q_LORA, QA_COLS).astype(BF16)
    w["w_kv_up"] = w_kv_up[l].astype(BF16)
    zpad = jnp.zeros((HEAD_BLOCK - QK_DIM_A,), F32)
    w["gq_row"] = jnp.concatenate([g_qn_a[l], g_qr_a[l], zpad]).reshape(1, HEAD_BLOCK)
    w["gk_row"] = jnp.concatenate([g_kn_a[l], jnp.zeros((HEAD_BLOCK - NOPE_DIM,), F32)]).reshape(1, HEAD_BLOCK)
    w["gkr_row"] = jnp.concatenate([jnp.zeros((ROPE_LANE0,), F32), g_kr_a[l], zpad]).reshape(1, LANES)
    w["gqb_row"] = jnp.tile(g_q_b[l], N_HEADS_B).reshape(1, B_COLS)
    w["gkb_row"] = jnp.tile(g_k_b[l], N_HEADS_B).reshape(1, B_COLS)
    w["g_mix"] = g_norm_mix[l].reshape(1, D_MODEL)
    w["g_q_lora"] = g_q_lora[l].reshape(1, Q_LORA)
    w["g_kv_lora"] = g_kv_lora[l].reshape(1, KV_LORA)
    w["g_ffn"] = g_norm_ffn[l].reshape(1, D_MODEL)
    w["w_o_a"] = w_o_a[l].astype(BF16)
    w["w_o_b"] = w_o_b[l].astype(BF16)
    w["w_out"] = w_out[l].astype(BF16)
    w["w_gate"] = w_gate[l].astype(BF16)
    w["w_up"] = w_up[l].astype(BF16)
    w["w_down"] = w_down[l].astype(BF16)
    return w


def kernel(x_prompt, x_sample, c_prompt, c_sample, cache_kv_latent, cache_k_rope, cache_band_k, cache_band_v, w_ada, b_ada, g_norm_mix, w_in, g_q_lora, w_q_up, g_kv_lora, w_kv_up, g_qn_a, g_kn_a, g_qr_a, g_kr_a, g_q_b, g_k_b, rel_bias, w_o_a, w_o_b, w_out, g_norm_ffn, w_gate, w_up, w_down):
    depth = w_in.shape[0]
    nb, seq, _ = x_prompt.shape
    nbs, sd, _ = x_sample.shape
    past = cache_kv_latent.shape[2]
    n_buf = cache_band_k.shape[2]
    keep = min(BAND_WINDOW, seq)
    assert depth == 1 and nbs * sd == LANES and n_buf == BAND_WINDOW and seq % 512 == 0 and past % 512 == 0
    tm = 512
    rows_s = nbs * sd

    consts = _group_constants()
    tab_p = _rope_tables(jnp.arange(seq))
    tab_s = _rope_tables(past + (jnp.arange(rows_s) % sd))
    xs = x_sample.reshape(1, rows_s, D_MODEL)

    l = 0
    wts = _layer_weights(l, w_in, g_norm_mix, g_q_lora, w_q_up, g_kv_lora, w_kv_up, g_qn_a, g_kn_a, g_qr_a, g_kr_a,
                         g_q_b, g_k_b, w_o_a, w_o_b, w_out, g_norm_ffn, w_gate, w_up, w_down)
    band_bias = _band_bias_call(rel_bias[l][:, 2 * REL_CLIP:0:-1])

    ada = _ada_call(jnp.concatenate([c_prompt, c_sample], axis=0), w_ada[l], b_ada[l])
    ada_p = ada[:nb].reshape(nb, 1, 6 * D_MODEL)
    ada_s = jnp.repeat(ada[nb:], sd, axis=0).reshape(1, rows_s, 6 * D_MODEL)

    qa, lat, _, krt, qb, kb, vb, kbt_tail, vbt_tail = _mixer_in_call(x_prompt, ada_p, tab_p, consts, wts, tm, keep, True)
    k_a, v_a = _kv_expand_call(lat, krt, wts, tm)
    o_a = _mla_prompt_call(qa, k_a, v_a, 512, 4)
    o_b = _band_prompt_call(qb, kb, vb, band_bias, 2)
    y_p = _tail_call(x_prompt, o_a, o_b, ada_p, wts, tm)

    qa_s, lat_s, kr_s, krt_s, qb_s, kb_s, vb_s, kb_s32, vb_s32 = _mixer_in_call(xs, ada_s, tab_s, consts, wts,
                                                                                rows_s, rows_s, False)
    kc, vc = _kv_expand_call(cache_kv_latent[l], jnp.transpose(cache_k_rope[l], (0, 2, 1)), wts, tm)
    kn, vn = _kv_expand_call(lat_s, krt_s, wts, rows_s)
    o_a_s = _mla_sample_call(qa_s.reshape(nbs, sd, QA_COLS), kc, vc,
                             kn.reshape(nbs, sd, QA_COLS), vn.reshape(nbs, sd, QA_COLS))
    feat_major = lambda c: jnp.transpose(c, (0, 2, 3, 1)).reshape(nbs, B_COLS, n_buf)
    o_b_s = _band_sample_call(qb_s.reshape(nbs, sd, B_COLS), feat_major(cache_band_k[l]), feat_major(cache_band_v[l]),
                              kb_s.reshape(nbs, sd, B_COLS), vb_s.reshape(nbs, sd, B_COLS), band_bias)
    y_s = _tail_call(xs, o_a_s.reshape(1, rows_s, VA_COLS), o_b_s.reshape(1, rows_s, B_COLS), ada_s, wts, rows_s)

    tok_major = lambda t: jnp.transpose(t.reshape(nb, N_HEADS_B, HEAD_DIM_B, keep), (0, 3, 1, 2))[None]
    return (y_p, y_s.reshape(nbs, sd, D_MODEL),
            lat.reshape(1, nb, seq, KV_LORA), jnp.transpose(krt, (0, 2, 1))[None],
            tok_major(kbt_tail), tok_major(vbt_tail),
            lat_s.reshape(1, nbs, sd, KV_LORA), kr_s.reshape(1, nbs, sd, ROPE_DIM),
            kb_s32.reshape(1, nbs, sd, N_HEADS_B, HEAD_DIM_B), vb_s32.reshape(1, nbs, sd, N_HEADS_B, HEAD_DIM_B))
```

```python
import functools

import jax
import jax.numpy as jnp
import numpy as np
from jax import lax
from jax.experimental import pallas as pl
from jax.experimental.pallas import tpu as pltpu

D_MODEL = 1024
CHUNK = 64
EPS = 1e-6
NEG_INF = -1e30
N_HEADS_A = 8
NOPE_DIM = 64
ROPE_DIM = 32
HALF_ROPE = ROPE_DIM // 2
V_DIM_A = 64
QK_DIM_A = NOPE_DIM + ROPE_DIM
Q_LORA = 384
KV_LORA = 256
ROPE_BASE = 10000.0
N_HEADS_B = 8
HEAD_DIM_B = 64
LEFT_CHUNKS = 8
BAND_WINDOW = LEFT_CHUNKS * CHUNK
REL_CLIP = 128
D_FF = -(-(8 * D_MODEL) // (3 * 256)) * 256
COL_QKV_B = 3 * N_HEADS_B * HEAD_DIM_B
N_GATE_COLS = 2 * D_MODEL

LANES = 128
HEAD_BLOCK = LANES
ROPE_LANE0 = NOPE_DIM
QA_COLS = N_HEADS_A * HEAD_BLOCK
VA_COLS = N_HEADS_A * V_DIM_A
B_COLS = N_HEADS_B * HEAD_DIM_B
MIX_COLS = Q_LORA + KV_LORA + LANES + COL_QKV_B
OFF_CKV = Q_LORA
OFF_KPE = Q_LORA + KV_LORA
OFF_QB = OFF_KPE + LANES
OFF_KB = OFF_QB + B_COLS
OFF_VB = OFF_KB + B_COLS
FF_CHUNK = 256
LOG2E = 1.4426950408889634
BAND_TQ = 256
BAND_WIN = BAND_WINDOW + BAND_TQ
VMEM_LIMIT = 56 * 1024 * 1024

BF16 = jnp.bfloat16
F32 = jnp.float32


def _cparams(n_axes):
    return pltpu.CompilerParams(dimension_semantics=("arbitrary",) * n_axes, vmem_limit_bytes=VMEM_LIMIT)


def _const_spec(shape):
    nd = len(shape)
    return pl.BlockSpec(shape, lambda *_: (0,) * nd, pipeline_mode=pl.Buffered(1))


def _dot(a, b):
    return jnp.dot(a, b, preferred_element_type=F32)


def _dot_nt(a, b):
    return lax.dot_general(a, b, (((1,), (1,)), ((), ())), preferred_element_type=F32)


def _pad_rows(x, rows):
    return jnp.concatenate([x, jnp.zeros((rows - x.shape[0], x.shape[1]), x.dtype)], axis=0)


def _row_rms(x, inv_n):
    return lax.rsqrt(jnp.sum(x * x, axis=-1, keepdims=True) * inv_n + EPS)


def _group_rms(x, g_ref, e2_ref, invcnt_ref):
    s = _dot((x * x).astype(BF16), g_ref[...])
    r = lax.rsqrt(s * invcnt_ref[...] + EPS)
    r_hi = r.astype(BF16)
    r_lo = (r - r_hi.astype(F32)).astype(BF16)
    return _dot(jnp.concatenate([r_hi, r_lo], axis=1), e2_ref[...])


def _rope_block(x, cos_t, sin_a, sin_b):
    return x * cos_t + pltpu.roll(x, LANES - HALF_ROPE, 1) * sin_a + pltpu.roll(x, HALF_ROPE, 1) * sin_b


def _ada_kernel(c_ref, w_ref, b_ref, o_ref):
    c = c_ref[...]
    a = (c * jax.nn.sigmoid(c)).astype(BF16)
    o_ref[...] = _dot(a, w_ref[...].astype(BF16)) + b_ref[...]


def _ada_call(c_all, w_ada, b_ada):
    rows = c_all.shape[0]
    n_out = w_ada.shape[1]
    tn = D_MODEL
    return pl.pallas_call(
        _ada_kernel,
        grid=(n_out // tn,),
        in_specs=[pl.BlockSpec((rows, D_MODEL), lambda n: (0, 0)),
                  pl.BlockSpec((D_MODEL, tn), lambda n: (0, n)),
                  pl.BlockSpec((1, tn), lambda n: (0, n))],
        out_specs=pl.BlockSpec((rows, tn), lambda n: (0, n)),
        out_shape=jax.ShapeDtypeStruct((rows, n_out), F32),
        compiler_params=_cparams(1),
        name="ada",
    )(c_all, w_ada, b_ada.reshape(1, n_out))


def _mixer_in_kernel(sub, tail_feat_major,
                     x_ref, ada_ref, cos_ref, sina_ref, sinb_ref, gmix_ref, w_in_ref, gql_ref, wq_ref, gkv_ref,
                     gq_row_ref, gkr_row_ref, gqb_row_ref, gkb_row_ref,
                     gq_ref, e2q_ref, icq_ref, gb_ref, e2b_ref, icb_ref,
                     qa_ref, lat_ref, kr_ref, krt_ref, qb_ref, kb_ref, vb_ref, kbt_ref, vbt_ref):
    n_sub = x_ref.shape[1] // sub
    for si in range(n_sub):
        rows = slice(si * sub, (si + 1) * sub)
        x = x_ref[0, rows, :]
        ada_rows = rows if ada_ref.shape[1] > 1 else slice(None)
        shift = ada_ref[0, ada_rows, 0:D_MODEL]
        scale = ada_ref[0, ada_rows, D_MODEL:2 * D_MODEL]
        h = x * _row_rms(x, 1.0 / D_MODEL) * gmix_ref[...]
        h = h * (1.0 + scale) + shift
        z = _dot(h.astype(BF16), w_in_ref[...])

        cos_t = cos_ref[rows, :]
        sin_a = sina_ref[rows, :]
        sin_b = sinb_ref[rows, :]

        c_q = z[:, 0:Q_LORA]
        cqn = c_q * _row_rms(c_q, 1.0 / Q_LORA) * gql_ref[...]
        q_raw = _dot(cqn.astype(BF16), wq_ref[...])
        r_full = _group_rms(q_raw, gq_ref, e2q_ref, icq_ref)
        q_gain = gq_row_ref[...] * (QK_DIM_A ** -0.5 * LOG2E)
        for hd in range(N_HEADS_A):
            hb = slice(hd * HEAD_BLOCK, (hd + 1) * HEAD_BLOCK)
            qa_ref[0, rows, hb] = _rope_block(q_raw[:, hb] * r_full[:, hb] * q_gain, cos_t, sin_a, sin_b).astype(BF16)

        c_kv = z[:, OFF_CKV:OFF_CKV + KV_LORA]
        lat_ref[0, rows, :] = c_kv * _row_rms(c_kv, 1.0 / KV_LORA) * gkv_ref[...]

        kp = z[:, OFF_KPE:OFF_KPE + LANES]
        kr = _rope_block(kp * _row_rms(kp, 1.0 / ROPE_DIM) * gkr_row_ref[...], cos_t, sin_a, sin_b)
        kr_ref[0, rows, :] = pltpu.roll(kr, LANES - ROPE_LANE0, 1)[:, 0:ROPE_DIM]
        krt_ref[0, :, rows] = kr.T[ROPE_LANE0:ROPE_LANE0 + ROPE_DIM, :]

        zq = z[:, OFF_QB:OFF_QB + B_COLS]
        qb_ref[0, rows, :] = (zq * _group_rms(zq, gb_ref, e2b_ref, icb_ref)
                              * (gqb_row_ref[...] * (HEAD_DIM_B ** -0.5 * LOG2E))).astype(BF16)
        zk = z[:, OFF_KB:OFF_KB + B_COLS]
        k_b = zk * _group_rms(zk, gb_ref, e2b_ref, icb_ref) * gkb_row_ref[...]
        v_b = z[:, OFF_VB:OFF_VB + B_COLS]
        kb_ref[0, rows, :] = k_b.astype(BF16)
        vb_ref[0, rows, :] = v_b.astype(BF16)

        if si == n_sub - 1:
            @pl.when(pl.program_id(1) == pl.num_programs(1) - 1)
            def _():
                kbt_ref[0] = k_b.T if tail_feat_major else k_b
                vbt_ref[0] = v_b.T if tail_feat_major else v_b


def _mixer_in_call(x, ada, tables, consts, weights, tm, sub, tail_feat_major):
    nb, sb, _ = x.shape
    nj = sb // tm
    keep = sub
    assert tm % sub == 0 and sb % tm == 0
    ada_rows = ada.shape[1]
    if ada_rows == 1:
        ada_spec = pl.BlockSpec((1, 1, 6 * D_MODEL), lambda b, j: (b, 0, 0))
    else:
        ada_spec = pl.BlockSpec((1, tm, 6 * D_MODEL), lambda b, j: (b, j, 0))
    tab_spec = pl.BlockSpec((tm, LANES), lambda b, j: (j, 0))
    tok = lambda c: pl.BlockSpec((1, tm, c), lambda b, j: (b, j, 0))
    if tail_feat_major:
        tail = pl.BlockSpec((1, B_COLS, keep), lambda b, j: (b, 0, 0))
        tail_shape = jax.ShapeDtypeStruct((nb, B_COLS, keep), F32)
    else:
        tail = pl.BlockSpec((1, keep, B_COLS), lambda b, j: (b, 0, 0))
        tail_shape = jax.ShapeDtypeStruct((nb, keep, B_COLS), F32)
    const_in = [weights["g_mix"], weights["w_in_mix"], weights["g_q_lora"], weights["w_q_up"], weights["g_kv_lora"],
                weights["gq_row"], weights["gkr_row"], weights["gqb_row"], weights["gkb_row"],
                consts["g_q"], consts["e2_q"], consts["ic_q"], consts["g_b"], consts["e2_b"], consts["ic_b"]]
    out_shape = [jax.ShapeDtypeStruct((nb, sb, QA_COLS), BF16),
                 jax.ShapeDtypeStruct((nb, sb, KV_LORA), F32),
                 jax.ShapeDtypeStruct((nb, sb, ROPE_DIM), F32),
                 jax.ShapeDtypeStruct((nb, ROPE_DIM, sb), F32),
                 jax.ShapeDtypeStruct((nb, sb, B_COLS), BF16),
                 jax.ShapeDtypeStruct((nb, sb, B_COLS), BF16),
                 jax.ShapeDtypeStruct((nb, sb, B_COLS), BF16),
                 tail_shape, tail_shape]
    rope_t = pl.BlockSpec((1, ROPE_DIM, tm), lambda b, j: (b, 0, j))
    return pl.pallas_call(
        functools.partial(_mixer_in_kernel, sub, tail_feat_major),
        grid=(nb, nj),
        in_specs=[tok(D_MODEL), ada_spec, tab_spec, tab_spec, tab_spec] + [_const_spec(a.shape) for a in const_in],
        out_specs=[tok(QA_COLS), tok(KV_LORA), tok(ROPE_DIM), rope_t, tok(B_COLS), tok(B_COLS), tok(B_COLS), tail, tail],
        out_shape=out_shape,
        compiler_params=_cparams(2),
        name="mixer_in",
    )(x, ada, tables[0], tables[1], tables[2], *const_in)


def _expand_kv(lat, krt, wkv, gk_row, store):
    tm = lat.shape[0]
    kv = _dot(lat.astype(BF16), wkv)
    kr_tok = jnp.concatenate([krt, jnp.zeros((LANES - ROPE_DIM, tm), F32)], axis=0).T
    slot = pltpu.roll(kr_tok, ROPE_LANE0, 1)
    nope = lax.broadcasted_iota(jnp.int32, (tm, LANES), 1) < NOPE_DIM
    for hd in range(wkv.shape[1] // HEAD_BLOCK):
        blk = kv[:, hd * HEAD_BLOCK:(hd + 1) * HEAD_BLOCK]
        ssq = jnp.sum(jnp.where(nope, blk * blk, 0.0), axis=-1, keepdims=True)
        r = lax.rsqrt(ssq * (1.0 / NOPE_DIM) + EPS)
        store(hd, jnp.where(nope, blk * r * gk_row, slot).astype(BF16), jnp.where(nope, 1.0, blk).astype(BF16))


def _kv_expand_kernel(lat_ref, krt_ref, wkv_ref, gk_row_ref, k_ref, v_ref):
    def store(hd, k_blk, v_blk):
        k_ref[0, :, hd * HEAD_BLOCK:(hd + 1) * HEAD_BLOCK] = k_blk
        v_ref[0, :, hd * HEAD_BLOCK:(hd + 1) * HEAD_BLOCK] = v_blk

    _expand_kv(lat_ref[0], krt_ref[0], wkv_ref[...], gk_row_ref[...], store)


def _kv_expand_call(latent, k_rope_t, weights, tm):
    nb, s, _ = latent.shape
    const_in = [weights["w_kv_up"], weights["gk_row"]]
    out = pl.BlockSpec((1, tm, QA_COLS), lambda b, j: (b, j, 0))
    return pl.pallas_call(
        _kv_expand_kernel,
        grid=(nb, s // tm),
        in_specs=[pl.BlockSpec((1, tm, KV_LORA), lambda b, j: (b, j, 0)),
                  pl.BlockSpec((1, ROPE_DIM, tm), lambda b, j: (b, 0, j))] + [_const_spec(a.shape) for a in const_in],
        out_specs=[out, out],
        out_shape=[jax.ShapeDtypeStruct((nb, s, QA_COLS), BF16), jax.ShapeDtypeStruct((nb, s, QA_COLS), BF16)],
        compiler_params=_cparams(2),
        name="kv_expand",
    )(latent, k_rope_t, *const_in)


def _normalize_heads(acc0, acc1, lane):
    o0 = acc0 / pltpu.roll(acc0, V_DIM_A, 1)
    o1 = acc1 / pltpu.roll(acc1, V_DIM_A, 1)
    return jnp.where(lane < V_DIM_A, pltpu.roll(o0, V_DIM_A, 1), o1)


def _mla_prompt_kernel(tq, q_ref, lat_ref, krt_ref, wkv_ref, gk_row_ref, o_ref, k_ref, v_ref):
    seq = q_ref.shape[1]
    n_heads = q_ref.shape[2] // HEAD_BLOCK
    lane = lax.broadcasted_iota(jnp.int32, (tq, LANES), 1)
    row_c = lax.broadcasted_iota(jnp.int32, (tq, tq), 0) // CHUNK
    col_c = lax.broadcasted_iota(jnp.int32, (tq, tq), 1) // CHUNK
    diag_ok = row_c >= col_c
    for qi in range(seq // tq):
        r0 = qi * tq

        def store(hd, k_blk, v_blk):
            k_ref[0, r0:r0 + tq, hd * HEAD_BLOCK:(hd + 1) * HEAD_BLOCK] = k_blk
            v_ref[0, r0:r0 + tq, hd * HEAD_BLOCK:(hd + 1) * HEAD_BLOCK] = v_blk

        _expand_kv(lat_ref[0, r0:r0 + tq, :], krt_ref[0, :, r0:r0 + tq], wkv_ref[...], gk_row_ref[...], store)
        accs = []
        for hd in range(n_heads):
            hb = slice(hd * HEAD_BLOCK, (hd + 1) * HEAD_BLOCK)
            q = q_ref[0, r0:r0 + tq, hb]
            s_d = jnp.where(diag_ok, _dot_nt(q, k_ref[0, r0:r0 + tq, hb]), NEG_INF)
            m = jnp.max(s_d, axis=-1, keepdims=True)
            if qi > 0:
                s_f = _dot_nt(q, k_ref[0, 0:r0, hb])
                m = jnp.maximum(m, jnp.max(s_f, axis=-1, keepdims=True))
            acc = _dot(jnp.exp2(s_d - m).astype(BF16), v_ref[0, r0:r0 + tq, hb])
            if qi > 0:
                acc = acc + _dot(jnp.exp2(s_f - m).astype(BF16), v_ref[0, 0:r0, hb])
            accs.append(acc)
        for pr in range(n_heads // 2):
            o_ref[0, r0:r0 + tq, pr * LANES:(pr + 1) * LANES] = _normalize_heads(
                accs[2 * pr], accs[2 * pr + 1], lane).astype(BF16)


def _mla_prompt_call(q_a, latent, k_rope_t, weights, tq, heads_per_step):
    nb, seq, _ = q_a.shape
    n_groups = N_HEADS_A // heads_per_step
    cols = heads_per_step * HEAD_BLOCK
    return pl.pallas_call(
        functools.partial(_mla_prompt_kernel, tq),
        grid=(nb, n_groups),
        in_specs=[pl.BlockSpec((1, seq, cols), lambda b, g: (b, 0, g)),
                  pl.BlockSpec((1, seq, KV_LORA), lambda b, g: (b, 0, 0)),
                  pl.BlockSpec((1, ROPE_DIM, seq), lambda b, g: (b, 0, 0)),
                  pl.BlockSpec((KV_LORA, cols), lambda b, g: (0, g)),
                  _const_spec(weights["gk_row"].shape)],
        out_specs=pl.BlockSpec((1, seq, heads_per_step * V_DIM_A), lambda b, g: (b, 0, g)),
        out_shape=jax.ShapeDtypeStruct((nb, seq, VA_COLS), BF16),
        scratch_shapes=[pltpu.VMEM((1, seq, cols), BF16), pltpu.VMEM((1, seq, cols), BF16)],
        compiler_params=_cparams(2),
        name="mla_prompt",
    )(q_a, latent, k_rope_t, weights["w_kv_up"], weights["gk_row"])


def _mla_sample_kernel(q_ref, kc_ref, vc_ref, kn_ref, vn_ref, o_ref):
    rows = q_ref.shape[1]
    lane = lax.broadcasted_iota(jnp.int32, (rows, LANES), 1)
    accs = []
    for hd in range(2):
        lo = hd * HEAD_BLOCK
        q = q_ref[0, :, lo:lo + HEAD_BLOCK]
        s_c = _dot_nt(q, kc_ref[0, :, lo:lo + HEAD_BLOCK])
        s_n = jnp.where(lane < rows, _dot_nt(q, _pad_rows(kn_ref[0, :, lo:lo + HEAD_BLOCK], LANES)), NEG_INF)
        m = jnp.maximum(jnp.max(s_c, axis=-1, keepdims=True), jnp.max(s_n, axis=-1, keepdims=True))
        accs.append(_dot(jnp.exp2(s_c - m).astype(BF16), vc_ref[0, :, lo:lo + HEAD_BLOCK])
                    + _dot(jnp.exp2(s_n - m).astype(BF16), _pad_rows(vn_ref[0, :, lo:lo + HEAD_BLOCK], LANES)))
    o_ref[0] = _normalize_heads(accs[0], accs[1], lane).astype(BF16)


def _mla_sample_call(q_a, k_cache, v_cache, k_new, v_new):
    nb, rows, _ = q_a.shape
    past = k_cache.shape[1]
    n_pairs = N_HEADS_A // 2
    kspec = lambda s: pl.BlockSpec((1, s, 2 * HEAD_BLOCK), lambda b, p: (b, 0, p))
    vspec = lambda s: pl.BlockSpec((1, s, 2 * V_DIM_A), lambda b, p: (b, 0, p))
    return pl.pallas_call(
        _mla_sample_kernel,
        grid=(nb, n_pairs),
        in_specs=[kspec(rows), kspec(past), kspec(past), kspec(rows), kspec(rows)],
        out_specs=vspec(rows),
        out_shape=jax.ShapeDtypeStruct((nb, rows, VA_COLS), BF16),
        compiler_params=_cparams(2),
        name="mla_sample",
    )(q_a, k_cache, v_cache, k_new, v_new)


def _toeplitz_bias(g0, rows):
    far = g0[:, 0:1]
    x0 = jnp.broadcast_to(g0[:, 0:LANES], (rows, LANES))
    x1 = jnp.broadcast_to(g0[:, LANES:2 * LANES], (rows, LANES))
    row = lax.broadcasted_iota(jnp.int32, (rows, LANES), 0)
    lane = lax.broadcasted_iota(jnp.int32, (rows, LANES), 1)
    step = 1
    while step < rows:
        r0 = pltpu.roll(x0, step, 1)
        r1 = pltpu.roll(x1, step, 1)
        keep = lane >= step
        take = (row & step) != 0
        x0, x1 = jnp.where(take, jnp.where(keep, r0, r1), x0), jnp.where(take, jnp.where(keep, r1, r0), x1)
        step *= 2
    return jnp.where(lane < row, far, x0), x1, far


def _band_bias_kernel(rb_ref, bias_ref):
    hd = pl.program_id(0)
    tw0, tw1, far = _toeplitz_bias(rb_ref[pl.ds(hd, 1), :], LANES)
    far_blk = jnp.broadcast_to(far, (LANES, LANES))
    n_blk = BAND_WIN // LANES
    row_c = lax.broadcasted_iota(jnp.int32, (LANES, LANES), 0) // CHUNK
    lane = lax.broadcasted_iota(jnp.int32, (LANES, LANES), 1)
    for half in range(BAND_TQ // LANES):
        first_tw = BAND_WINDOW // LANES - 1 + half
        for cb in range(n_blk):
            blk = tw0 if cb == first_tw else (tw1 if cb == first_tw + 1 else far_blk)
            q_c = row_c + half * (LANES // CHUNK)
            col_c = (cb * LANES + lane) // CHUNK
            ok = (col_c >= q_c) & (col_c <= q_c + LEFT_CHUNKS)
            bias_ref[0, half * LANES:(half + 1) * LANES, cb * LANES:(cb + 1) * LANES] = jnp.where(ok, blk * LOG2E, NEG_INF)


def _band_bias_call(rb_rev):
    return pl.pallas_call(
        _band_bias_kernel,
        grid=(N_HEADS_B,),
        in_specs=[_const_spec(rb_rev.shape)],
        out_specs=pl.BlockSpec((1, BAND_TQ, BAND_WIN), lambda h: (h, 0, 0)),
        out_shape=jax.ShapeDtypeStruct((N_HEADS_B, BAND_TQ, BAND_WIN), F32),
        compiler_params=_cparams(1),
        name="band_bias",
    )(rb_rev)


def _split_heads(q, lane):
    zero = jnp.zeros_like(q)
    return jnp.concatenate([jnp.where(lane < HEAD_DIM_B, q, zero), jnp.where(lane >= HEAD_DIM_B, q, zero)], axis=0)


def _band_prompt_kernel(q_ref, k_ref, v_ref, bias_ref, o_ref, vext_ref):
    seq = q_ref.shape[1]
    n_pairs = q_ref.shape[2] // LANES
    lane_q = lax.broadcasted_iota(jnp.int32, (BAND_TQ, LANES), 1)
    for pr in range(n_pairs):
        vext_ref[pr, :, 0:LANES] = v_ref[0, :, pr * LANES:(pr + 1) * LANES]
        vext_ref[pr, :, LANES:2 * LANES] = jnp.ones((seq, LANES), BF16)
    for t in reversed(range(seq // BAND_TQ)):
        t0 = t * BAND_TQ
        k_lo = max(t0 - BAND_WINDOW, 0)
        w = t0 + BAND_TQ - k_lo
        for pr in range(n_pairs):
            cols = slice(pr * LANES, (pr + 1) * LANES)
            q2 = _split_heads(q_ref[0, t0:t0 + BAND_TQ, cols], lane_q)
            bias2 = jnp.concatenate([bias_ref[2 * pr, :, BAND_WIN - w:BAND_WIN],
                                     bias_ref[2 * pr + 1, :, BAND_WIN - w:BAND_WIN]], axis=0)
            s = _dot_nt(q2, k_ref[0, k_lo:t0 + BAND_TQ, cols]) + bias2
            p = jnp.exp2(s - jnp.max(s, axis=-1, keepdims=True))
            acc = _dot(p.astype(BF16), vext_ref[pr, k_lo:t0 + BAND_TQ, :])
            o2 = acc[:, 0:LANES] / acc[:, LANES:2 * LANES]
            o_ref[0, t0:t0 + BAND_TQ, cols] = jnp.where(lane_q < HEAD_DIM_B, o2[0:BAND_TQ],
                                                        o2[BAND_TQ:2 * BAND_TQ]).astype(BF16)


def _band_prompt_call(q_b, k_b, v_b, bias, pairs_per_step):
    nb, seq, _ = q_b.shape
    n_groups = N_HEADS_B // (2 * pairs_per_step)
    spec = pl.BlockSpec((1, seq, pairs_per_step * LANES), lambda b, p: (b, 0, p))
    return pl.pallas_call(
        _band_prompt_kernel,
        grid=(nb, n_groups),
        in_specs=[spec, spec, spec, pl.BlockSpec((2 * pairs_per_step, BAND_TQ, BAND_WIN), lambda b, p: (p, 0, 0))],
        out_specs=spec,
        out_shape=jax.ShapeDtypeStruct((nb, seq, B_COLS), BF16),
        scratch_shapes=[pltpu.VMEM((pairs_per_step, seq, 2 * LANES), BF16)],
        compiler_params=_cparams(2),
        name="band_prompt",
    )(q_b, k_b, v_b, bias)


def _band_sample_kernel(q_ref, kct_ref, vct_ref, kn_ref, vn_ref, bias_ref, o_ref):
    rows = q_ref.shape[1]
    n_cache = kct_ref.shape[2]
    lane = lax.broadcasted_iota(jnp.int32, (rows, LANES), 1)
    lane2 = lax.broadcasted_iota(jnp.int32, (2 * rows, LANES), 1)
    for pair in range(N_HEADS_B // 2):
        cols = slice(pair * LANES, (pair + 1) * LANES)
        q2 = _split_heads(q_ref[0, :, cols], lane)
        kct = kct_ref[0, cols, :].astype(BF16)
        vct = vct_ref[0, cols, :].astype(BF16)
        kn = _pad_rows(kn_ref[0, :, cols], LANES)
        vn = _pad_rows(vn_ref[0, :, cols], LANES)
        bias_c = jnp.concatenate([bias_ref[2 * pair, :, 0:n_cache], bias_ref[2 * pair + 1, :, 0:n_cache]], axis=0)
        bias_n = jnp.concatenate([bias_ref[2 * pair, :, n_cache:n_cache + LANES],
                                  bias_ref[2 * pair + 1, :, n_cache:n_cache + LANES]], axis=0)
        s_c = _dot(q2, kct) + bias_c
        s_n = jnp.where(lane2 < rows, _dot_nt(q2, kn) + bias_n, NEG_INF)
        m = jnp.maximum(jnp.max(s_c, axis=-1, keepdims=True), jnp.max(s_n, axis=-1, keepdims=True))
        p_c = jnp.exp2(s_c - m)
        p_n = jnp.exp2(s_n - m)
        l = jnp.sum(p_c, axis=-1, keepdims=True) + jnp.sum(p_n, axis=-1, keepdims=True)
        o2 = (_dot_nt(p_c.astype(BF16), vct) + _dot(p_n.astype(BF16), vn)) / l
        o_ref[0, :, cols] = jnp.where(lane < HEAD_DIM_B, o2[0:rows], o2[rows:2 * rows]).astype(BF16)


def _band_sample_call(q_b, k_cache_t, v_cache_t, k_new, v_new, bias):
    nb, rows, _ = q_b.shape
    n_cache = k_cache_t.shape[2]
    tok = pl.BlockSpec((1, rows, B_COLS), lambda b: (b, 0, 0))
    cache = pl.BlockSpec((1, B_COLS, n_cache), lambda b: (b, 0, 0))
    return pl.pallas_call(
        _band_sample_kernel,
        grid=(nb,),
        in_specs=[tok, cache, cache, tok, tok, pl.BlockSpec((N_HEADS_B, rows, BAND_WIN), lambda b: (0, 0, 0))],
        out_specs=tok,
        out_shape=jax.ShapeDtypeStruct((nb, rows, B_COLS), BF16),
        compiler_params=_cparams(1),
        name="band_sample",
    )(q_b, k_cache_t, v_cache_t, k_new, v_new, bias)


def _tail_kernel(x_ref, oa_ref, ob_ref, ada_ref, gmix_ref, wg_ref, woa_ref, wob_ref, wout_ref, gffn_ref,
                 wgate_ref, wup_ref, wdown_ref, y_ref):
    x = x_ref[0]
    ada = lambda k: ada_ref[0, :, k * D_MODEL:(k + 1) * D_MODEL]
    h = x * _row_rms(x, 1.0 / D_MODEL) * gmix_ref[...]
    h = (h * (1.0 + ada(1)) + ada(0)).astype(BF16)
    gates = jax.nn.sigmoid(_dot(h, wg_ref[...]))
    y_a = _dot(oa_ref[0], woa_ref[...])
    y_b = _dot(ob_ref[0], wob_ref[...])
    mixed = gates[:, 0:D_MODEL] * y_a + gates[:, D_MODEL:2 * D_MODEL] * y_b
    x1 = x + ada(2) * _dot(mixed.astype(BF16), wout_ref[...])
    h2 = x1 * _row_rms(x1, 1.0 / D_MODEL) * gffn_ref[...]
    h2 = (h2 * (1.0 + ada(4)) + ada(3)).astype(BF16)
    acc = jnp.zeros_like(x1)
    for c in range(D_FF // FF_CHUNK):
        cols = slice(c * FF_CHUNK, (c + 1) * FF_CHUNK)
        g = _dot(h2, wgate_ref[:, cols])
        u = _dot(h2, wup_ref[:, cols])
        act = (g * jax.nn.sigmoid(g) * u).astype(BF16)
        acc = acc + _dot(act, wdown_ref[cols, :])
    y_ref[0] = x1 + ada(5) * acc


def _tail_call(x, o_a, o_b, ada, weights, tm):
    nb, sb, _ = x.shape
    if ada.shape[1] == 1:
        ada_spec = pl.BlockSpec((1, 1, 6 * D_MODEL), lambda b, j: (b, 0, 0))
    else:
        ada_spec = pl.BlockSpec((1, tm, 6 * D_MODEL), lambda b, j: (b, j, 0))
    tok = lambda c: pl.BlockSpec((1, tm, c), lambda b, j: (b, j, 0))
    const_in = [weights["g_mix"], weights["w_in_gate"], weights["w_o_a"], weights["w_o_b"], weights["w_out"],
                weights["g_ffn"], weights["w_gate"], weights["w_up"], weights["w_down"]]
    return pl.pallas_call(
        _tail_kernel,
        grid=(nb, sb // tm),
        in_specs=[tok(D_MODEL), tok(VA_COLS), tok(B_COLS), ada_spec] + [_const_spec(a.shape) for a in const_in],
        out_specs=tok(D_MODEL),
        out_shape=jax.ShapeDtypeStruct((nb, sb, D_MODEL), F32),
        compiler_params=_cparams(2),
        name="tail",
    )(x, o_a, o_b, ada, *const_in)


def _group_constants():
    def pack(g, inv_cnt):
        ic = np.ones((1, LANES), np.float32)
        ic[0, :len(inv_cnt)] = inv_cnt
        return jnp.asarray(g, BF16), jnp.asarray(np.concatenate([g.T, g.T], axis=0), BF16), jnp.asarray(ic)

    g_q = np.zeros((QA_COLS, LANES), np.float32)
    for hd in range(N_HEADS_A):
        g_q[hd * HEAD_BLOCK:hd * HEAD_BLOCK + NOPE_DIM, hd] = 1.0
        g_q[hd * HEAD_BLOCK + ROPE_LANE0:hd * HEAD_BLOCK + ROPE_LANE0 + ROPE_DIM, N_HEADS_A + hd] = 1.0
    g_b = np.zeros((B_COLS, LANES), np.float32)
    for hd in range(N_HEADS_B):
        g_b[hd * HEAD_DIM_B:(hd + 1) * HEAD_DIM_B, hd] = 1.0
    c = {}
    c["g_q"], c["e2_q"], c["ic_q"] = pack(g_q, [1.0 / NOPE_DIM] * N_HEADS_A + [1.0 / ROPE_DIM] * N_HEADS_A)
    c["g_b"], c["e2_b"], c["ic_b"] = pack(g_b, [1.0 / HEAD_DIM_B] * N_HEADS_B)
    return c


def _rope_tables(pos):
    inv_freq = ROPE_BASE ** (-jnp.arange(HALF_ROPE, dtype=F32) / HALF_ROPE)
    ang = pos.astype(F32)[:, None] * inv_freq[None, :]
    cos, sin = jnp.cos(ang), jnp.sin(ang)
    n = pos.shape[0]
    ones = jnp.ones((n, ROPE_LANE0), F32)
    zeros = jnp.zeros((n, ROPE_LANE0), F32)
    pad1 = jnp.ones((n, LANES - ROPE_LANE0 - ROPE_DIM), F32)
    pad0 = jnp.zeros((n, LANES - ROPE_LANE0 - ROPE_DIM), F32)
    z16 = jnp.zeros((n, HALF_ROPE), F32)
    cos_t = jnp.concatenate([ones, cos, cos, pad1], axis=1)
    sin_a = jnp.concatenate([zeros, -sin, z16, pad0], axis=1)
    sin_b = jnp.concatenate([zeros, z16, sin, pad0], axis=1)
    return cos_t, sin_a, sin_b


def _layer_weights(l, w_in, g_norm_mix, g_q_lora, w_q_up, g_kv_lora, w_kv_up, g_qn_a, g_kn_a, g_qr_a, g_kr_a,
                   g_q_b, g_k_b, w_o_a, w_o_b, w_out, g_norm_ffn, w_gate, w_up, w_down):
    wi = w_in[l]
    kpe_block = jnp.pad(wi[:, OFF_KPE:OFF_KPE + ROPE_DIM], ((0, 0), (ROPE_LANE0, LANES - ROPE_LANE0 - ROPE_DIM)))
    n_mix = OFF_KPE + ROPE_DIM + COL_QKV_B
    w = {}
    w["w_in_mix"] = jnp.concatenate([wi[:, 0:OFF_KPE], kpe_block, wi[:, OFF_KPE + ROPE_DIM:n_mix]], axis=1).astype(BF16)
    w["w_in_gate"] = wi[:, n_mix:n_mix + N_GATE_COLS].astype(BF16)
    wq3 = w_q_up[l].reshape(Q_LORA, N_HEADS_A, QK_DIM_A)
    w["w_q_up"] = jnp.pad(wq3, ((0, 0), (0, 0), (0, HEAD_BLOCK - QK_DIM_A))).reshape(Q_LORA, QA_COLS).astype(BF16)
    w["w_kv_up"] = w_kv_up[l].astype(BF16)
    zpad = jnp.zeros((HEAD_BLOCK - QK_DIM_A,), F32)
    w["gq_row"] = jnp.concatenate([g_qn_a[l], g_qr_a[l], zpad]).reshape(1, HEAD_BLOCK)
    w["gk_row"] = jnp.concatenate([g_kn_a[l], jnp.zeros((HEAD_BLOCK - NOPE_DIM,), F32)]).reshape(1, HEAD_BLOCK)
    w["gkr_row"] = jnp.concatenate([jnp.zeros((ROPE_LANE0,), F32), g_kr_a[l], zpad]).reshape(1, LANES)
    w["gqb_row"] = jnp.tile(g_q_b[l], N_HEADS_B).reshape(1, B_COLS)
    w["gkb_row"] = jnp.tile(g_k_b[l], N_HEADS_B).reshape(1, B_COLS)
    w["g_mix"] = g_norm_mix[l].reshape(1, D_MODEL)
    w["g_q_lora"] = g_q_lora[l].reshape(1, Q_LORA)
    w["g_kv_lora"] = g_kv_lora[l].reshape(1, KV_LORA)
    w["g_ffn"] = g_norm_ffn[l].reshape(1, D_MODEL)
    w["w_o_a"] = w_o_a[l].astype(BF16)
    w["w_o_b"] = w_o_b[l].astype(BF16)
    w["w_out"] = w_out[l].astype(BF16)
    w["w_gate"] = w_gate[l].astype(BF16)
    w["w_up"] = w_up[l].astype(BF16)
    w["w_down"] = w_down[l].astype(BF16)
    return w


def kernel(x_prompt, x_sample, c_prompt, c_sample, cache_kv_latent, cache_k_rope, cache_band_k, cache_band_v, w_ada, b_ada, g_norm_mix, w_in, g_q_lora, w_q_up, g_kv_lora, w_kv_up, g_qn_a, g_kn_a, g_qr_a, g_kr_a, g_q_b, g_k_b, rel_bias, w_o_a, w_o_b, w_out, g_norm_ffn, w_gate, w_up, w_down):
    depth = w_in.shape[0]
    nb, seq, _ = x_prompt.shape
    nbs, sd, _ = x_sample.shape
    past = cache_kv_latent.shape[2]
    n_buf = cache_band_k.shape[2]
    keep = min(BAND_WINDOW, seq)
    assert depth == 1 and nbs * sd == LANES and n_buf == BAND_WINDOW and seq % 512 == 0 and past % 512 == 0
    tm = 512
    rows_s = nbs * sd

    consts = _group_constants()
    tab_p = _rope_tables(jnp.arange(seq))
    tab_s = _rope_tables(past + (jnp.arange(rows_s) % sd))
    xs = x_sample.reshape(1, rows_s, D_MODEL)

    l = 0
    wts = _layer_weights(l, w_in, g_norm_mix, g_q_lora, w_q_up, g_kv_lora, w_kv_up, g_qn_a, g_kn_a, g_qr_a, g_kr_a,
                         g_q_b, g_k_b, w_o_a, w_o_b, w_out, g_norm_ffn, w_gate, w_up, w_down)
    band_bias = _band_bias_call(rel_bias[l][:, 2 * REL_CLIP:0:-1])

    ada = _ada_call(jnp.concatenate([c_prompt, c_sample], axis=0), w_ada[l], b_ada[l])
    ada_p = ada[:nb].reshape(nb, 1, 6 * D_MODEL)
    ada_s = jnp.repeat(ada[nb:], sd, axis=0).reshape(1, rows_s, 6 * D_MODEL)

    assert keep == tm
    qa, lat, _, krt, qb, kb, vb, kbt_tail, vbt_tail = _mixer_in_call(x_prompt, ada_p, tab_p, consts, wts,
                                                                      2 * tm, tm, True)
    o_a = _mla_prompt_call(qa, lat, krt, wts, 512, 4)
    o_b = _band_prompt_call(qb, kb, vb, band_bias, 2)
    y_p = _tail_call(x_prompt, o_a, o_b, ada_p, wts, tm)

    qa_s, lat_s, kr_s, krt_s, qb_s, kb_s, vb_s, kb_s32, vb_s32 = _mixer_in_call(xs, ada_s, tab_s, consts, wts,
                                                                                rows_s, rows_s, False)
    kc, vc = _kv_expand_call(cache_kv_latent[l], jnp.transpose(cache_k_rope[l], (0, 2, 1)), wts, tm)
    kn, vn = _kv_expand_call(lat_s, krt_s, wts, rows_s)
    o_a_s = _mla_sample_call(qa_s.reshape(nbs, sd, QA_COLS), kc, vc,
                             kn.reshape(nbs, sd, QA_COLS), vn.reshape(nbs, sd, QA_COLS))
    feat_major = lambda c: jnp.transpose(c, (0, 2, 3, 1)).reshape(nbs, B_COLS, n_buf)
    o_b_s = _band_sample_call(qb_s.reshape(nbs, sd, B_COLS), feat_major(cache_band_k[l]), feat_major(cache_band_v[l]),
                              kb_s.reshape(nbs, sd, B_COLS), vb_s.reshape(nbs, sd, B_COLS), band_bias)
    y_s = _tail_call(xs, o_a_s.reshape(1, rows_s, VA_COLS), o_b_s.reshape(1, rows_s, B_COLS), ada_s, wts, rows_s)

    tok_major = lambda t: jnp.transpose(t.reshape(nb, N_HEADS_B, HEAD_DIM_B, keep), (0, 3, 1, 2))[None]
    return (y_p, y_s.reshape(nbs, sd, D_MODEL),
            lat.reshape(1, nb, seq, KV_LORA), jnp.transpose(krt, (0, 2, 1))[None],
            tok_major(kbt_tail), tok_major(vbt_tail),
            lat_s.reshape(1, nbs, sd, KV_LORA), kr_s.reshape(1, nbs, sd, ROPE_DIM),
            kb_s32.reshape(1, nbs, sd, N_HEADS_B, HEAD_DIM_B), vb_s32.reshape(1, nbs, sd, N_HEADS_B, HEAD_DIM_B))
```

```python
import functools

import jax
import jax.numpy as jnp
import numpy as np
from jax import lax
from jax.experimental import pallas as pl
from jax.experimental.pallas import tpu as pltpu

D_MODEL = 1024
CHUNK = 64
EPS = 1e-6
NEG_INF = -1e30
N_HEADS_A = 8
NOPE_DIM = 64
ROPE_DIM = 32
HALF_ROPE = ROPE_DIM // 2
V_DIM_A = 64
QK_DIM_A = NOPE_DIM + ROPE_DIM
Q_LORA = 384
KV_LORA = 256
ROPE_BASE = 10000.0
N_HEADS_B = 8
HEAD_DIM_B = 64
LEFT_CHUNKS = 8
BAND_WINDOW = LEFT_CHUNKS * CHUNK
REL_CLIP = 128
D_FF = -(-(8 * D_MODEL) // (3 * 256)) * 256
COL_QKV_B = 3 * N_HEADS_B * HEAD_DIM_B
N_GATE_COLS = 2 * D_MODEL

LANES = 128
HEAD_BLOCK = LANES
ROPE_LANE0 = NOPE_DIM
QA_COLS = N_HEADS_A * HEAD_BLOCK
VA_COLS = N_HEADS_A * V_DIM_A
B_COLS = N_HEADS_B * HEAD_DIM_B
MIX_COLS = Q_LORA + KV_LORA + LANES + COL_QKV_B
OFF_CKV = Q_LORA
OFF_KPE = Q_LORA + KV_LORA
OFF_QB = OFF_KPE + LANES
OFF_KB = OFF_QB + B_COLS
OFF_VB = OFF_KB + B_COLS
FF_CHUNK = 256
LOG2E = 1.4426950408889634
BAND_TQ = 256
BAND_WIN = BAND_WINDOW + BAND_TQ
VMEM_LIMIT = 56 * 1024 * 1024

BF16 = jnp.bfloat16
F32 = jnp.float32


def _cparams(n_axes):
    return pltpu.CompilerParams(dimension_semantics=("arbitrary",) * n_axes, vmem_limit_bytes=VMEM_LIMIT)


def _const_spec(shape):
    nd = len(shape)
    return pl.BlockSpec(shape, lambda *_: (0,) * nd, pipeline_mode=pl.Buffered(1))


def _dot(a, b):
    return jnp.dot(a, b, preferred_element_type=F32)


def _dot_nt(a, b):
    return lax.dot_general(a, b, (((1,), (1,)), ((), ())), preferred_element_type=F32)


def _pad_rows(x, rows):
    return jnp.concatenate([x, jnp.zeros((rows - x.shape[0], x.shape[1]), x.dtype)], axis=0)


def _row_rms(x, inv_n):
    return lax.rsqrt(jnp.sum(x * x, axis=-1, keepdims=True) * inv_n + EPS)


def _group_rms(x, g_ref, e2_ref, invcnt_ref):
    s = _dot((x * x).astype(BF16), g_ref[...])
    r = lax.rsqrt(s * invcnt_ref[...] + EPS)
    r_hi = r.astype(BF16)
    r_lo = (r - r_hi.astype(F32)).astype(BF16)
    return _dot(jnp.concatenate([r_hi, r_lo], axis=1), e2_ref[...])


def _rope_block(x, cos_t, sin_a, sin_b):
    return x * cos_t + pltpu.roll(x, LANES - HALF_ROPE, 1) * sin_a + pltpu.roll(x, HALF_ROPE, 1) * sin_b


def _ada_kernel(c_ref, w_ref, b_ref, o_ref):
    c = c_ref[...]
    a = (c * jax.nn.sigmoid(c)).astype(BF16)
    o_ref[...] = _dot(a, w_ref[...].astype(BF16)) + b_ref[...]


def _ada_call(c_all, w_ada, b_ada):
    rows = c_all.shape[0]
    n_out = w_ada.shape[1]
    tn = D_MODEL
    return pl.pallas_call(
        _ada_kernel,
        grid=(n_out // tn,),
        in_specs=[pl.BlockSpec((rows, D_MODEL), lambda n: (0, 0)),
                  pl.BlockSpec((D_MODEL, tn), lambda n: (0, n)),
                  pl.BlockSpec((1, tn), lambda n: (0, n))],
        out_specs=pl.BlockSpec((rows, tn), lambda n: (0, n)),
        out_shape=jax.ShapeDtypeStruct((rows, n_out), F32),
        compiler_params=_cparams(1),
        name="ada",
    )(c_all, w_ada, b_ada.reshape(1, n_out))


def _mixer_in_kernel(sub, tail_feat_major,
                     x_ref, ada_ref, cos_ref, sina_ref, sinb_ref, gmix_ref, w_in_ref, gql_ref, wq_ref, gkv_ref,
                     gq_row_ref, gkr_row_ref, gqb_row_ref, gkb_row_ref,
                     gq_ref, e2q_ref, icq_ref, gb_ref, e2b_ref, icb_ref,
                     qa_ref, lat_ref, kr_ref, krt_ref, qb_ref, kb_ref, vb_ref, kbt_ref, vbt_ref):
    n_sub = x_ref.shape[1] // sub
    for si in range(n_sub):
        rows = slice(si * sub, (si + 1) * sub)
        x = x_ref[0, rows, :]
        ada_rows = rows if ada_ref.shape[1] > 1 else slice(None)
        shift = ada_ref[0, ada_rows, 0:D_MODEL]
        scale = ada_ref[0, ada_rows, D_MODEL:2 * D_MODEL]
        h = x * _row_rms(x, 1.0 / D_MODEL) * gmix_ref[...]
        h = h * (1.0 + scale) + shift
        z = _dot(h.astype(BF16), w_in_ref[...])

        cos_t = cos_ref[rows, :]
        sin_a = sina_ref[rows, :]
        sin_b = sinb_ref[rows, :]

        c_q = z[:, 0:Q_LORA]
        cqn = c_q * _row_rms(c_q, 1.0 / Q_LORA) * gql_ref[...]
        q_raw = _dot(cqn.astype(BF16), wq_ref[...])
        r_full = _group_rms(q_raw, gq_ref, e2q_ref, icq_ref)
        q_gain = gq_row_ref[...] * (QK_DIM_A ** -0.5 * LOG2E)
        for hd in range(N_HEADS_A):
            hb = slice(hd * HEAD_BLOCK, (hd + 1) * HEAD_BLOCK)
            qa_ref[0, rows, hb] = _rope_block(q_raw[:, hb] * r_full[:, hb] * q_gain, cos_t, sin_a, sin_b).astype(BF16)

        c_kv = z[:, OFF_CKV:OFF_CKV + KV_LORA]
        lat_ref[0, rows, :] = c_kv * _row_rms(c_kv, 1.0 / KV_LORA) * gkv_ref[...]

        kp = z[:, OFF_KPE:OFF_KPE + LANES]
        kr = _rope_block(kp * _row_rms(kp, 1.0 / ROPE_DIM) * gkr_row_ref[...], cos_t, sin_a, sin_b)
        kr_ref[0, rows, :] = pltpu.roll(kr, LANES - ROPE_LANE0, 1)[:, 0:ROPE_DIM]
        krt_ref[0, :, rows] = kr.T[ROPE_LANE0:ROPE_LANE0 + ROPE_DIM, :]

        zq = z[:, OFF_QB:OFF_QB + B_COLS]
        qb_ref[0, rows, :] = (zq * _group_rms(zq, gb_ref, e2b_ref, icb_ref)
                              * (gqb_row_ref[...] * (HEAD_DIM_B ** -0.5 * LOG2E))).astype(BF16)
        zk = z[:, OFF_KB:OFF_KB + B_COLS]
        k_b = zk * _group_rms(zk, gb_ref, e2b_ref, icb_ref) * gkb_row_ref[...]
        v_b = z[:, OFF_VB:OFF_VB + B_COLS]
        kb_ref[0, rows, :] = k_b.astype(BF16)
        vb_ref[0, rows, :] = v_b.astype(BF16)

        if si == n_sub - 1:
            @pl.when(pl.program_id(1) == pl.num_programs(1) - 1)
            def _():
                kbt_ref[0] = k_b.T if tail_feat_major else k_b
                vbt_ref[0] = v_b.T if tail_feat_major else v_b


def _mixer_in_call(x, ada, tables, consts, weights, tm, sub, tail_feat_major):
    nb, sb, _ = x.shape
    nj = sb // tm
    keep = sub
    assert tm % sub == 0 and sb % tm == 0
    ada_rows = ada.shape[1]
    if ada_rows == 1:
        ada_spec = pl.BlockSpec((1, 1, 6 * D_MODEL), lambda b, j: (b, 0, 0))
    else:
        ada_spec = pl.BlockSpec((1, tm, 6 * D_MODEL), lambda b, j: (b, j, 0))
    tab_spec = pl.BlockSpec((tm, LANES), lambda b, j: (j, 0))
    tok = lambda c: pl.BlockSpec((1, tm, c), lambda b, j: (b, j, 0))
    if tail_feat_major:
        tail = pl.BlockSpec((1, B_COLS, keep), lambda b, j: (b, 0, 0))
        tail_shape = jax.ShapeDtypeStruct((nb, B_COLS, keep), F32)
    else:
        tail = pl.BlockSpec((1, keep, B_COLS), lambda b, j: (b, 0, 0))
        tail_shape = jax.ShapeDtypeStruct((nb, keep, B_COLS), F32)
    const_in = [weights["g_mix"], weights["w_in_mix"], weights["g_q_lora"], weights["w_q_up"], weights["g_kv_lora"],
                weights["gq_row"], weights["gkr_row"], weights["gqb_row"], weights["gkb_row"],
                consts["g_q"], consts["e2_q"], consts["ic_q"], consts["g_b"], consts["e2_b"], consts["ic_b"]]
    out_shape = [jax.ShapeDtypeStruct((nb, sb, QA_COLS), BF16),
                 jax.ShapeDtypeStruct((nb, sb, KV_LORA), F32),
                 jax.ShapeDtypeStruct((nb, sb, ROPE_DIM), F32),
                 jax.ShapeDtypeStruct((nb, ROPE_DIM, sb), F32),
                 jax.ShapeDtypeStruct((nb, sb, B_COLS), BF16),
                 jax.ShapeDtypeStruct((nb, sb, B_COLS), BF16),
                 jax.ShapeDtypeStruct((nb, sb, B_COLS), BF16),
                 tail_shape, tail_shape]
    rope_t = pl.BlockSpec((1, ROPE_DIM, tm), lambda b, j: (b, 0, j))
    return pl.pallas_call(
        functools.partial(_mixer_in_kernel, sub, tail_feat_major),
        grid=(nb, nj),
        in_specs=[tok(D_MODEL), ada_spec, tab_spec, tab_spec, tab_spec] + [_const_spec(a.shape) for a in const_in],
        out_specs=[tok(QA_COLS), tok(KV_LORA), tok(ROPE_DIM), rope_t, tok(B_COLS), tok(B_COLS), tok(B_COLS), tail, tail],
        out_shape=out_shape,
        compiler_params=_cparams(2),
        name="mixer_in",
    )(x, ada, tables[0], tables[1], tables[2], *const_in)


def _expand_kv(lat, krt, wkv, gk_row, store):
    tm = lat.shape[0]
    kv = _dot(lat.astype(BF16), wkv)
    kr_tok = jnp.concatenate([krt, jnp.zeros((LANES - ROPE_DIM, tm), F32)], axis=0).T
    slot = pltpu.roll(kr_tok, ROPE_LANE0, 1)
    nope = lax.broadcasted_iota(jnp.int32, (tm, LANES), 1) < NOPE_DIM
    for hd in range(wkv.shape[1] // HEAD_BLOCK):
        blk = kv[:, hd * HEAD_BLOCK:(hd + 1) * HEAD_BLOCK]
        ssq = jnp.sum(jnp.where(nope, blk * blk, 0.0), axis=-1, keepdims=True)
        r = lax.rsqrt(ssq * (1.0 / NOPE_DIM) + EPS)
        store(hd, jnp.where(nope, blk * r * gk_row, slot).astype(BF16), jnp.where(nope, 1.0, blk).astype(BF16))


def _kv_expand_kernel(lat_ref, krt_ref, wkv_ref, gk_row_ref, k_ref, v_ref):
    def store(hd, k_blk, v_blk):
        k_ref[0, :, hd * HEAD_BLOCK:(hd + 1) * HEAD_BLOCK] = k_blk
        v_ref[0, :, hd * HEAD_BLOCK:(hd + 1) * HEAD_BLOCK] = v_blk

    _expand_kv(lat_ref[0], krt_ref[0], wkv_ref[...], gk_row_ref[...], store)


def _kv_expand_call(latent, k_rope_t, weights, tm):
    nb, s, _ = latent.shape
    const_in = [weights["w_kv_up"], weights["gk_row"]]
    out = pl.BlockSpec((1, tm, QA_COLS), lambda b, j: (b, j, 0))
    return pl.pallas_call(
        _kv_expand_kernel,
        grid=(nb, s // tm),
        in_specs=[pl.BlockSpec((1, tm, KV_LORA), lambda b, j: (b, j, 0)),
                  pl.BlockSpec((1, ROPE_DIM, tm), lambda b, j: (b, 0, j))] + [_const_spec(a.shape) for a in const_in],
        out_specs=[out, out],
        out_shape=[jax.ShapeDtypeStruct((nb, s, QA_COLS), BF16), jax.ShapeDtypeStruct((nb, s, QA_COLS), BF16)],
        compiler_params=_cparams(2),
        name="kv_expand",
    )(latent, k_rope_t, *const_in)


def _normalize_heads(acc0, acc1, lane):
    o0 = acc0 / pltpu.roll(acc0, V_DIM_A, 1)
    o1 = acc1 / pltpu.roll(acc1, V_DIM_A, 1)
    return jnp.where(lane < V_DIM_A, pltpu.roll(o0, V_DIM_A, 1), o1)


def _mla_prompt_kernel(tq, q_ref, lat_ref, krt_ref, wkv_ref, gk_row_ref, o_ref, k_ref, v_ref):
    seq = q_ref.shape[1]
    n_heads = q_ref.shape[2] // HEAD_BLOCK
    lane = lax.broadcasted_iota(jnp.int32, (tq, LANES), 1)
    row_c = lax.broadcasted_iota(jnp.int32, (tq, tq), 0) // CHUNK
    col_c = lax.broadcasted_iota(jnp.int32, (tq, tq), 1) // CHUNK
    diag_ok = row_c >= col_c
    def expand(ci):
        c0 = ci * tq

        def store(hd, k_blk, v_blk):
            k_ref[0, c0:c0 + tq, hd * HEAD_BLOCK:(hd + 1) * HEAD_BLOCK] = k_blk
            v_ref[0, c0:c0 + tq, hd * HEAD_BLOCK:(hd + 1) * HEAD_BLOCK] = v_blk

        _expand_kv(lat_ref[0, c0:c0 + tq, :], krt_ref[0, :, c0:c0 + tq], wkv_ref[...], gk_row_ref[...], store)

    n_q = seq // tq
    expand(0)
    for qi in range(n_q):
        r0 = qi * tq
        if qi + 1 < n_q:
            expand(qi + 1)
        accs = []
        for hd in range(n_heads):
            hb = slice(hd * HEAD_BLOCK, (hd + 1) * HEAD_BLOCK)
            q = q_ref[0, r0:r0 + tq, hb]
            s_d = jnp.where(diag_ok, _dot_nt(q, k_ref[0, r0:r0 + tq, hb]), NEG_INF)
            m = jnp.max(s_d, axis=-1, keepdims=True)
            if qi > 0:
                s_f = _dot_nt(q, k_ref[0, 0:r0, hb])
                m = jnp.maximum(m, jnp.max(s_f, axis=-1, keepdims=True))
            acc = _dot(jnp.exp2(s_d - m).astype(BF16), v_ref[0, r0:r0 + tq, hb])
            if qi > 0:
                acc = acc + _dot(jnp.exp2(s_f - m).astype(BF16), v_ref[0, 0:r0, hb])
            accs.append(acc)
        for pr in range(n_heads // 2):
            o_ref[0, r0:r0 + tq, pr * LANES:(pr + 1) * LANES] = _normalize_heads(
                accs[2 * pr], accs[2 * pr + 1], lane).astype(BF16)


def _mla_prompt_call(q_a, latent, k_rope_t, weights, tq, heads_per_step):
    nb, seq, _ = q_a.shape
    n_groups = N_HEADS_A // heads_per_step
    cols = heads_per_step * HEAD_BLOCK
    return pl.pallas_call(
        functools.partial(_mla_prompt_kernel, tq),
        grid=(nb, n_groups),
        in_specs=[pl.BlockSpec((1, seq, cols), lambda b, g: (b, 0, g)),
                  pl.BlockSpec((1, seq, KV_LORA), lambda b, g: (b, 0, 0)),
                  pl.BlockSpec((1, ROPE_DIM, seq), lambda b, g: (b, 0, 0)),
                  pl.BlockSpec((KV_LORA, cols), lambda b, g: (0, g)),
                  _const_spec(weights["gk_row"].shape)],
        out_specs=pl.BlockSpec((1, seq, heads_per_step * V_DIM_A), lambda b, g: (b, 0, g)),
        out_shape=jax.ShapeDtypeStruct((nb, seq, VA_COLS), BF16),
        scratch_shapes=[pltpu.VMEM((1, seq, cols), BF16), pltpu.VMEM((1, seq, cols), BF16)],
        compiler_params=_cparams(2),
        name="mla_prompt",
    )(q_a, latent, k_rope_t, weights["w_kv_up"], weights["gk_row"])


def _mla_sample_kernel(tc, q_ref, lat_ref, krt_ref, kn_ref, vn_ref, wkv_ref, gk_row_ref, o_ref, kc_ref, vc_ref):
    rows = q_ref.shape[1]
    past = lat_ref.shape[1]
    lane = lax.broadcasted_iota(jnp.int32, (rows, LANES), 1)
    for ci in range(past // tc):
        c0 = ci * tc

        def store(hd, k_blk, v_blk):
            kc_ref[c0:c0 + tc, hd * HEAD_BLOCK:(hd + 1) * HEAD_BLOCK] = k_blk
            vc_ref[c0:c0 + tc, hd * HEAD_BLOCK:(hd + 1) * HEAD_BLOCK] = v_blk

        _expand_kv(lat_ref[0, c0:c0 + tc, :], krt_ref[0, :, c0:c0 + tc], wkv_ref[...], gk_row_ref[...], store)
    accs = []
    for hd in range(N_HEADS_A):
        hb = slice(hd * HEAD_BLOCK, (hd + 1) * HEAD_BLOCK)
        q = q_ref[0, :, hb]
        s_c = _dot_nt(q, kc_ref[:, hb])
        s_n = jnp.where(lane < rows, _dot_nt(q, _pad_rows(kn_ref[0, :, hb], LANES)), NEG_INF)
        m = jnp.maximum(jnp.max(s_c, axis=-1, keepdims=True), jnp.max(s_n, axis=-1, keepdims=True))
        accs.append(_dot(jnp.exp2(s_c - m).astype(BF16), vc_ref[:, hb])
                    + _dot(jnp.exp2(s_n - m).astype(BF16), _pad_rows(vn_ref[0, :, hb], LANES)))
    for pr in range(N_HEADS_A // 2):
        o_ref[0, :, pr * LANES:(pr + 1) * LANES] = _normalize_heads(accs[2 * pr], accs[2 * pr + 1], lane).astype(BF16)


def _mla_sample_call(q_a, latent_cache, k_rope_cache_t, k_new, v_new, weights, tc):
    nb, rows, _ = q_a.shape
    past = latent_cache.shape[1]
    tok = pl.BlockSpec((1, rows, QA_COLS), lambda b: (b, 0, 0))
    return pl.pallas_call(
        functools.partial(_mla_sample_kernel, tc),
        grid=(nb,),
        in_specs=[tok, pl.BlockSpec((1, past, KV_LORA), lambda b: (b, 0, 0)),
                  pl.BlockSpec((1, ROPE_DIM, past), lambda b: (b, 0, 0)), tok, tok,
                  _const_spec(weights["w_kv_up"].shape), _const_spec(weights["gk_row"].shape)],
        out_specs=pl.BlockSpec((1, rows, VA_COLS), lambda b: (b, 0, 0)),
        out_shape=jax.ShapeDtypeStruct((nb, rows, VA_COLS), BF16),
        scratch_shapes=[pltpu.VMEM((past, QA_COLS), BF16), pltpu.VMEM((past, QA_COLS), BF16)],
        compiler_params=_cparams(1),
        name="mla_sample",
    )(q_a, latent_cache, k_rope_cache_t, k_new, v_new, weights["w_kv_up"], weights["gk_row"])


def _toeplitz_bias(g0, rows):
    far = g0[:, 0:1]
    x0 = jnp.broadcast_to(g0[:, 0:LANES], (rows, LANES))
    x1 = jnp.broadcast_to(g0[:, LANES:2 * LANES], (rows, LANES))
    row = lax.broadcasted_iota(jnp.int32, (rows, LANES), 0)
    lane = lax.broadcasted_iota(jnp.int32, (rows, LANES), 1)
    step = 1
    while step < rows:
        r0 = pltpu.roll(x0, step, 1)
        r1 = pltpu.roll(x1, step, 1)
        keep = lane >= step
        take = (row & step) != 0
        x0, x1 = jnp.where(take, jnp.where(keep, r0, r1), x0), jnp.where(take, jnp.where(keep, r1, r0), x1)
        step *= 2
    return jnp.where(lane < row, far, x0), x1, far


def _band_bias_kernel(rb_ref, bias_ref):
    hd = pl.program_id(0)
    tw0, tw1, far = _toeplitz_bias(rb_ref[pl.ds(hd, 1), :], LANES)
    far_blk = jnp.broadcast_to(far, (LANES, LANES))
    n_blk = BAND_WIN // LANES
    row_c = lax.broadcasted_iota(jnp.int32, (LANES, LANES), 0) // CHUNK
    lane = lax.broadcasted_iota(jnp.int32, (LANES, LANES), 1)
    for half in range(BAND_TQ // LANES):
        first_tw = BAND_WINDOW // LANES - 1 + half
        for cb in range(n_blk):
            blk = tw0 if cb == first_tw else (tw1 if cb == first_tw + 1 else far_blk)
            q_c = row_c + half * (LANES // CHUNK)
            col_c = (cb * LANES + lane) // CHUNK
            ok = (col_c >= q_c) & (col_c <= q_c + LEFT_CHUNKS)
            bias_ref[0, half * LANES:(half + 1) * LANES, cb * LANES:(cb + 1) * LANES] = jnp.where(ok, blk * LOG2E, NEG_INF)


def _band_bias_call(rb_rev):
    return pl.pallas_call(
        _band_bias_kernel,
        grid=(N_HEADS_B,),
        in_specs=[_const_spec(rb_rev.shape)],
        out_specs=pl.BlockSpec((1, BAND_TQ, BAND_WIN), lambda h: (h, 0, 0)),
        out_shape=jax.ShapeDtypeStruct((N_HEADS_B, BAND_TQ, BAND_WIN), F32),
        compiler_params=_cparams(1),
        name="band_bias",
    )(rb_rev)


def _split_heads(q, lane):
    zero = jnp.zeros_like(q)
    return jnp.concatenate([jnp.where(lane < HEAD_DIM_B, q, zero), jnp.where(lane >= HEAD_DIM_B, q, zero)], axis=0)


def _band_prompt_kernel(q_ref, k_ref, v_ref, bias_ref, o_ref, vext_ref):
    seq = q_ref.shape[1]
    n_pairs = q_ref.shape[2] // LANES
    lane_q = lax.broadcasted_iota(jnp.int32, (BAND_TQ, LANES), 1)
    for pr in range(n_pairs):
        vext_ref[pr, :, 0:LANES] = v_ref[0, :, pr * LANES:(pr + 1) * LANES]
        vext_ref[pr, :, LANES:2 * LANES] = jnp.ones((seq, LANES), BF16)
    for t in reversed(range(seq // BAND_TQ)):
        t0 = t * BAND_TQ
        k_lo = max(t0 - BAND_WINDOW, 0)
        w = t0 + BAND_TQ - k_lo
        for pr in range(n_pairs):
            cols = slice(pr * LANES, (pr + 1) * LANES)
            q2 = _split_heads(q_ref[0, t0:t0 + BAND_TQ, cols], lane_q)
            bias2 = jnp.concatenate([bias_ref[2 * pr, :, BAND_WIN - w:BAND_WIN],
                                     bias_ref[2 * pr + 1, :, BAND_WIN - w:BAND_WIN]], axis=0)
            s = _dot_nt(q2, k_ref[0, k_lo:t0 + BAND_TQ, cols]) + bias2
            p = jnp.exp2(s - jnp.max(s, axis=-1, keepdims=True))
            acc = _dot(p.astype(BF16), vext_ref[pr, k_lo:t0 + BAND_TQ, :])
            o2 = acc[:, 0:LANES] / acc[:, LANES:2 * LANES]
            o_ref[0, t0:t0 + BAND_TQ, cols] = jnp.where(lane_q < HEAD_DIM_B, o2[0:BAND_TQ],
                                                        o2[BAND_TQ:2 * BAND_TQ]).astype(BF16)


def _band_prompt_call(q_b, k_b, v_b, bias, pairs_per_step):
    nb, seq, _ = q_b.shape
    n_groups = N_HEADS_B // (2 * pairs_per_step)
    spec = pl.BlockSpec((1, seq, pairs_per_step * LANES), lambda b, p: (b, 0, p))
    return pl.pallas_call(
        _band_prompt_kernel,
        grid=(nb, n_groups),
        in_specs=[spec, spec, spec, pl.BlockSpec((2 * pairs_per_step, BAND_TQ, BAND_WIN), lambda b, p: (p, 0, 0))],
        out_specs=spec,
        out_shape=jax.ShapeDtypeStruct((nb, seq, B_COLS), BF16),
        scratch_shapes=[pltpu.VMEM((pairs_per_step, seq, 2 * LANES), BF16)],
        compiler_params=_cparams(2),
        name="band_prompt",
    )(q_b, k_b, v_b, bias)


def _band_sample_kernel(q_ref, kct_ref, vct_ref, kn_ref, vn_ref, bias_ref, o_ref):
    rows = q_ref.shape[1]
    n_cache = kct_ref.shape[2]
    lane = lax.broadcasted_iota(jnp.int32, (rows, LANES), 1)
    lane2 = lax.broadcasted_iota(jnp.int32, (2 * rows, LANES), 1)
    for pair in range(N_HEADS_B // 2):
        cols = slice(pair * LANES, (pair + 1) * LANES)
        q2 = _split_heads(q_ref[0, :, cols], lane)
        kct = kct_ref[0, cols, :].astype(BF16)
        vct = vct_ref[0, cols, :].astype(BF16)
        kn = _pad_rows(kn_ref[0, :, cols], LANES)
        vn = _pad_rows(vn_ref[0, :, cols], LANES)
        bias_c = jnp.concatenate([bias_ref[2 * pair, :, 0:n_cache], bias_ref[2 * pair + 1, :, 0:n_cache]], axis=0)
        bias_n = jnp.concatenate([bias_ref[2 * pair, :, n_cache:n_cache + LANES],
                                  bias_ref[2 * pair + 1, :, n_cache:n_cache + LANES]], axis=0)
        s_c = _dot(q2, kct) + bias_c
        s_n = jnp.where(lane2 < rows, _dot_nt(q2, kn) + bias_n, NEG_INF)
        m = jnp.maximum(jnp.max(s_c, axis=-1, keepdims=True), jnp.max(s_n, axis=-1, keepdims=True))
        p_c = jnp.exp2(s_c - m)
        p_n = jnp.exp2(s_n - m)
        l = jnp.sum(p_c, axis=-1, keepdims=True) + jnp.sum(p_n, axis=-1, keepdims=True)
        o2 = (_dot_nt(p_c.astype(BF16), vct) + _dot(p_n.astype(BF16), vn)) / l
        o_ref[0, :, cols] = jnp.where(lane < HEAD_DIM_B, o2[0:rows], o2[rows:2 * rows]).astype(BF16)


def _band_sample_call(q_b, k_cache_t, v_cache_t, k_new, v_new, bias):
    nb, rows, _ = q_b.shape
    n_cache = k_cache_t.shape[2]
    tok = pl.BlockSpec((1, rows, B_COLS), lambda b: (b, 0, 0))
    cache = pl.BlockSpec((1, B_COLS, n_cache), lambda b: (b, 0, 0))
    return pl.pallas_call(
        _band_sample_kernel,
        grid=(nb,),
        in_specs=[tok, cache, cache, tok, tok, pl.BlockSpec((N_HEADS_B, rows, BAND_WIN), lambda b: (0, 0, 0))],
        out_specs=tok,
        out_shape=jax.ShapeDtypeStruct((nb, rows, B_COLS), BF16),
        compiler_params=_cparams(1),
        name="band_sample",
    )(q_b, k_cache_t, v_cache_t, k_new, v_new, bias)


def _tail_rows(x_ref, oa_ref, ob_ref, ada_ref, y_ref, gmix_ref, wg_ref, woa_ref, wob_ref, wout_ref, gffn_ref,
               wgate_ref, wup_ref, wdown_ref):
    x = x_ref[0]
    ada = lambda k: ada_ref[0, :, k * D_MODEL:(k + 1) * D_MODEL]
    h = x * _row_rms(x, 1.0 / D_MODEL) * gmix_ref[...]
    h = (h * (1.0 + ada(1)) + ada(0)).astype(BF16)
    gates = jax.nn.sigmoid(_dot(h, wg_ref[...]))
    y_a = _dot(oa_ref[0], woa_ref[...])
    y_b = _dot(ob_ref[0], wob_ref[...])
    mixed = gates[:, 0:D_MODEL] * y_a + gates[:, D_MODEL:2 * D_MODEL] * y_b
    x1 = x + ada(2) * _dot(mixed.astype(BF16), wout_ref[...])
    h2 = x1 * _row_rms(x1, 1.0 / D_MODEL) * gffn_ref[...]
    h2 = (h2 * (1.0 + ada(4)) + ada(3)).astype(BF16)
    acc = jnp.zeros_like(x1)
    for c in range(D_FF // FF_CHUNK):
        cols = slice(c * FF_CHUNK, (c + 1) * FF_CHUNK)
        g = _dot(h2, wgate_ref[:, cols])
        u = _dot(h2, wup_ref[:, cols])
        act = (g * jax.nn.sigmoid(g) * u).astype(BF16)
        acc = acc + _dot(act, wdown_ref[cols, :])
    y_ref[0] = x1 + ada(5) * acc


def _tail_kernel(x_ref, oa_ref, ob_ref, ada_ref, xs_ref, oas_ref, obs_ref, adas_ref, *rest):
    weights, (y_ref, ys_ref) = rest[:-2], rest[-2:]
    _tail_rows(x_ref, oa_ref, ob_ref, ada_ref, y_ref, *weights)

    @pl.when((pl.program_id(0) == 0) & (pl.program_id(1) == 0))
    def _():
        _tail_rows(xs_ref, oas_ref, obs_ref, adas_ref, ys_ref, *weights)


def _tail_call(x, o_a, o_b, ada, xs, o_a_s, o_b_s, ada_s, weights, tm):
    nb, sb, _ = x.shape
    rows_s = xs.shape[1]
    tok = lambda c: pl.BlockSpec((1, tm, c), lambda b, j: (b, j, 0))
    whole = lambda c: pl.BlockSpec((1, rows_s, c), lambda b, j: (0, 0, 0))
    const_in = [weights["g_mix"], weights["w_in_gate"], weights["w_o_a"], weights["w_o_b"], weights["w_out"],
                weights["g_ffn"], weights["w_gate"], weights["w_up"], weights["w_down"]]
    return pl.pallas_call(
        _tail_kernel,
        grid=(nb, sb // tm),
        in_specs=[tok(D_MODEL), tok(VA_COLS), tok(B_COLS), pl.BlockSpec((1, 1, 6 * D_MODEL), lambda b, j: (b, 0, 0)),
                  whole(D_MODEL), whole(VA_COLS), whole(B_COLS), whole(6 * D_MODEL)]
        + [_const_spec(a.shape) for a in const_in],
        out_specs=[tok(D_MODEL), whole(D_MODEL)],
        out_shape=[jax.ShapeDtypeStruct((nb, sb, D_MODEL), F32), jax.ShapeDtypeStruct((1, rows_s, D_MODEL), F32)],
        compiler_params=_cparams(2),
        name="tail",
    )(x, o_a, o_b, ada, xs, o_a_s, o_b_s, ada_s, *const_in)


def _group_constants():
    def pack(g, inv_cnt):
        ic = np.ones((1, LANES), np.float32)
        ic[0, :len(inv_cnt)] = inv_cnt
        return jnp.asarray(g, BF16), jnp.asarray(np.concatenate([g.T, g.T], axis=0), BF16), jnp.asarray(ic)

    g_q = np.zeros((QA_COLS, LANES), np.float32)
    for hd in range(N_HEADS_A):
        g_q[hd * HEAD_BLOCK:hd * HEAD_BLOCK + NOPE_DIM, hd] = 1.0
        g_q[hd * HEAD_BLOCK + ROPE_LANE0:hd * HEAD_BLOCK + ROPE_LANE0 + ROPE_DIM, N_HEADS_A + hd] = 1.0
    g_b = np.zeros((B_COLS, LANES), np.float32)
    for hd in range(N_HEADS_B):
        g_b[hd * HEAD_DIM_B:(hd + 1) * HEAD_DIM_B, hd] = 1.0
    c = {}
    c["g_q"], c["e2_q"], c["ic_q"] = pack(g_q, [1.0 / NOPE_DIM] * N_HEADS_A + [1.0 / ROPE_DIM] * N_HEADS_A)
    c["g_b"], c["e2_b"], c["ic_b"] = pack(g_b, [1.0 / HEAD_DIM_B] * N_HEADS_B)
    return c


def _rope_tables(pos):
    inv_freq = ROPE_BASE ** (-jnp.arange(HALF_ROPE, dtype=F32) / HALF_ROPE)
    ang = pos.astype(F32)[:, None] * inv_freq[None, :]
    cos, sin = jnp.cos(ang), jnp.sin(ang)
    n = pos.shape[0]
    ones = jnp.ones((n, ROPE_LANE0), F32)
    zeros = jnp.zeros((n, ROPE_LANE0), F32)
    pad1 = jnp.ones((n, LANES - ROPE_LANE0 - ROPE_DIM), F32)
    pad0 = jnp.zeros((n, LANES - ROPE_LANE0 - ROPE_DIM), F32)
    z16 = jnp.zeros((n, HALF_ROPE), F32)
    cos_t = jnp.concatenate([ones, cos, cos, pad1], axis=1)
    sin_a = jnp.concatenate([zeros, -sin, z16, pad0], axis=1)
    sin_b = jnp.concatenate([zeros, z16, sin, pad0], axis=1)
    return cos_t, sin_a, sin_b


def _layer_weights(l, w_in, g_norm_mix, g_q_lora, w_q_up, g_kv_lora, w_kv_up, g_qn_a, g_kn_a, g_qr_a, g_kr_a,
                   g_q_b, g_k_b, w_o_a, w_o_b, w_out, g_norm_ffn, w_gate, w_up, w_down):
    wi = w_in[l]
    kpe_block = jnp.pad(wi[:, OFF_KPE:OFF_KPE + ROPE_DIM], ((0, 0), (ROPE_LANE0, LANES - ROPE_LANE0 - ROPE_DIM)))
    n_mix = OFF_KPE + ROPE_DIM + COL_QKV_B
    w = {}
    w["w_in_mix"] = jnp.concatenate([wi[:, 0:OFF_KPE].astype(BF16), kpe_block.astype(BF16),
                                     wi[:, OFF_KPE + ROPE_DIM:n_mix].astype(BF16)], axis=1)
    w["w_in_gate"] = wi[:, n_mix:n_mix + N_GATE_COLS].astype(BF16)
    wq3 = w_q_up[l].reshape(Q_LORA, N_HEADS_A, QK_DIM_A)
    w["w_q_up"] = jnp.pad(wq3, ((0, 0), (0, 0), (0, HEAD_BLOCK - QK_DIM_A))).reshape(Q_LORA, QA_COLS).astype(BF16)
    w["w_kv_up"] = w_kv_up[l].astype(BF16)
    zpad = jnp.zeros((HEAD_BLOCK - QK_DIM_A,), F32)
    w["gq_row"] = jnp.concatenate([g_qn_a[l], g_qr_a[l], zpad]).reshape(1, HEAD_BLOCK)
    w["gk_row"] = jnp.concatenate([g_kn_a[l], jnp.zeros((HEAD_BLOCK - NOPE_DIM,), F32)]).reshape(1, HEAD_BLOCK)
    w["gkr_row"] = jnp.concatenate([jnp.zeros((ROPE_LANE0,), F32), g_kr_a[l], zpad]).reshape(1, LANES)
    w["gqb_row"] = jnp.tile(g_q_b[l], N_HEADS_B).reshape(1, B_COLS)
    w["gkb_row"] = jnp.tile(g_k_b[l], N_HEADS_B).reshape(1, B_COLS)
    w["g_mix"] = g_norm_mix[l].reshape(1, D_MODEL)
    w["g_q_lora"] = g_q_lora[l].reshape(1, Q_LORA)
    w["g_kv_lora"] = g_kv_lora[l].reshape(1, KV_LORA)
    w["g_ffn"] = g_norm_ffn[l].reshape(1, D_MODEL)
    w["w_o_a"] = w_o_a[l].astype(BF16)
    w["w_o_b"] = w_o_b[l].astype(BF16)
    w["w_out"] = w_out[l].astype(BF16)
    w["w_gate"] = w_gate[l].astype(BF16)
    w["w_up"] = w_up[l].astype(BF16)
    w["w_down"] = w_down[l].astype(BF16)
    return w


def kernel(x_prompt, x_sample, c_prompt, c_sample, cache_kv_latent, cache_k_rope, cache_band_k, cache_band_v, w_ada, b_ada, g_norm_mix, w_in, g_q_lora, w_q_up, g_kv_lora, w_kv_up, g_qn_a, g_kn_a, g_qr_a, g_kr_a, g_q_b, g_k_b, rel_bias, w_o_a, w_o_b, w_out, g_norm_ffn, w_gate, w_up, w_down):
    depth = w_in.shape[0]
    nb, seq, _ = x_prompt.shape
    nbs, sd, _ = x_sample.shape
    past = cache_kv_latent.shape[2]
    n_buf = cache_band_k.shape[2]
    keep = min(BAND_WINDOW, seq)
    assert depth == 1 and nbs * sd == LANES and n_buf == BAND_WINDOW and seq % 512 == 0 and past % 512 == 0
    tm = 512
    rows_s = nbs * sd

    consts = _group_constants()
    tab_p = _rope_tables(jnp.arange(seq))
    tab_s = _rope_tables(past + (jnp.arange(rows_s) % sd))
    xs = x_sample.reshape(1, rows_s, D_MODEL)

    l = 0
    wts = _layer_weights(l, w_in, g_norm_mix, g_q_lora, w_q_up, g_kv_lora, w_kv_up, g_qn_a, g_kn_a, g_qr_a, g_kr_a,
                         g_q_b, g_k_b, w_o_a, w_o_b, w_out, g_norm_ffn, w_gate, w_up, w_down)
    band_bias = _band_bias_call(rel_bias[l][:, 2 * REL_CLIP:0:-1])

    ada = _ada_call(jnp.concatenate([c_prompt, c_sample], axis=0), w_ada[l], b_ada[l])
    ada_p = ada[:nb].reshape(nb, 1, 6 * D_MODEL)
    ada_s = jnp.repeat(ada[nb:], sd, axis=0).reshape(1, rows_s, 6 * D_MODEL)

    assert keep == tm
    qa, lat, _, krt, qb, kb, vb, kbt_tail, vbt_tail = _mixer_in_call(x_prompt, ada_p, tab_p, consts, wts,
                                                                      2 * tm, tm, True)
    o_a = _mla_prompt_call(qa, lat, krt, wts, 512, 4)
    o_b = _band_prompt_call(qb, kb, vb, band_bias, 2)

    qa_s, lat_s, kr_s, krt_s, qb_s, kb_s, vb_s, kb_s32, vb_s32 = _mixer_in_call(xs, ada_s, tab_s, consts, wts,
                                                                                rows_s, rows_s, False)
    kn, vn = _kv_expand_call(lat_s, krt_s, wts, rows_s)
    o_a_s = _mla_sample_call(qa_s.reshape(nbs, sd, QA_COLS), cache_kv_latent[l],
                             jnp.transpose(cache_k_rope[l], (0, 2, 1)),
                             kn.reshape(nbs, sd, QA_COLS), vn.reshape(nbs, sd, QA_COLS), wts, tm)
    feat_major = lambda c: jnp.transpose(c, (0, 2, 3, 1)).reshape(nbs, B_COLS, n_buf)
    o_b_s = _band_sample_call(qb_s.reshape(nbs, sd, B_COLS), feat_major(cache_band_k[l]), feat_major(cache_band_v[l]),
                              kb_s.reshape(nbs, sd, B_COLS), vb_s.reshape(nbs, sd, B_COLS), band_bias)

    y_p, y_s = _tail_call(x_prompt, o_a, o_b, ada_p, xs, o_a_s.reshape(1, rows_s, VA_COLS),
                          o_b_s.reshape(1, rows_s, B_COLS), ada_s, wts, tm)

    tok_major = lambda t: jnp.transpose(t.reshape(nb, N_HEADS_B, HEAD_DIM_B, keep), (0, 3, 1, 2))[None]
    return (y_p, y_s.reshape(nbs, sd, D_MODEL),
            lat.reshape(1, nb, seq, KV_LORA), jnp.transpose(krt, (0, 2, 1))[None],
            tok_major(kbt_tail), tok_major(vbt_tail),
            lat_s.reshape(1, nbs, sd, KV_LORA), kr_s.reshape(1, nbs, sd, ROPE_DIM),
            kb_s32.reshape(1, nbs, sd, N_HEADS_B, HEAD_DIM_B), vb_s32.reshape(1, nbs, sd, N_HEADS_B, HEAD_DIM_B))
```

```python
import functools

import jax
import jax.numpy as jnp
import numpy as np
from jax import lax
from jax.experimental import pallas as pl
from jax.experimental.pallas import tpu as pltpu

D_MODEL = 1024
CHUNK = 64
EPS = 1e-6
NEG_INF = -1e30
N_HEADS_A = 8
NOPE_DIM = 64
ROPE_DIM = 32
HALF_ROPE = ROPE_DIM // 2
V_DIM_A = 64
QK_DIM_A = NOPE_DIM + ROPE_DIM
Q_LORA = 384
KV_LORA = 256
ROPE_BASE = 10000.0
N_HEADS_B = 8
HEAD_DIM_B = 64
LEFT_CHUNKS = 8
BAND_WINDOW = LEFT_CHUNKS * CHUNK
REL_CLIP = 128
D_FF = -(-(8 * D_MODEL) // (3 * 256)) * 256
COL_QKV_B = 3 * N_HEADS_B * HEAD_DIM_B
N_GATE_COLS = 2 * D_MODEL

LANES = 128
HEAD_BLOCK = LANES
ROPE_LANE0 = NOPE_DIM
QA_COLS = N_HEADS_A * HEAD_BLOCK
VA_COLS = N_HEADS_A * V_DIM_A
B_COLS = N_HEADS_B * HEAD_DIM_B
MIX_COLS = Q_LORA + KV_LORA + LANES + COL_QKV_B
OFF_CKV = Q_LORA
OFF_KPE = Q_LORA + KV_LORA
OFF_QB = OFF_KPE + LANES
OFF_KB = OFF_QB + B_COLS
OFF_VB = OFF_KB + B_COLS
FF_CHUNK = 256
LOG2E = 1.4426950408889634
BAND_TQ = 256
BAND_WIN = BAND_WINDOW + BAND_TQ
VMEM_LIMIT = 60 * 1024 * 1024

BF16 = jnp.bfloat16
F32 = jnp.float32


def _cparams(n_axes):
    return pltpu.CompilerParams(dimension_semantics=("arbitrary",) * n_axes, vmem_limit_bytes=VMEM_LIMIT)


def _const_spec(shape):
    nd = len(shape)
    return pl.BlockSpec(shape, lambda *_: (0,) * nd, pipeline_mode=pl.Buffered(1))


def _dot(a, b):
    return jnp.dot(a, b, preferred_element_type=F32)


def _dot_nt(a, b):
    return lax.dot_general(a, b, (((1,), (1,)), ((), ())), preferred_element_type=F32)


def _pad_rows(x, rows):
    return jnp.concatenate([x, jnp.zeros((rows - x.shape[0], x.shape[1]), x.dtype)], axis=0)


def _row_rms(x, inv_n):
    return lax.rsqrt(jnp.sum(x * x, axis=-1, keepdims=True) * inv_n + EPS)


def _group_rms(x, g_ref, e2_ref, invcnt_ref):
    s = _dot((x * x).astype(BF16), g_ref[...])
    r = lax.rsqrt(s * invcnt_ref[...] + EPS)
    r_hi = r.astype(BF16)
    r_lo = (r - r_hi.astype(F32)).astype(BF16)
    return _dot(jnp.concatenate([r_hi, r_lo], axis=1), e2_ref[...])


def _rope_block(x, cos_t, sin_a, sin_b):
    return x * cos_t + pltpu.roll(x, LANES - HALF_ROPE, 1) * sin_a + pltpu.roll(x, HALF_ROPE, 1) * sin_b


def _ada_kernel(c_ref, w_ref, b_ref, o_ref):
    c = c_ref[...]
    a = (c * jax.nn.sigmoid(c)).astype(BF16)
    o_ref[...] = _dot(a, w_ref[...].astype(BF16)) + b_ref[...]


def _ada_call(c_all, w_ada, b_ada):
    rows = c_all.shape[0]
    n_out = w_ada.shape[1]
    tn = D_MODEL
    return pl.pallas_call(
        _ada_kernel,
        grid=(n_out // tn,),
        in_specs=[pl.BlockSpec((rows, D_MODEL), lambda n: (0, 0)),
                  pl.BlockSpec((D_MODEL, tn), lambda n: (0, n)),
                  pl.BlockSpec((1, tn), lambda n: (0, n))],
        out_specs=pl.BlockSpec((rows, tn), lambda n: (0, n)),
        out_shape=jax.ShapeDtypeStruct((rows, n_out), F32),
        compiler_params=_cparams(1),
        name="ada",
    )(c_all, w_ada, b_ada.reshape(1, n_out))


def _w_in_prep_kernel(w_ref, mix_ref, gate_ref):
    rows = w_ref.shape[0]
    n_mix = OFF_KPE + ROPE_DIM + COL_QKV_B
    mix_ref[:, 0:OFF_KPE] = w_ref[:, 0:OFF_KPE].astype(BF16)
    kpe = jnp.concatenate([jnp.zeros((rows, ROPE_LANE0), F32), w_ref[:, OFF_KPE:OFF_KPE + ROPE_DIM],
                           jnp.zeros((rows, LANES - ROPE_LANE0 - ROPE_DIM), F32)], axis=1)
    mix_ref[:, OFF_KPE:OFF_QB] = kpe.astype(BF16)
    mix_ref[:, OFF_QB:MIX_COLS] = w_ref[:, OFF_KPE + ROPE_DIM:n_mix].astype(BF16)
    gate_ref[...] = w_ref[:, n_mix:n_mix + N_GATE_COLS].astype(BF16)


def _w_in_prep_call(w_in_l, tr):
    d_in, n_cols = w_in_l.shape
    return pl.pallas_call(
        _w_in_prep_kernel,
        grid=(d_in // tr,),
        in_specs=[pl.BlockSpec((tr, n_cols), lambda i: (i, 0))],
        out_specs=[pl.BlockSpec((tr, MIX_COLS), lambda i: (i, 0)), pl.BlockSpec((tr, N_GATE_COLS), lambda i: (i, 0))],
        out_shape=[jax.ShapeDtypeStruct((d_in, MIX_COLS), BF16), jax.ShapeDtypeStruct((d_in, N_GATE_COLS), BF16)],
        compiler_params=_cparams(1),
        name="w_in_prep",
    )(w_in_l)


def _mixer_in_kernel(sub, tail_feat_major,
                     x_ref, ada_ref, cos_ref, sina_ref, sinb_ref, gmix_ref, w_in_ref, gql_ref, wq_ref, gkv_ref,
                     gq_row_ref, gkr_row_ref, gqb_row_ref, gkb_row_ref,
                     gq_ref, e2q_ref, icq_ref, gb_ref, e2b_ref, icb_ref, wkv_ref, gk_row_ref,
                     qa_ref, lat_ref, kr_ref, krt_ref, ka_ref, va_ref, qb_ref, kb_ref, vb_ref, kbt_ref, vbt_ref):
    n_sub = x_ref.shape[1] // sub
    for si in range(n_sub):
        rows = slice(si * sub, (si + 1) * sub)
        x = x_ref[0, rows, :]
        ada_rows = rows if ada_ref.shape[1] > 1 else slice(None)
        shift = ada_ref[0, ada_rows, 0:D_MODEL]
        scale = ada_ref[0, ada_rows, D_MODEL:2 * D_MODEL]
        h = x * _row_rms(x, 1.0 / D_MODEL) * gmix_ref[...]
        h = h * (1.0 + scale) + shift
        z = _dot(h.astype(BF16), w_in_ref[...])

        cos_t = cos_ref[rows, :]
        sin_a = sina_ref[rows, :]
        sin_b = sinb_ref[rows, :]

        c_q = z[:, 0:Q_LORA]
        cqn = c_q * _row_rms(c_q, 1.0 / Q_LORA) * gql_ref[...]
        q_raw = _dot(cqn.astype(BF16), wq_ref[...])
        r_full = _group_rms(q_raw, gq_ref, e2q_ref, icq_ref)
        q_gain = gq_row_ref[...] * (QK_DIM_A ** -0.5 * LOG2E)
        for hd in range(N_HEADS_A):
            hb = slice(hd * HEAD_BLOCK, (hd + 1) * HEAD_BLOCK)
            qa_ref[0, rows, hb] = _rope_block(q_raw[:, hb] * r_full[:, hb] * q_gain, cos_t, sin_a, sin_b).astype(BF16)

        c_kv = z[:, OFF_CKV:OFF_CKV + KV_LORA]
        lat = c_kv * _row_rms(c_kv, 1.0 / KV_LORA) * gkv_ref[...]
        lat_ref[0, rows, :] = lat

        kp = z[:, OFF_KPE:OFF_KPE + LANES]
        kr = _rope_block(kp * _row_rms(kp, 1.0 / ROPE_DIM) * gkr_row_ref[...], cos_t, sin_a, sin_b)
        kr_ref[0, rows, :] = pltpu.roll(kr, LANES - ROPE_LANE0, 1)[:, 0:ROPE_DIM]
        krt_ref[0, :, rows] = kr.T[ROPE_LANE0:ROPE_LANE0 + ROPE_DIM, :]

        def store(hd, k_blk, v_blk):
            ka_ref[0, rows, hd * HEAD_BLOCK:(hd + 1) * HEAD_BLOCK] = k_blk
            va_ref[0, rows, hd * HEAD_BLOCK:(hd + 1) * HEAD_BLOCK] = v_blk

        _expand_kv(lat, kr, wkv_ref[...], gk_row_ref[...], store)

        zq = z[:, OFF_QB:OFF_QB + B_COLS]
        qb_ref[0, rows, :] = (zq * _group_rms(zq, gb_ref, e2b_ref, icb_ref)
                              * (gqb_row_ref[...] * (HEAD_DIM_B ** -0.5 * LOG2E))).astype(BF16)
        zk = z[:, OFF_KB:OFF_KB + B_COLS]
        k_b = zk * _group_rms(zk, gb_ref, e2b_ref, icb_ref) * gkb_row_ref[...]
        v_b = z[:, OFF_VB:OFF_VB + B_COLS]
        kb_ref[0, rows, :] = k_b.astype(BF16)
        vb_ref[0, rows, :] = v_b.astype(BF16)

        if si == n_sub - 1:
            @pl.when(pl.program_id(1) == pl.num_programs(1) - 1)
            def _():
                kbt_ref[0] = k_b.T if tail_feat_major else k_b
                vbt_ref[0] = v_b.T if tail_feat_major else v_b


def _mixer_in_call(x, ada, tables, consts, weights, tm, sub, tail_feat_major):
    nb, sb, _ = x.shape
    nj = sb // tm
    keep = sub
    assert tm % sub == 0 and sb % tm == 0
    ada_rows = ada.shape[1]
    if ada_rows == 1:
        ada_spec = pl.BlockSpec((1, 1, 6 * D_MODEL), lambda b, j: (b, 0, 0))
    else:
        ada_spec = pl.BlockSpec((1, tm, 6 * D_MODEL), lambda b, j: (b, j, 0))
    tab_spec = pl.BlockSpec((tm, LANES), lambda b, j: (j, 0))
    tok = lambda c: pl.BlockSpec((1, tm, c), lambda b, j: (b, j, 0))
    if tail_feat_major:
        tail = pl.BlockSpec((1, B_COLS, keep), lambda b, j: (b, 0, 0))
        tail_shape = jax.ShapeDtypeStruct((nb, B_COLS, keep), F32)
    else:
        tail = pl.BlockSpec((1, keep, B_COLS), lambda b, j: (b, 0, 0))
        tail_shape = jax.ShapeDtypeStruct((nb, keep, B_COLS), F32)
    const_in = [weights["g_mix"], weights["w_in_mix"], weights["g_q_lora"], weights["w_q_up"], weights["g_kv_lora"],
                weights["gq_row"], weights["gkr_row"], weights["gqb_row"], weights["gkb_row"],
                consts["g_q"], consts["e2_q"], consts["ic_q"], consts["g_b"], consts["e2_b"], consts["ic_b"],
                weights["w_kv_up"], weights["gk_row"]]
    out_shape = [jax.ShapeDtypeStruct((nb, sb, QA_COLS), BF16),
                 jax.ShapeDtypeStruct((nb, sb, KV_LORA), F32),
                 jax.ShapeDtypeStruct((nb, sb, ROPE_DIM), F32),
                 jax.ShapeDtypeStruct((nb, ROPE_DIM, sb), F32),
                 jax.ShapeDtypeStruct((nb, sb, QA_COLS), BF16),
                 jax.ShapeDtypeStruct((nb, sb, QA_COLS), BF16),
                 jax.ShapeDtypeStruct((nb, sb, B_COLS), BF16),
                 jax.ShapeDtypeStruct((nb, sb, B_COLS), BF16),
                 jax.ShapeDtypeStruct((nb, sb, B_COLS), BF16),
                 tail_shape, tail_shape]
    rope_t = pl.BlockSpec((1, ROPE_DIM, tm), lambda b, j: (b, 0, j))
    return pl.pallas_call(
        functools.partial(_mixer_in_kernel, sub, tail_feat_major),
        grid=(nb, nj),
        in_specs=[tok(D_MODEL), ada_spec, tab_spec, tab_spec, tab_spec] + [_const_spec(a.shape) for a in const_in],
        out_specs=[tok(QA_COLS), tok(KV_LORA), tok(ROPE_DIM), rope_t, tok(QA_COLS), tok(QA_COLS),
                   tok(B_COLS), tok(B_COLS), tok(B_COLS), tail, tail],
        out_shape=out_shape,
        compiler_params=_cparams(2),
        name="mixer_in",
    )(x, ada, tables[0], tables[1], tables[2], *const_in)


def _expand_kv(lat, slot, wkv, gk_row, store):
    tm = lat.shape[0]
    kv = _dot(lat.astype(BF16), wkv)
    nope = lax.broadcasted_iota(jnp.int32, (tm, LANES), 1) < NOPE_DIM
    for hd in range(wkv.shape[1] // HEAD_BLOCK):
        blk = kv[:, hd * HEAD_BLOCK:(hd + 1) * HEAD_BLOCK]
        ssq = jnp.sum(jnp.where(nope, blk * blk, 0.0), axis=-1, keepdims=True)
        r = lax.rsqrt(ssq * (1.0 / NOPE_DIM) + EPS)
        store(hd, jnp.where(nope, blk * r * gk_row, slot).astype(BF16), jnp.where(nope, 1.0, blk).astype(BF16))


def _rope_slot_from_feature_major(krt):
    tm = krt.shape[1]
    kr_tok = jnp.concatenate([krt, jnp.zeros((LANES - ROPE_DIM, tm), F32)], axis=0).T
    return pltpu.roll(kr_tok, ROPE_LANE0, 1)


def _normalize_heads(acc0, acc1, lane):
    o0 = acc0 / pltpu.roll(acc0, V_DIM_A, 1)
    o1 = acc1 / pltpu.roll(acc1, V_DIM_A, 1)
    return jnp.where(lane < V_DIM_A, pltpu.roll(o0, V_DIM_A, 1), o1)


def _mla_prompt_kernel(tq, q_ref, k_ref, v_ref, o_ref):
    seq = q_ref.shape[1]
    n_heads = q_ref.shape[2] // HEAD_BLOCK
    lane = lax.broadcasted_iota(jnp.int32, (tq, LANES), 1)
    row_c = lax.broadcasted_iota(jnp.int32, (tq, tq), 0) // CHUNK
    col_c = lax.broadcasted_iota(jnp.int32, (tq, tq), 1) // CHUNK
    diag_ok = row_c >= col_c
    for qi in reversed(range(seq // tq)):
        r0 = qi * tq
        accs = []
        for hd in range(n_heads):
            hb = slice(hd * HEAD_BLOCK, (hd + 1) * HEAD_BLOCK)
            q = q_ref[0, r0:r0 + tq, hb]
            s_d = jnp.where(diag_ok, _dot_nt(q, k_ref[0, r0:r0 + tq, hb]), NEG_INF)
            m = jnp.max(s_d, axis=-1, keepdims=True)
            if qi > 0:
                s_f = _dot_nt(q, k_ref[0, 0:r0, hb])
                m = jnp.maximum(m, jnp.max(s_f, axis=-1, keepdims=True))
            acc = _dot(jnp.exp2(s_d - m).astype(BF16), v_ref[0, r0:r0 + tq, hb])
            if qi > 0:
                acc = acc + _dot(jnp.exp2(s_f - m).astype(BF16), v_ref[0, 0:r0, hb])
            accs.append(acc)
        for pr in range(n_heads // 2):
            o_ref[0, r0:r0 + tq, pr * LANES:(pr + 1) * LANES] = _normalize_heads(
                accs[2 * pr], accs[2 * pr + 1], lane).astype(BF16)


def _mla_prompt_call(q_a, k_a, v_a, tq, heads_per_step):
    nb, seq, _ = q_a.shape
    n_groups = N_HEADS_A // heads_per_step
    group = pl.BlockSpec((1, seq, heads_per_step * HEAD_BLOCK), lambda b, g: (b, 0, g))
    return pl.pallas_call(
        functools.partial(_mla_prompt_kernel, tq),
        grid=(nb, n_groups),
        in_specs=[group, group, group],
        out_specs=pl.BlockSpec((1, seq, heads_per_step * V_DIM_A), lambda b, g: (b, 0, g)),
        out_shape=jax.ShapeDtypeStruct((nb, seq, VA_COLS), BF16),
        compiler_params=_cparams(2),
        name="mla_prompt",
    )(q_a, k_a, v_a)


def _mla_sample_kernel(tc, q_ref, lat_ref, krt_ref, kn_ref, vn_ref, wkv_ref, gk_row_ref, o_ref, kc_ref, vc_ref):
    rows = q_ref.shape[1]
    past = lat_ref.shape[1]
    lane = lax.broadcasted_iota(jnp.int32, (rows, LANES), 1)
    for ci in range(past // tc):
        c0 = ci * tc

        def store(hd, k_blk, v_blk):
            kc_ref[c0:c0 + tc, hd * HEAD_BLOCK:(hd + 1) * HEAD_BLOCK] = k_blk
            vc_ref[c0:c0 + tc, hd * HEAD_BLOCK:(hd + 1) * HEAD_BLOCK] = v_blk

        _expand_kv(lat_ref[0, c0:c0 + tc, :], _rope_slot_from_feature_major(krt_ref[0, :, c0:c0 + tc]),
                   wkv_ref[...], gk_row_ref[...], store)
    accs = []
    for hd in range(N_HEADS_A):
        hb = slice(hd * HEAD_BLOCK, (hd + 1) * HEAD_BLOCK)
        q = q_ref[0, :, hb]
        s_c = _dot_nt(q, kc_ref[:, hb])
        s_n = jnp.where(lane < rows, _dot_nt(q, _pad_rows(kn_ref[0, :, hb], LANES)), NEG_INF)
        m = jnp.maximum(jnp.max(s_c, axis=-1, keepdims=True), jnp.max(s_n, axis=-1, keepdims=True))
        accs.append(_dot(jnp.exp2(s_c - m).astype(BF16), vc_ref[:, hb])
                    + _dot(jnp.exp2(s_n - m).astype(BF16), _pad_rows(vn_ref[0, :, hb], LANES)))
    for pr in range(N_HEADS_A // 2):
        o_ref[0, :, pr * LANES:(pr + 1) * LANES] = _normalize_heads(accs[2 * pr], accs[2 * pr + 1], lane).astype(BF16)


def _mla_sample_call(q_a, latent_cache, k_rope_cache_t, k_new, v_new, weights, tc):
    nb, rows, _ = q_a.shape
    past = latent_cache.shape[1]
    tok = pl.BlockSpec((1, rows, QA_COLS), lambda b: (b, 0, 0))
    return pl.pallas_call(
        functools.partial(_mla_sample_kernel, tc),
        grid=(nb,),
        in_specs=[tok, pl.BlockSpec((1, past, KV_LORA), lambda b: (b, 0, 0)),
                  pl.BlockSpec((1, ROPE_DIM, past), lambda b: (b, 0, 0)), tok, tok,
                  _const_spec(weights["w_kv_up"].shape), _const_spec(weights["gk_row"].shape)],
        out_specs=pl.BlockSpec((1, rows, VA_COLS), lambda b: (b, 0, 0)),
        out_shape=jax.ShapeDtypeStruct((nb, rows, VA_COLS), BF16),
        scratch_shapes=[pltpu.VMEM((past, QA_COLS), BF16), pltpu.VMEM((past, QA_COLS), BF16)],
        compiler_params=_cparams(1),
        name="mla_sample",
    )(q_a, latent_cache, k_rope_cache_t, k_new, v_new, weights["w_kv_up"], weights["gk_row"])


def _toeplitz_bias(g0, rows):
    far = g0[:, 0:1]
    x0 = jnp.broadcast_to(g0[:, 0:LANES], (rows, LANES))
    x1 = jnp.broadcast_to(g0[:, LANES:2 * LANES], (rows, LANES))
    row = lax.broadcasted_iota(jnp.int32, (rows, LANES), 0)
    lane = lax.broadcasted_iota(jnp.int32, (rows, LANES), 1)
    step = 1
    while step < rows:
        r0 = pltpu.roll(x0, step, 1)
        r1 = pltpu.roll(x1, step, 1)
        keep = lane >= step
        take = (row & step) != 0
        x0, x1 = jnp.where(take, jnp.where(keep, r0, r1), x0), jnp.where(take, jnp.where(keep, r1, r0), x1)
        step *= 2
    return jnp.where(lane < row, far, x0), x1, far


def _band_bias_kernel(rb_ref, bias_ref):
    hd = pl.program_id(0)
    tw0, tw1, far = _toeplitz_bias(rb_ref[pl.ds(hd, 1), :], LANES)
    far_blk = jnp.broadcast_to(far, (LANES, LANES))
    n_blk = BAND_WIN // LANES
    row_c = lax.broadcasted_iota(jnp.int32, (LANES, LANES), 0) // CHUNK
    lane = lax.broadcasted_iota(jnp.int32, (LANES, LANES), 1)
    for half in range(BAND_TQ // LANES):
        first_tw = BAND_WINDOW // LANES - 1 + half
        for cb in range(n_blk):
            blk = tw0 if cb == first_tw else (tw1 if cb == first_tw + 1 else far_blk)
            q_c = row_c + half * (LANES // CHUNK)
            col_c = (cb * LANES + lane) // CHUNK
            ok = (col_c >= q_c) & (col_c <= q_c + LEFT_CHUNKS)
            bias_ref[0, half * LANES:(half + 1) * LANES, cb * LANES:(cb + 1) * LANES] = jnp.where(ok, blk * LOG2E, NEG_INF)


def _band_bias_call(rb_rev):
    return pl.pallas_call(
        _band_bias_kernel,
        grid=(N_HEADS_B,),
        in_specs=[_const_spec(rb_rev.shape)],
        out_specs=pl.BlockSpec((1, BAND_TQ, BAND_WIN), lambda h: (h, 0, 0)),
        out_shape=jax.ShapeDtypeStruct((N_HEADS_B, BAND_TQ, BAND_WIN), F32),
        compiler_params=_cparams(1),
        name="band_bias",
    )(rb_rev)


def _split_heads(q, lane):
    zero = jnp.zeros_like(q)
    return jnp.concatenate([jnp.where(lane < HEAD_DIM_B, q, zero), jnp.where(lane >= HEAD_DIM_B, q, zero)], axis=0)


def _band_prompt_kernel(q_ref, k_ref, v_ref, bias_ref, o_ref, vext_ref):
    seq = q_ref.shape[1]
    n_pairs = q_ref.shape[2] // LANES
    lane_q = lax.broadcasted_iota(jnp.int32, (BAND_TQ, LANES), 1)
    for pr in range(n_pairs):
        vext_ref[pr, :, 0:LANES] = v_ref[0, :, pr * LANES:(pr + 1) * LANES]
        vext_ref[pr, :, LANES:2 * LANES] = jnp.ones((seq, LANES), BF16)
    for t in reversed(range(seq // BAND_TQ)):
        t0 = t * BAND_TQ
        k_lo = max(t0 - BAND_WINDOW, 0)
        w = t0 + BAND_TQ - k_lo
        for pr in range(n_pairs):
            cols = slice(pr * LANES, (pr + 1) * LANES)
            q2 = _split_heads(q_ref[0, t0:t0 + BAND_TQ, cols], lane_q)
            bias2 = jnp.concatenate([bias_ref[2 * pr, :, BAND_WIN - w:BAND_WIN],
                                     bias_ref[2 * pr + 1, :, BAND_WIN - w:BAND_WIN]], axis=0)
            s = _dot_nt(q2, k_ref[0, k_lo:t0 + BAND_TQ, cols]) + bias2
            p = jnp.exp2(s - jnp.max(s, axis=-1, keepdims=True))
            acc = _dot(p.astype(BF16), vext_ref[pr, k_lo:t0 + BAND_TQ, :])
            o2 = acc[:, 0:LANES] / acc[:, LANES:2 * LANES]
            o_ref[0, t0:t0 + BAND_TQ, cols] = jnp.where(lane_q < HEAD_DIM_B, o2[0:BAND_TQ],
                                                        o2[BAND_TQ:2 * BAND_TQ]).astype(BF16)


def _band_prompt_call(q_b, k_b, v_b, bias, pairs_per_step):
    nb, seq, _ = q_b.shape
    n_groups = N_HEADS_B // (2 * pairs_per_step)
    spec = pl.BlockSpec((1, seq, pairs_per_step * LANES), lambda b, p: (b, 0, p))
    return pl.pallas_call(
        _band_prompt_kernel,
        grid=(nb, n_groups),
        in_specs=[spec, spec, spec, pl.BlockSpec((2 * pairs_per_step, BAND_TQ, BAND_WIN), lambda b, p: (p, 0, 0),
                                                 pipeline_mode=pl.Buffered(1 if n_groups == 1 else 2))],
        out_specs=spec,
        out_shape=jax.ShapeDtypeStruct((nb, seq, B_COLS), BF16),
        scratch_shapes=[pltpu.VMEM((pairs_per_step, seq, 2 * LANES), BF16)],
        compiler_params=_cparams(2),
        name="band_prompt",
    )(q_b, k_b, v_b, bias)


def _band_sample_kernel(q_ref, kct_ref, vct_ref, kn_ref, vn_ref, bias_ref, o_ref):
    rows = q_ref.shape[1]
    n_cache = kct_ref.shape[2]
    lane = lax.broadcasted_iota(jnp.int32, (rows, LANES), 1)
    lane2 = lax.broadcasted_iota(jnp.int32, (2 * rows, LANES), 1)
    for pair in range(N_HEADS_B // 2):
        cols = slice(pair * LANES, (pair + 1) * LANES)
        q2 = _split_heads(q_ref[0, :, cols], lane)
        kct = kct_ref[0, cols, :].astype(BF16)
        vct = vct_ref[0, cols, :].astype(BF16)
        kn = _pad_rows(kn_ref[0, :, cols], LANES)
        vn = _pad_rows(vn_ref[0, :, cols], LANES)
        bias_c = jnp.concatenate([bias_ref[2 * pair, :, 0:n_cache], bias_ref[2 * pair + 1, :, 0:n_cache]], axis=0)
        bias_n = jnp.concatenate([bias_ref[2 * pair, :, n_cache:n_cache + LANES],
                                  bias_ref[2 * pair + 1, :, n_cache:n_cache + LANES]], axis=0)
        s_c = _dot(q2, kct) + bias_c
        s_n = jnp.where(lane2 < rows, _dot_nt(q2, kn) + bias_n, NEG_INF)
        m = jnp.maximum(jnp.max(s_c, axis=-1, keepdims=True), jnp.max(s_n, axis=-1, keepdims=True))
        p_c = jnp.exp2(s_c - m)
        p_n = jnp.exp2(s_n - m)
        l = jnp.sum(p_c, axis=-1, keepdims=True) + jnp.sum(p_n, axis=-1, keepdims=True)
        o2 = (_dot_nt(p_c.astype(BF16), vct) + _dot(p_n.astype(BF16), vn)) / l
        o_ref[0, :, cols] = jnp.where(lane < HEAD_DIM_B, o2[0:rows], o2[rows:2 * rows]).astype(BF16)


def _band_sample_call(q_b, k_cache_t, v_cache_t, k_new, v_new, bias):
    nb, rows, _ = q_b.shape
    n_cache = k_cache_t.shape[2]
    tok = pl.BlockSpec((1, rows, B_COLS), lambda b: (b, 0, 0))
    cache = pl.BlockSpec((1, B_COLS, n_cache), lambda b: (b, 0, 0))
    return pl.pallas_call(
        _band_sample_kernel,
        grid=(nb,),
        in_specs=[tok, cache, cache, tok, tok, pl.BlockSpec((N_HEADS_B, rows, BAND_WIN), lambda b: (0, 0, 0))],
        out_specs=tok,
        out_shape=jax.ShapeDtypeStruct((nb, rows, B_COLS), BF16),
        compiler_params=_cparams(1),
        name="band_sample",
    )(q_b, k_cache_t, v_cache_t, k_new, v_new, bias)


def _tail_rows(x_ref, oa_ref, ob_ref, ada_ref, y_ref, gmix_ref, wg_ref, woa_ref, wob_ref, wout_ref, gffn_ref,
               wgate_ref, wup_ref, wdown_ref):
    x = x_ref[0]
    ada = lambda k: ada_ref[0, :, k * D_MODEL:(k + 1) * D_MODEL]
    h = x * _row_rms(x, 1.0 / D_MODEL) * gmix_ref[...]
    h = (h * (1.0 + ada(1)) + ada(0)).astype(BF16)
    gates = jax.nn.sigmoid(_dot(h, wg_ref[...]))
    y_a = _dot(oa_ref[0], woa_ref[...])
    y_b = _dot(ob_ref[0], wob_ref[...])
    mixed = gates[:, 0:D_MODEL] * y_a + gates[:, D_MODEL:2 * D_MODEL] * y_b
    x1 = x + ada(2) * _dot(mixed.astype(BF16), wout_ref[...])
    h2 = x1 * _row_rms(x1, 1.0 / D_MODEL) * gffn_ref[...]
    h2 = (h2 * (1.0 + ada(4)) + ada(3)).astype(BF16)
    acc = jnp.zeros_like(x1)
    for c in range(D_FF // FF_CHUNK):
        cols = slice(c * FF_CHUNK, (c + 1) * FF_CHUNK)
        g = _dot(h2, wgate_ref[:, cols])
        u = _dot(h2, wup_ref[:, cols])
        act = (g * jax.nn.sigmoid(g) * u).astype(BF16)
        acc = acc + _dot(act, wdown_ref[cols, :])
    y_ref[0] = x1 + ada(5) * acc


def _tail_kernel(x_ref, oa_ref, ob_ref, ada_ref, xs_ref, oas_ref, obs_ref, adas_ref, *rest):
    weights, (y_ref, ys_ref) = rest[:-2], rest[-2:]
    _tail_rows(x_ref, oa_ref, ob_ref, ada_ref, y_ref, *weights)

    @pl.when((pl.program_id(0) == 0) & (pl.program_id(1) == 0))
    def _():
        _tail_rows(xs_ref, oas_ref, obs_ref, adas_ref, ys_ref, *weights)


def _tail_call(x, o_a, o_b, ada, xs, o_a_s, o_b_s, ada_s, weights, tm):
    nb, sb, _ = x.shape
    rows_s = xs.shape[1]
    tok = lambda c: pl.BlockSpec((1, tm, c), lambda b, j: (b, j, 0))
    whole = lambda c: pl.BlockSpec((1, rows_s, c), lambda b, j: (0, 0, 0))
    const_in = [weights["g_mix"], weights["w_in_gate"], weights["w_o_a"], weights["w_o_b"], weights["w_out"],
                weights["g_ffn"], weights["w_gate"], weights["w_up"], weights["w_down"]]
    return pl.pallas_call(
        _tail_kernel,
        grid=(nb, sb // tm),
        in_specs=[tok(D_MODEL), tok(VA_COLS), tok(B_COLS), pl.BlockSpec((1, 1, 6 * D_MODEL), lambda b, j: (b, 0, 0)),
                  whole(D_MODEL), whole(VA_COLS), whole(B_COLS), whole(6 * D_MODEL)]
        + [_const_spec(a.shape) for a in const_in],
        out_specs=[tok(D_MODEL), whole(D_MODEL)],
        out_shape=[jax.ShapeDtypeStruct((nb, sb, D_MODEL), F32), jax.ShapeDtypeStruct((1, rows_s, D_MODEL), F32)],
        compiler_params=_cparams(2),
        name="tail",
    )(x, o_a, o_b, ada, xs, o_a_s, o_b_s, ada_s, *const_in)


def _group_constants():
    def pack(g, inv_cnt):
        ic = np.ones((1, LANES), np.float32)
        ic[0, :len(inv_cnt)] = inv_cnt
        return jnp.asarray(g, BF16), jnp.asarray(np.concatenate([g.T, g.T], axis=0), BF16), jnp.asarray(ic)

    g_q = np.zeros((QA_COLS, LANES), np.float32)
    for hd in range(N_HEADS_A):
        g_q[hd * HEAD_BLOCK:hd * HEAD_BLOCK + NOPE_DIM, hd] = 1.0
        g_q[hd * HEAD_BLOCK + ROPE_LANE0:hd * HEAD_BLOCK + ROPE_LANE0 + ROPE_DIM, N_HEADS_A + hd] = 1.0
    g_b = np.zeros((B_COLS, LANES), np.float32)
    for hd in range(N_HEADS_B):
        g_b[hd * HEAD_DIM_B:(hd + 1) * HEAD_DIM_B, hd] = 1.0
    c = {}
    c["g_q"], c["e2_q"], c["ic_q"] = pack(g_q, [1.0 / NOPE_DIM] * N_HEADS_A + [1.0 / ROPE_DIM] * N_HEADS_A)
    c["g_b"], c["e2_b"], c["ic_b"] = pack(g_b, [1.0 / HEAD_DIM_B] * N_HEADS_B)
    return c


def _rope_tables(pos):
    inv_freq = ROPE_BASE ** (-jnp.arange(HALF_ROPE, dtype=F32) / HALF_ROPE)
    ang = pos.astype(F32)[:, None] * inv_freq[None, :]
    cos, sin = jnp.cos(ang), jnp.sin(ang)
    n = pos.shape[0]
    ones = jnp.ones((n, ROPE_LANE0), F32)
    zeros = jnp.zeros((n, ROPE_LANE0), F32)
    pad1 = jnp.ones((n, LANES - ROPE_LANE0 - ROPE_DIM), F32)
    pad0 = jnp.zeros((n, LANES - ROPE_LANE0 - ROPE_DIM), F32)
    z16 = jnp.zeros((n, HALF_ROPE), F32)
    cos_t = jnp.concatenate([ones, cos, cos, pad1], axis=1)
    sin_a = jnp.concatenate([zeros, -sin, z16, pad0], axis=1)
    sin_b = jnp.concatenate([zeros, z16, sin, pad0], axis=1)
    return cos_t, sin_a, sin_b


def _layer_weights(l, w_in, g_norm_mix, g_q_lora, w_q_up, g_kv_lora, w_kv_up, g_qn_a, g_kn_a, g_qr_a, g_kr_a,
                   g_q_b, g_k_b, w_o_a, w_o_b, w_out, g_norm_ffn, w_gate, w_up, w_down):
    w = {}
    w["w_in_mix"], w["w_in_gate"] = _w_in_prep_call(w_in[l], 256)
    wq3 = w_q_up[l].reshape(Q_LORA, N_HEADS_A, QK_DIM_A)
    w["w_q_up"] = jnp.pad(wq3, ((0, 0), (0, 0), (0, HEAD_BLOCK - QK_DIM_A))).reshape(Q_LORA, QA_COLS).astype(BF16)
    w["w_kv_up"] = w_kv_up[l].astype(BF16)
    zpad = jnp.zeros((HEAD_BLOCK - QK_DIM_A,), F32)
    w["gq_row"] = jnp.concatenate([g_qn_a[l], g_qr_a[l], zpad]).reshape(1, HEAD_BLOCK)
    w["gk_row"] = jnp.concatenate([g_kn_a[l], jnp.zeros((HEAD_BLOCK - NOPE_DIM,), F32)]).reshape(1, HEAD_BLOCK)
    w["gkr_row"] = jnp.concatenate([jnp.zeros((ROPE_LANE0,), F32), g_kr_a[l], zpad]).reshape(1, LANES)
    w["gqb_row"] = jnp.tile(g_q_b[l], N_HEADS_B).reshape(1, B_COLS)
    w["gkb_row"] = jnp.tile(g_k_b[l], N_HEADS_B).reshape(1, B_COLS)
    w["g_mix"] = g_norm_mix[l].reshape(1, D_MODEL)
    w["g_q_lora"] = g_q_lora[l].reshape(1, Q_LORA)
    w["g_kv_lora"] = g_kv_lora[l].reshape(1, KV_LORA)
    w["g_ffn"] = g_norm_ffn[l].reshape(1, D_MODEL)
    w["w_o_a"] = w_o_a[l].astype(BF16)
    w["w_o_b"] = w_o_b[l].astype(BF16)
    w["w_out"] = w_out[l].astype(BF16)
    w["w_gate"] = w_gate[l].astype(BF16)
    w["w_up"] = w_up[l].astype(BF16)
    w["w_down"] = w_down[l].astype(BF16)
    return w


def kernel(x_prompt, x_sample, c_prompt, c_sample, cache_kv_latent, cache_k_rope, cache_band_k, cache_band_v, w_ada, b_ada, g_norm_mix, w_in, g_q_lora, w_q_up, g_kv_lora, w_kv_up, g_qn_a, g_kn_a, g_qr_a, g_kr_a, g_q_b, g_k_b, rel_bias, w_o_a, w_o_b, w_out, g_norm_ffn, w_gate, w_up, w_down):
    depth = w_in.shape[0]
    nb, seq, _ = x_prompt.shape
    nbs, sd, _ = x_sample.shape
    past = cache_kv_latent.shape[2]
    n_buf = cache_band_k.shape[2]
    keep = min(BAND_WINDOW, seq)
    assert depth == 1 and nbs * sd == LANES and n_buf == BAND_WINDOW and seq % 512 == 0 and past % 512 == 0
    tm = 512
    rows_s = nbs * sd

    consts = _group_constants()
    tab_p = _rope_tables(jnp.arange(seq))
    tab_s = _rope_tables(past + (jnp.arange(rows_s) % sd))
    xs = x_sample.reshape(1, rows_s, D_MODEL)

    l = 0
    wts = _layer_weights(l, w_in, g_norm_mix, g_q_lora, w_q_up, g_kv_lora, w_kv_up, g_qn_a, g_kn_a, g_qr_a, g_kr_a,
                         g_q_b, g_k_b, w_o_a, w_o_b, w_out, g_norm_ffn, w_gate, w_up, w_down)
    band_bias = _band_bias_call(rel_bias[l][:, 2 * REL_CLIP:0:-1])

    ada = _ada_call(jnp.concatenate([c_prompt, c_sample], axis=0), w_ada[l], b_ada[l])
    ada_p = ada[:nb].reshape(nb, 1, 6 * D_MODEL)
    ada_s = jnp.repeat(ada[nb:], sd, axis=0).reshape(1, rows_s, 6 * D_MODEL)

    assert keep == tm
    qa, lat, _, krt, k_a, v_a, qb, kb, vb, kbt_tail, vbt_tail = _mixer_in_call(x_prompt, ada_p, tab_p, consts, wts,
                                                                                2 * tm, tm, True)
    o_a = _mla_prompt_call(qa, k_a, v_a, 512, 4)
    o_b = _band_prompt_call(qb, kb, vb, band_bias, 4)

    qa_s, lat_s, kr_s, _, kn, vn, qb_s, kb_s, vb_s, kb_s32, vb_s32 = _mixer_in_call(xs, ada_s, tab_s, consts, wts,
                                                                                    rows_s, rows_s, False)
    o_a_s = _mla_sample_call(qa_s.reshape(nbs, sd, QA_COLS), cache_kv_latent[l],
                             jnp.transpose(cache_k_rope[l], (0, 2, 1)),
                             kn.reshape(nbs, sd, QA_COLS), vn.reshape(nbs, sd, QA_COLS), wts, tm)
    feat_major = lambda c: jnp.transpose(c, (0, 2, 3, 1)).reshape(nbs, B_COLS, n_buf)
    o_b_s = _band_sample_call(qb_s.reshape(nbs, sd, B_COLS), feat_major(cache_band_k[l]), feat_major(cache_band_v[l]),
                              kb_s.reshape(nbs, sd, B_COLS), vb_s.reshape(nbs, sd, B_COLS), band_bias)

    y_p, y_s = _tail_call(x_prompt, o_a, o_b, ada_p, xs, o_a_s.reshape(1, rows_s, VA_COLS),
                          o_b_s.reshape(1, rows_s, B_COLS), ada_s, wts, tm)

    tok_major = lambda t: jnp.transpose(t.reshape(nb, N_HEADS_B, HEAD_DIM_B, keep), (0, 3, 1, 2))[None]
    return (y_p, y_s.reshape(nbs, sd, D_MODEL),
            lat.reshape(1, nb, seq, KV_LORA), jnp.transpose(krt, (0, 2, 1))[None],
            tok_major(kbt_tail), tok_major(vbt_tail),
            lat_s.reshape(1, nbs, sd, KV_LORA), kr_s.reshape(1, nbs, sd, ROPE_DIM),
            kb_s32.reshape(1, nbs, sd, N_HEADS_B, HEAD_DIM_B), vb_s32.reshape(1, nbs, sd, N_HEADS_B, HEAD_DIM_B))
```

```python
import functools

import jax
import jax.numpy as jnp
import numpy as np
from jax import lax
from jax.experimental import pallas as pl
from jax.experimental.pallas import tpu as pltpu

D_MODEL = 1024
CHUNK = 64
EPS = 1e-6
NEG_INF = -1e30
N_HEADS_A = 8
NOPE_DIM = 64
ROPE_DIM = 32
HALF_ROPE = ROPE_DIM // 2
V_DIM_A = 64
QK_DIM_A = NOPE_DIM + ROPE_DIM
Q_LORA = 384
KV_LORA = 256
ROPE_BASE = 10000.0
N_HEADS_B = 8
HEAD_DIM_B = 64
LEFT_CHUNKS = 8
BAND_WINDOW = LEFT_CHUNKS * CHUNK
REL_CLIP = 128
D_FF = -(-(8 * D_MODEL) // (3 * 256)) * 256
COL_QKV_B = 3 * N_HEADS_B * HEAD_DIM_B
N_GATE_COLS = 2 * D_MODEL

LANES = 128
HEAD_BLOCK = LANES
ROPE_LANE0 = NOPE_DIM
QA_COLS = N_HEADS_A * HEAD_BLOCK
VA_COLS = N_HEADS_A * V_DIM_A
B_COLS = N_HEADS_B * HEAD_DIM_B
MIX_COLS = Q_LORA + KV_LORA + LANES + COL_QKV_B
OFF_CKV = Q_LORA
OFF_KPE = Q_LORA + KV_LORA
OFF_QB = OFF_KPE + LANES
OFF_KB = OFF_QB + B_COLS
OFF_VB = OFF_KB + B_COLS
FF_CHUNK = 256
LOG2E = 1.4426950408889634
BAND_TQ = 256
BAND_WIN = BAND_WINDOW + BAND_TQ
VMEM_LIMIT = 60 * 1024 * 1024

BF16 = jnp.bfloat16
F32 = jnp.float32


def _cparams(n_axes):
    return pltpu.CompilerParams(dimension_semantics=("arbitrary",) * n_axes, vmem_limit_bytes=VMEM_LIMIT)


def _const_spec(shape):
    nd = len(shape)
    return pl.BlockSpec(shape, lambda *_: (0,) * nd, pipeline_mode=pl.Buffered(1))


def _dot(a, b):
    return jnp.dot(a, b, preferred_element_type=F32)


def _dot_nt(a, b):
    return lax.dot_general(a, b, (((1,), (1,)), ((), ())), preferred_element_type=F32)


def _pad_rows(x, rows):
    return jnp.concatenate([x, jnp.zeros((rows - x.shape[0], x.shape[1]), x.dtype)], axis=0)


def _row_rms(x, inv_n):
    return lax.rsqrt(jnp.sum(x * x, axis=-1, keepdims=True) * inv_n + EPS)


def _group_rms(x, g_ref, e2_ref, invcnt_ref):
    s = _dot((x * x).astype(BF16), g_ref[...])
    r = lax.rsqrt(s * invcnt_ref[...] + EPS)
    r_hi = r.astype(BF16)
    r_lo = (r - r_hi.astype(F32)).astype(BF16)
    return _dot(jnp.concatenate([r_hi, r_lo], axis=1), e2_ref[...])


def _rope_block(x, cos_t, sin_a, sin_b):
    return x * cos_t + pltpu.roll(x, LANES - HALF_ROPE, 1) * sin_a + pltpu.roll(x, HALF_ROPE, 1) * sin_b


def _ada_kernel(c_ref, w_ref, b_ref, o_ref):
    c = c_ref[...]
    a = (c * jax.nn.sigmoid(c)).astype(BF16)
    o_ref[...] = _dot(a, w_ref[...].astype(BF16)) + b_ref[...]


def _ada_call(c_all, w_ada, b_ada):
    rows = c_all.shape[0]
    n_out = w_ada.shape[1]
    tn = D_MODEL
    return pl.pallas_call(
        _ada_kernel,
        grid=(n_out // tn,),
        in_specs=[pl.BlockSpec((rows, D_MODEL), lambda n: (0, 0)),
                  pl.BlockSpec((D_MODEL, tn), lambda n: (0, n)),
                  pl.BlockSpec((1, tn), lambda n: (0, n))],
        out_specs=pl.BlockSpec((rows, tn), lambda n: (0, n)),
        out_shape=jax.ShapeDtypeStruct((rows, n_out), F32),
        compiler_params=_cparams(1),
        name="ada",
    )(c_all, w_ada, b_ada.reshape(1, n_out))


def _w_in_prep_kernel(wt_ref, mix_ref, gate_ref):
    tk = wt_ref.shape[1]
    n_mix = OFF_KPE + ROPE_DIM + COL_QKV_B

    def put(dst_ref, col0, src0):
        dst_ref[:, col0:col0 + LANES] = wt_ref[src0:src0 + LANES, :].T.astype(BF16)

    for c in range(OFF_KPE // LANES):
        put(mix_ref, c * LANES, c * LANES)
    kpe = jnp.concatenate([jnp.zeros((ROPE_LANE0, tk), F32), wt_ref[OFF_KPE:OFF_KPE + ROPE_DIM, :],
                           jnp.zeros((LANES - ROPE_LANE0 - ROPE_DIM, tk), F32)], axis=0)
    mix_ref[:, OFF_KPE:OFF_QB] = kpe.T.astype(BF16)
    for c in range(COL_QKV_B // LANES):
        put(mix_ref, OFF_QB + c * LANES, OFF_KPE + ROPE_DIM + c * LANES)
    for c in range(N_GATE_COLS // LANES):
        put(gate_ref, c * LANES, n_mix + c * LANES)


def _w_in_prep_call(w_in_t, tk):
    n_cols, d_in = w_in_t.shape
    return pl.pallas_call(
        _w_in_prep_kernel,
        grid=(d_in // tk,),
        in_specs=[pl.BlockSpec((n_cols, tk), lambda i: (0, i))],
        out_specs=[pl.BlockSpec((tk, MIX_COLS), lambda i: (i, 0)), pl.BlockSpec((tk, N_GATE_COLS), lambda i: (i, 0))],
        out_shape=[jax.ShapeDtypeStruct((d_in, MIX_COLS), BF16), jax.ShapeDtypeStruct((d_in, N_GATE_COLS), BF16)],
        compiler_params=_cparams(1),
        name="w_in_prep",
    )(w_in_t)


def _mixer_in_kernel(sub, tail_feat_major,
                     x_ref, ada_ref, cos_ref, sina_ref, sinb_ref, gmix_ref, w_in_ref, gql_ref, wq_ref, gkv_ref,
                     gq_row_ref, gkr_row_ref, gqb_row_ref, gkb_row_ref,
                     gq_ref, e2q_ref, icq_ref, gb_ref, e2b_ref, icb_ref, wkv_ref, gk_row_ref,
                     qa_ref, lat_ref, kr_ref, krt_ref, ka_ref, va_ref, qb_ref, kb_ref, vb_ref, kbt_ref, vbt_ref):
    n_sub = x_ref.shape[1] // sub
    for si in range(n_sub):
        rows = slice(si * sub, (si + 1) * sub)
        x = x_ref[0, rows, :]
        ada_rows = rows if ada_ref.shape[1] > 1 else slice(None)
        shift = ada_ref[0, ada_rows, 0:D_MODEL]
        scale = ada_ref[0, ada_rows, D_MODEL:2 * D_MODEL]
        h = x * _row_rms(x, 1.0 / D_MODEL) * gmix_ref[...]
        h = h * (1.0 + scale) + shift
        z = _dot(h.astype(BF16), w_in_ref[...])

        cos_t = cos_ref[rows, :]
        sin_a = sina_ref[rows, :]
        sin_b = sinb_ref[rows, :]

        c_q = z[:, 0:Q_LORA]
        cqn = c_q * _row_rms(c_q, 1.0 / Q_LORA) * gql_ref[...]
        q_raw = _dot(cqn.astype(BF16), wq_ref[...])
        r_full = _group_rms(q_raw, gq_ref, e2q_ref, icq_ref)
        q_gain = gq_row_ref[...] * (QK_DIM_A ** -0.5 * LOG2E)
        for hd in range(N_HEADS_A):
            hb = slice(hd * HEAD_BLOCK, (hd + 1) * HEAD_BLOCK)
            qa_ref[0, rows, hb] = _rope_block(q_raw[:, hb] * r_full[:, hb] * q_gain, cos_t, sin_a, sin_b).astype(BF16)

        c_kv = z[:, OFF_CKV:OFF_CKV + KV_LORA]
        lat = c_kv * _row_rms(c_kv, 1.0 / KV_LORA) * gkv_ref[...]
        lat_ref[0, rows, :] = lat

        kp = z[:, OFF_KPE:OFF_KPE + LANES]
        kr = _rope_block(kp * _row_rms(kp, 1.0 / ROPE_DIM) * gkr_row_ref[...], cos_t, sin_a, sin_b)
        kr_ref[0, rows, :] = pltpu.roll(kr, LANES - ROPE_LANE0, 1)[:, 0:ROPE_DIM]
        krt_ref[0, :, rows] = kr.T[ROPE_LANE0:ROPE_LANE0 + ROPE_DIM, :]

        def store(hd, k_blk, v_blk):
            ka_ref[0, rows, hd * HEAD_BLOCK:(hd + 1) * HEAD_BLOCK] = k_blk
            va_ref[0, rows, hd * HEAD_BLOCK:(hd + 1) * HEAD_BLOCK] = v_blk

        _expand_kv(lat, kr, wkv_ref[...], gk_row_ref[...], store)

        zq = z[:, OFF_QB:OFF_QB + B_COLS]
        qb_ref[0, rows, :] = (zq * _group_rms(zq, gb_ref, e2b_ref, icb_ref)
                              * (gqb_row_ref[...] * (HEAD_DIM_B ** -0.5 * LOG2E))).astype(BF16)
        zk = z[:, OFF_KB:OFF_KB + B_COLS]
        k_b = zk * _group_rms(zk, gb_ref, e2b_ref, icb_ref) * gkb_row_ref[...]
        v_b = z[:, OFF_VB:OFF_VB + B_COLS]
        kb_ref[0, rows, :] = k_b.astype(BF16)
        vb_ref[0, rows, :] = v_b.astype(BF16)

        if si == n_sub - 1:
            @pl.when(pl.program_id(1) == pl.num_programs(1) - 1)
            def _():
                kbt_ref[0] = k_b.T if tail_feat_major else k_b
                vbt_ref[0] = v_b.T if tail_feat_major else v_b


def _mixer_in_call(x, ada, tables, consts, weights, tm, sub, tail_feat_major):
    nb, sb, _ = x.shape
    nj = sb // tm
    keep = sub
    assert tm % sub == 0 and sb % tm == 0
    ada_rows = ada.shape[1]
    if ada_rows == 1:
        ada_spec = pl.BlockSpec((1, 1, 6 * D_MODEL), lambda b, j: (b, 0, 0))
    else:
        ada_spec = pl.BlockSpec((1, tm, 6 * D_MODEL), lambda b, j: (b, j, 0))
    tab_spec = pl.BlockSpec((tm, LANES), lambda b, j: (j, 0))
    tok = lambda c: pl.BlockSpec((1, tm, c), lambda b, j: (b, j, 0))
    if tail_feat_major:
        tail = pl.BlockSpec((1, B_COLS, keep), lambda b, j: (b, 0, 0))
        tail_shape = jax.ShapeDtypeStruct((nb, B_COLS, keep), F32)
    else:
        tail = pl.BlockSpec((1, keep, B_COLS), lambda b, j: (b, 0, 0))
        tail_shape = jax.ShapeDtypeStruct((nb, keep, B_COLS), F32)
    const_in = [weights["g_mix"], weights["w_in_mix"], weights["g_q_lora"], weights["w_q_up"], weights["g_kv_lora"],
                weights["gq_row"], weights["gkr_row"], weights["gqb_row"], weights["gkb_row"],
                consts["g_q"], consts["e2_q"], consts["ic_q"], consts["g_b"], consts["e2_b"], consts["ic_b"],
                weights["w_kv_up"], weights["gk_row"]]
    out_shape = [jax.ShapeDtypeStruct((nb, sb, QA_COLS), BF16),
                 jax.ShapeDtypeStruct((nb, sb, KV_LORA), F32),
                 jax.ShapeDtypeStruct((nb, sb, ROPE_DIM), F32),
                 jax.ShapeDtypeStruct((nb, ROPE_DIM, sb), F32),
                 jax.ShapeDtypeStruct((nb, sb, QA_COLS), BF16),
                 jax.ShapeDtypeStruct((nb, sb, QA_COLS), BF16),
                 jax.ShapeDtypeStruct((nb, sb, B_COLS), BF16),
                 jax.ShapeDtypeStruct((nb, sb, B_COLS), BF16),
                 jax.ShapeDtypeStruct((nb, sb, B_COLS), BF16),
                 tail_shape, tail_shape]
    rope_t = pl.BlockSpec((1, ROPE_DIM, tm), lambda b, j: (b, 0, j))
    return pl.pallas_call(
        functools.partial(_mixer_in_kernel, sub, tail_feat_major),
        grid=(nb, nj),
        in_specs=[tok(D_MODEL), ada_spec, tab_spec, tab_spec, tab_spec] + [_const_spec(a.shape) for a in const_in],
        out_specs=[tok(QA_COLS), tok(KV_LORA), tok(ROPE_DIM), rope_t, tok(QA_COLS), tok(QA_COLS),
                   tok(B_COLS), tok(B_COLS), tok(B_COLS), tail, tail],
        out_shape=out_shape,
        compiler_params=_cparams(2),
        name="mixer_in",
    )(x, ada, tables[0], tables[1], tables[2], *const_in)


def _expand_kv(lat, slot, wkv, gk_row, store):
    tm = lat.shape[0]
    kv = _dot(lat.astype(BF16), wkv)
    nope = lax.broadcasted_iota(jnp.int32, (tm, LANES), 1) < NOPE_DIM
    for hd in range(wkv.shape[1] // HEAD_BLOCK):
        blk = kv[:, hd * HEAD_BLOCK:(hd + 1) * HEAD_BLOCK]
        ssq = jnp.sum(jnp.where(nope, blk * blk, 0.0), axis=-1, keepdims=True)
        r = lax.rsqrt(ssq * (1.0 / NOPE_DIM) + EPS)
        store(hd, jnp.where(nope, blk * r * gk_row, slot).astype(BF16), jnp.where(nope, 1.0, blk).astype(BF16))


def _rope_slot_from_feature_major(krt):
    tm = krt.shape[1]
    kr_tok = jnp.concatenate([krt, jnp.zeros((LANES - ROPE_DIM, tm), F32)], axis=0).T
    return pltpu.roll(kr_tok, ROPE_LANE0, 1)


def _normalize_heads(acc0, acc1, lane):
    o0 = acc0 / pltpu.roll(acc0, V_DIM_A, 1)
    o1 = acc1 / pltpu.roll(acc1, V_DIM_A, 1)
    return jnp.where(lane < V_DIM_A, pltpu.roll(o0, V_DIM_A, 1), o1)


def _mla_prompt_kernel(tq, q_ref, k_ref, v_ref, o_ref):
    seq = q_ref.shape[1]
    n_heads = q_ref.shape[2] // HEAD_BLOCK
    lane = lax.broadcasted_iota(jnp.int32, (tq, LANES), 1)
    row_c = lax.broadcasted_iota(jnp.int32, (tq, tq), 0) // CHUNK
    col_c = lax.broadcasted_iota(jnp.int32, (tq, tq), 1) // CHUNK
    diag_ok = row_c >= col_c
    for qi in reversed(range(seq // tq)):
        r0 = qi * tq
        accs = []
        for hd in range(n_heads):
            hb = slice(hd * HEAD_BLOCK, (hd + 1) * HEAD_BLOCK)
            q = q_ref[0, r0:r0 + tq, hb]
            s_d = jnp.where(diag_ok, _dot_nt(q, k_ref[0, r0:r0 + tq, hb]), NEG_INF)
            m = jnp.max(s_d, axis=-1, keepdims=True)
            if qi > 0:
                s_f = _dot_nt(q, k_ref[0, 0:r0, hb])
                m = jnp.maximum(m, jnp.max(s_f, axis=-1, keepdims=True))
            acc = _dot(jnp.exp2(s_d - m).astype(BF16), v_ref[0, r0:r0 + tq, hb])
            if qi > 0:
                acc = acc + _dot(jnp.exp2(s_f - m).astype(BF16), v_ref[0, 0:r0, hb])
            accs.append(acc)
        for pr in range(n_heads // 2):
            o_ref[0, r0:r0 + tq, pr * LANES:(pr + 1) * LANES] = _normalize_heads(
                accs[2 * pr], accs[2 * pr + 1], lane).astype(BF16)


def _mla_prompt_call(q_a, k_a, v_a, tq, heads_per_step):
    nb, seq, _ = q_a.shape
    n_groups = N_HEADS_A // heads_per_step
    group = pl.BlockSpec((1, seq, heads_per_step * HEAD_BLOCK), lambda b, g: (b, 0, g))
    return pl.pallas_call(
        functools.partial(_mla_prompt_kernel, tq),
        grid=(nb, n_groups),
        in_specs=[group, group, group],
        out_specs=pl.BlockSpec((1, seq, heads_per_step * V_DIM_A), lambda b, g: (b, 0, g)),
        out_shape=jax.ShapeDtypeStruct((nb, seq, VA_COLS), BF16),
        compiler_params=_cparams(2),
        name="mla_prompt",
    )(q_a, k_a, v_a)


def _mla_sample_kernel(tc, q_ref, lat_ref, krt_ref, kn_ref, vn_ref, wkv_ref, gk_row_ref, o_ref, kc_ref, vc_ref):
    rows = q_ref.shape[1]
    past = lat_ref.shape[1]
    lane = lax.broadcasted_iota(jnp.int32, (rows, LANES), 1)
    for ci in range(past // tc):
        c0 = ci * tc

        def store(hd, k_blk, v_blk):
            kc_ref[c0:c0 + tc, hd * HEAD_BLOCK:(hd + 1) * HEAD_BLOCK] = k_blk
            vc_ref[c0:c0 + tc, hd * HEAD_BLOCK:(hd + 1) * HEAD_BLOCK] = v_blk

        _expand_kv(lat_ref[0, c0:c0 + tc, :], _rope_slot_from_feature_major(krt_ref[0, :, c0:c0 + tc]),
                   wkv_ref[...], gk_row_ref[...], store)
    accs = []
    for hd in range(N_HEADS_A):
        hb = slice(hd * HEAD_BLOCK, (hd + 1) * HEAD_BLOCK)
        q = q_ref[0, :, hb]
        s_c = _dot_nt(q, kc_ref[:, hb])
        s_n = jnp.where(lane < rows, _dot_nt(q, _pad_rows(kn_ref[0, :, hb], LANES)), NEG_INF)
        m = jnp.maximum(jnp.max(s_c, axis=-1, keepdims=True), jnp.max(s_n, axis=-1, keepdims=True))
        accs.append(_dot(jnp.exp2(s_c - m).astype(BF16), vc_ref[:, hb])
                    + _dot(jnp.exp2(s_n - m).astype(BF16), _pad_rows(vn_ref[0, :, hb], LANES)))
    for pr in range(N_HEADS_A // 2):
        o_ref[0, :, pr * LANES:(pr + 1) * LANES] = _normalize_heads(accs[2 * pr], accs[2 * pr + 1], lane).astype(BF16)


def _mla_sample_call(q_a, latent_cache, k_rope_cache_t, k_new, v_new, weights, tc):
    nb, rows, _ = q_a.shape
    past = latent_cache.shape[1]
    tok = pl.BlockSpec((1, rows, QA_COLS), lambda b: (b, 0, 0))
    return pl.pallas_call(
        functools.partial(_mla_sample_kernel, tc),
        grid=(nb,),
        in_specs=[tok, pl.BlockSpec((1, past, KV_LORA), lambda b: (b, 0, 0)),
                  pl.BlockSpec((1, ROPE_DIM, past), lambda b: (b, 0, 0)), tok, tok,
                  _const_spec(weights["w_kv_up"].shape), _const_spec(weights["gk_row"].shape)],
        out_specs=pl.BlockSpec((1, rows, VA_COLS), lambda b: (b, 0, 0)),
        out_shape=jax.ShapeDtypeStruct((nb, rows, VA_COLS), BF16),
        scratch_shapes=[pltpu.VMEM((past, QA_COLS), BF16), pltpu.VMEM((past, QA_COLS), BF16)],
        compiler_params=_cparams(1),
        name="mla_sample",
    )(q_a, latent_cache, k_rope_cache_t, k_new, v_new, weights["w_kv_up"], weights["gk_row"])


def _toeplitz_bias(g0, rows):
    far = g0[:, 0:1]
    x0 = jnp.broadcast_to(g0[:, 0:LANES], (rows, LANES))
    x1 = jnp.broadcast_to(g0[:, LANES:2 * LANES], (rows, LANES))
    row = lax.broadcasted_iota(jnp.int32, (rows, LANES), 0)
    lane = lax.broadcasted_iota(jnp.int32, (rows, LANES), 1)
    step = 1
    while step < rows:
        r0 = pltpu.roll(x0, step, 1)
        r1 = pltpu.roll(x1, step, 1)
        keep = lane >= step
        take = (row & step) != 0
        x0, x1 = jnp.where(take, jnp.where(keep, r0, r1), x0), jnp.where(take, jnp.where(keep, r1, r0), x1)
        step *= 2
    return jnp.where(lane < row, far, x0), x1, far


def _band_bias_kernel(rb_ref, bias_ref):
    hd = pl.program_id(0)
    tw0, tw1, far = _toeplitz_bias(rb_ref[pl.ds(hd, 1), :], LANES)
    far_blk = jnp.broadcast_to(far, (LANES, LANES))
    n_blk = BAND_WIN // LANES
    row_c = lax.broadcasted_iota(jnp.int32, (LANES, LANES), 0) // CHUNK
    lane = lax.broadcasted_iota(jnp.int32, (LANES, LANES), 1)
    for half in range(BAND_TQ // LANES):
        first_tw = BAND_WINDOW // LANES - 1 + half
        for cb in range(n_blk):
            blk = tw0 if cb == first_tw else (tw1 if cb == first_tw + 1 else far_blk)
            q_c = row_c + half * (LANES // CHUNK)
            col_c = (cb * LANES + lane) // CHUNK
            ok = (col_c >= q_c) & (col_c <= q_c + LEFT_CHUNKS)
            bias_ref[0, half * LANES:(half + 1) * LANES, cb * LANES:(cb + 1) * LANES] = jnp.where(ok, blk * LOG2E, NEG_INF)


def _band_bias_call(rb_rev):
    return pl.pallas_call(
        _band_bias_kernel,
        grid=(N_HEADS_B,),
        in_specs=[_const_spec(rb_rev.shape)],
        out_specs=pl.BlockSpec((1, BAND_TQ, BAND_WIN), lambda h: (h, 0, 0)),
        out_shape=jax.ShapeDtypeStruct((N_HEADS_B, BAND_TQ, BAND_WIN), F32),
        compiler_params=_cparams(1),
        name="band_bias",
    )(rb_rev)


def _split_heads(q, lane):
    zero = jnp.zeros_like(q)
    return jnp.concatenate([jnp.where(lane < HEAD_DIM_B, q, zero), jnp.where(lane >= HEAD_DIM_B, q, zero)], axis=0)


def _band_prompt_kernel(q_ref, k_ref, v_ref, bias_ref, o_ref, vext_ref):
    seq = q_ref.shape[1]
    n_pairs = q_ref.shape[2] // LANES
    lane_q = lax.broadcasted_iota(jnp.int32, (BAND_TQ, LANES), 1)
    for pr in range(n_pairs):
        vext_ref[pr, :, 0:LANES] = v_ref[0, :, pr * LANES:(pr + 1) * LANES]
        vext_ref[pr, :, LANES:2 * LANES] = jnp.ones((seq, LANES), BF16)
    for t in reversed(range(seq // BAND_TQ)):
        t0 = t * BAND_TQ
        k_lo = max(t0 - BAND_WINDOW, 0)
        w = t0 + BAND_TQ - k_lo
        for pr in range(n_pairs):
            cols = slice(pr * LANES, (pr + 1) * LANES)
            q2 = _split_heads(q_ref[0, t0:t0 + BAND_TQ, cols], lane_q)
            bias2 = jnp.concatenate([bias_ref[2 * pr, :, BAND_WIN - w:BAND_WIN],
                                     bias_ref[2 * pr + 1, :, BAND_WIN - w:BAND_WIN]], axis=0)
            s = _dot_nt(q2, k_ref[0, k_lo:t0 + BAND_TQ, cols]) + bias2
            p = jnp.exp2(s - jnp.max(s, axis=-1, keepdims=True))
            acc = _dot(p.astype(BF16), vext_ref[pr, k_lo:t0 + BAND_TQ, :])
            o2 = acc[:, 0:LANES] / acc[:, LANES:2 * LANES]
            o_ref[0, t0:t0 + BAND_TQ, cols] = jnp.where(lane_q < HEAD_DIM_B, o2[0:BAND_TQ],
                                                        o2[BAND_TQ:2 * BAND_TQ]).astype(BF16)


def _band_prompt_call(q_b, k_b, v_b, bias, pairs_per_step):
    nb, seq, _ = q_b.shape
    n_groups = N_HEADS_B // (2 * pairs_per_step)
    spec = pl.BlockSpec((1, seq, pairs_per_step * LANES), lambda b, p: (b, 0, p))
    return pl.pallas_call(
        _band_prompt_kernel,
        grid=(nb, n_groups),
        in_specs=[spec, spec, spec, pl.BlockSpec((2 * pairs_per_step, BAND_TQ, BAND_WIN), lambda b, p: (p, 0, 0),
                                                 pipeline_mode=pl.Buffered(1 if n_groups == 1 else 2))],
        out_specs=spec,
        out_shape=jax.ShapeDtypeStruct((nb, seq, B_COLS), BF16),
        scratch_shapes=[pltpu.VMEM((pairs_per_step, seq, 2 * LANES), BF16)],
        compiler_params=_cparams(2),
        name="band_prompt",
    )(q_b, k_b, v_b, bias)


def _band_sample_kernel(q_ref, kct_ref, vct_ref, kn_ref, vn_ref, bias_ref, o_ref):
    rows = q_ref.shape[1]
    n_cache = kct_ref.shape[2]
    lane = lax.broadcasted_iota(jnp.int32, (rows, LANES), 1)
    lane2 = lax.broadcasted_iota(jnp.int32, (2 * rows, LANES), 1)
    for pair in range(N_HEADS_B // 2):
        cols = slice(pair * LANES, (pair + 1) * LANES)
        q2 = _split_heads(q_ref[0, :, cols], lane)
        kct = kct_ref[0, cols, :].astype(BF16)
        vct = vct_ref[0, cols, :].astype(BF16)
        kn = _pad_rows(kn_ref[0, :, cols], LANES)
        vn = _pad_rows(vn_ref[0, :, cols], LANES)
        bias_c = jnp.concatenate([bias_ref[2 * pair, :, 0:n_cache], bias_ref[2 * pair + 1, :, 0:n_cache]], axis=0)
        bias_n = jnp.concatenate([bias_ref[2 * pair, :, n_cache:n_cache + LANES],
                                  bias_ref[2 * pair + 1, :, n_cache:n_cache + LANES]], axis=0)
        s_c = _dot(q2, kct) + bias_c
        s_n = jnp.where(lane2 < rows, _dot_nt(q2, kn) + bias_n, NEG_INF)
        m = jnp.maximum(jnp.max(s_c, axis=-1, keepdims=True), jnp.max(s_n, axis=-1, keepdims=True))
        p_c = jnp.exp2(s_c - m)
        p_n = jnp.exp2(s_n - m)
        l = jnp.sum(p_c, axis=-1, keepdims=True) + jnp.sum(p_n, axis=-1, keepdims=True)
        o2 = (_dot_nt(p_c.astype(BF16), vct) + _dot(p_n.astype(BF16), vn)) / l
        o_ref[0, :, cols] = jnp.where(lane < HEAD_DIM_B, o2[0:rows], o2[rows:2 * rows]).astype(BF16)


def _band_sample_call(q_b, k_cache_t, v_cache_t, k_new, v_new, bias):
    nb, rows, _ = q_b.shape
    n_cache = k_cache_t.shape[2]
    tok = pl.BlockSpec((1, rows, B_COLS), lambda b: (b, 0, 0))
    cache = pl.BlockSpec((1, B_COLS, n_cache), lambda b: (b, 0, 0))
    return pl.pallas_call(
        _band_sample_kernel,
        grid=(nb,),
        in_specs=[tok, cache, cache, tok, tok, pl.BlockSpec((N_HEADS_B, rows, BAND_WIN), lambda b: (0, 0, 0))],
        out_specs=tok,
        out_shape=jax.ShapeDtypeStruct((nb, rows, B_COLS), BF16),
        compiler_params=_cparams(1),
        name="band_sample",
    )(q_b, k_cache_t, v_cache_t, k_new, v_new, bias)


def _tail_rows(x_ref, oa_ref, ob_ref, ada_ref, y_ref, gmix_ref, wg_ref, woa_ref, wob_ref, wout_ref, gffn_ref,
               wgate_ref, wup_ref, wdown_ref):
    x = x_ref[0]
    ada = lambda k: ada_ref[0, :, k * D_MODEL:(k + 1) * D_MODEL]
    h = x * _row_rms(x, 1.0 / D_MODEL) * gmix_ref[...]
    h = (h * (1.0 + ada(1)) + ada(0)).astype(BF16)
    gates = jax.nn.sigmoid(_dot(h, wg_ref[...]))
    y_a = _dot(oa_ref[0], woa_ref[...])
    y_b = _dot(ob_ref[0], wob_ref[...])
    mixed = gates[:, 0:D_MODEL] * y_a + gates[:, D_MODEL:2 * D_MODEL] * y_b
    x1 = x + ada(2) * _dot(mixed.astype(BF16), wout_ref[...])
    h2 = x1 * _row_rms(x1, 1.0 / D_MODEL) * gffn_ref[...]
    h2 = (h2 * (1.0 + ada(4)) + ada(3)).astype(BF16)
    acc = jnp.zeros_like(x1)
    for c in range(D_FF // FF_CHUNK):
        cols = slice(c * FF_CHUNK, (c + 1) * FF_CHUNK)
        g = _dot(h2, wgate_ref[:, cols])
        u = _dot(h2, wup_ref[:, cols])
        act = (g * jax.nn.sigmoid(g) * u).astype(BF16)
        acc = acc + _dot(act, wdown_ref[cols, :])
    y_ref[0] = x1 + ada(5) * acc


def _tail_kernel(x_ref, oa_ref, ob_ref, ada_ref, xs_ref, oas_ref, obs_ref, adas_ref, *rest):
    weights, (y_ref, ys_ref) = rest[:-2], rest[-2:]
    _tail_rows(x_ref, oa_ref, ob_ref, ada_ref, y_ref, *weights)

    @pl.when((pl.program_id(0) == 0) & (pl.program_id(1) == 0))
    def _():
        _tail_rows(xs_ref, oas_ref, obs_ref, adas_ref, ys_ref, *weights)


def _tail_call(x, o_a, o_b, ada, xs, o_a_s, o_b_s, ada_s, weights, tm):
    nb, sb, _ = x.shape
    rows_s = xs.shape[1]
    tok = lambda c: pl.BlockSpec((1, tm, c), lambda b, j: (b, j, 0))
    whole = lambda c: pl.BlockSpec((1, rows_s, c), lambda b, j: (0, 0, 0))
    const_in = [weights["g_mix"], weights["w_in_gate"], weights["w_o_a"], weights["w_o_b"], weights["w_out"],
                weights["g_ffn"], weights["w_gate"], weights["w_up"], weights["w_down"]]
    return pl.pallas_call(
        _tail_kernel,
        grid=(nb, sb // tm),
        in_specs=[tok(D_MODEL), tok(VA_COLS), tok(B_COLS), pl.BlockSpec((1, 1, 6 * D_MODEL), lambda b, j: (b, 0, 0)),
                  whole(D_MODEL), whole(VA_COLS), whole(B_COLS), whole(6 * D_MODEL)]
        + [_const_spec(a.shape) for a in const_in],
        out_specs=[tok(D_MODEL), whole(D_MODEL)],
        out_shape=[jax.ShapeDtypeStruct((nb, sb, D_MODEL), F32), jax.ShapeDtypeStruct((1, rows_s, D_MODEL), F32)],
        compiler_params=_cparams(2),
        name="tail",
    )(x, o_a, o_b, ada, xs, o_a_s, o_b_s, ada_s, *const_in)


def _group_constants():
    def pack(g, inv_cnt):
        ic = np.ones((1, LANES), np.float32)
        ic[0, :len(inv_cnt)] = inv_cnt
        return jnp.asarray(g, BF16), jnp.asarray(np.concatenate([g.T, g.T], axis=0), BF16), jnp.asarray(ic)

    g_q = np.zeros((QA_COLS, LANES), np.float32)
    for hd in range(N_HEADS_A):
        g_q[hd * HEAD_BLOCK:hd * HEAD_BLOCK + NOPE_DIM, hd] = 1.0
        g_q[hd * HEAD_BLOCK + ROPE_LANE0:hd * HEAD_BLOCK + ROPE_LANE0 + ROPE_DIM, N_HEADS_A + hd] = 1.0
    g_b = np.zeros((B_COLS, LANES), np.float32)
    for hd in range(N_HEADS_B):
        g_b[hd * HEAD_DIM_B:(hd + 1) * HEAD_DIM_B, hd] = 1.0
    c = {}
    c["g_q"], c["e2_q"], c["ic_q"] = pack(g_q, [1.0 / NOPE_DIM] * N_HEADS_A + [1.0 / ROPE_DIM] * N_HEADS_A)
    c["g_b"], c["e2_b"], c["ic_b"] = pack(g_b, [1.0 / HEAD_DIM_B] * N_HEADS_B)
    return c


def _rope_tables(pos):
    inv_freq = ROPE_BASE ** (-jnp.arange(HALF_ROPE, dtype=F32) / HALF_ROPE)
    ang = pos.astype(F32)[:, None] * inv_freq[None, :]
    cos, sin = jnp.cos(ang), jnp.sin(ang)
    n = pos.shape[0]
    ones = jnp.ones((n, ROPE_LANE0), F32)
    zeros = jnp.zeros((n, ROPE_LANE0), F32)
    pad1 = jnp.ones((n, LANES - ROPE_LANE0 - ROPE_DIM), F32)
    pad0 = jnp.zeros((n, LANES - ROPE_LANE0 - ROPE_DIM), F32)
    z16 = jnp.zeros((n, HALF_ROPE), F32)
    cos_t = jnp.concatenate([ones, cos, cos, pad1], axis=1)
    sin_a = jnp.concatenate([zeros, -sin, z16, pad0], axis=1)
    sin_b = jnp.concatenate([zeros, z16, sin, pad0], axis=1)
    return cos_t, sin_a, sin_b


def _layer_weights(l, w_in, g_norm_mix, g_q_lora, w_q_up, g_kv_lora, w_kv_up, g_qn_a, g_kn_a, g_qr_a, g_kr_a,
                   g_q_b, g_k_b, w_o_a, w_o_b, w_out, g_norm_ffn, w_gate, w_up, w_down):
    w = {}
    w["w_in_mix"], w["w_in_gate"] = _w_in_prep_call(jnp.transpose(w_in[l]), 256)
    wq3 = w_q_up[l].reshape(Q_LORA, N_HEADS_A, QK_DIM_A)
    w["w_q_up"] = jnp.pad(wq3, ((0, 0), (0, 0), (0, HEAD_BLOCK - QK_DIM_A))).reshape(Q_LORA, QA_COLS).astype(BF16)
    w["w_kv_up"] = w_kv_up[l].astype(BF16)
    zpad = jnp.zeros((HEAD_BLOCK - QK_DIM_A,), F32)
    w["gq_row"] = jnp.concatenate([g_qn_a[l], g_qr_a[l], zpad]).reshape(1, HEAD_BLOCK)
    w["gk_row"] = jnp.concatenate([g_kn_a[l], jnp.zeros((HEAD_BLOCK - NOPE_DIM,), F32)]).reshape(1, HEAD_BLOCK)
    w["gkr_row"] = jnp.concatenate([jnp.zeros((ROPE_LANE0,), F32), g_kr_a[l], zpad]).reshape(1, LANES)
    w["gqb_row"] = jnp.tile(g_q_b[l], N_HEADS_B).reshape(1, B_COLS)
    w["gkb_row"] = jnp.tile(g_k_b[l], N_HEADS_B).reshape(1, B_COLS)
    w["g_mix"] = g_norm_mix[l].reshape(1, D_MODEL)
    w["g_q_lora"] = g_q_lora[l].reshape(1, Q_LORA)
    w["g_kv_lora"] = g_kv_lora[l].reshape(1, KV_LORA)
    w["g_ffn"] = g_norm_ffn[l].reshape(1, D_MODEL)
    w["w_o_a"] = w_o_a[l].astype(BF16)
    w["w_o_b"] = w_o_b[l].astype(BF16)
    w["w_out"] = w_out[l].astype(BF16)
    w["w_gate"] = w_gate[l].astype(BF16)
    w["w_up"] = w_up[l].astype(BF16)
    w["w_down"] = w_down[l].astype(BF16)
    return w


def kernel(x_prompt, x_sample, c_prompt, c_sample, cache_kv_latent, cache_k_rope, cache_band_k, cache_band_v, w_ada, b_ada, g_norm_mix, w_in, g_q_lora, w_q_up, g_kv_lora, w_kv_up, g_qn_a, g_kn_a, g_qr_a, g_kr_a, g_q_b, g_k_b, rel_bias, w_o_a, w_o_b, w_out, g_norm_ffn, w_gate, w_up, w_down):
    depth = w_in.shape[0]
    nb, seq, _ = x_prompt.shape
    nbs, sd, _ = x_sample.shape
    past = cache_kv_latent.shape[2]
    n_buf = cache_band_k.shape[2]
    keep = min(BAND_WINDOW, seq)
    assert depth == 1 and nbs * sd == LANES and n_buf == BAND_WINDOW and seq % 512 == 0 and past % 512 == 0
    tm = 512
    rows_s = nbs * sd

    consts = _group_constants()
    tab_p = _rope_tables(jnp.arange(seq))
    tab_s = _rope_tables(past + (jnp.arange(rows_s) % sd))
    xs = x_sample.reshape(1, rows_s, D_MODEL)

    l = 0
    wts = _layer_weights(l, w_in, g_norm_mix, g_q_lora, w_q_up, g_kv_lora, w_kv_up, g_qn_a, g_kn_a, g_qr_a, g_kr_a,
                         g_q_b, g_k_b, w_o_a, w_o_b, w_out, g_norm_ffn, w_gate, w_up, w_down)
    band_bias = _band_bias_call(rel_bias[l][:, 2 * REL_CLIP:0:-1])

    ada = _ada_call(jnp.concatenate([c_prompt, c_sample], axis=0), w_ada[l], b_ada[l])
    ada_p = ada[:nb].reshape(nb, 1, 6 * D_MODEL)
    ada_s = jnp.repeat(ada[nb:], sd, axis=0).reshape(1, rows_s, 6 * D_MODEL)

    assert keep == tm
    qa, lat, _, krt, k_a, v_a, qb, kb, vb, kbt_tail, vbt_tail = _mixer_in_call(x_prompt, ada_p, tab_p, consts, wts,
                                                                                2 * tm, tm, True)
    o_a = _mla_prompt_call(qa, k_a, v_a, 512, 4)
    o_b = _band_prompt_call(qb, kb, vb, band_bias, 4)

    qa_s, lat_s, kr_s, _, kn, vn, qb_s, kb_s, vb_s, kb_s32, vb_s32 = _mixer_in_call(xs, ada_s, tab_s, consts, wts,
                                                                                    rows_s, rows_s, False)
    o_a_s = _mla_sample_call(qa_s.reshape(nbs, sd, QA_COLS), cache_kv_latent[l],
                             jnp.transpose(cache_k_rope[l], (0, 2, 1)),
                             kn.reshape(nbs, sd, QA_COLS), vn.reshape(nbs, sd, QA_COLS), wts, tm)
    feat_major = lambda c: jnp.transpose(c, (0, 2, 3, 1)).reshape(nbs, B_COLS, n_buf)
    o_b_s = _band_sample_call(qb_s.reshape(nbs, sd, B_COLS), feat_major(cache_band_k[l]), feat_major(cache_band_v[l]),
                              kb_s.reshape(nbs, sd, B_COLS), vb_s.reshape(nbs, sd, B_COLS), band_bias)

    y_p, y_s = _tail_call(x_prompt, o_a, o_b, ada_p, xs, o_a_s.reshape(1, rows_s, VA_COLS),
                          o_b_s.reshape(1, rows_s, B_COLS), ada_s, wts, tm)

    tok_major = lambda t: jnp.transpose(t.reshape(nb, N_HEADS_B, HEAD_DIM_B, keep), (0, 3, 1, 2))[None]
    return (y_p, y_s.reshape(nbs, sd, D_MODEL),
            lat.reshape(1, nb, seq, KV_LORA), jnp.transpose(krt, (0, 2, 1))[None],
            tok_major(kbt_tail), tok_major(vbt_tail),
            lat_s.reshape(1, nbs, sd, KV_LORA), kr_s.reshape(1, nbs, sd, ROPE_DIM),
            kb_s32.reshape(1, nbs, sd, N_HEADS_B, HEAD_DIM_B), vb_s32.reshape(1, nbs, sd, N_HEADS_B, HEAD_DIM_B))
```

```python
import functools

import jax
import jax.numpy as jnp
import numpy as np
from jax import lax
from jax.experimental import pallas as pl
from jax.experimental.pallas import tpu as pltpu

D_MODEL = 1024
CHUNK = 64
EPS = 1e-6
NEG_INF = -1e30
N_HEADS_A = 8
NOPE_DIM = 64
ROPE_DIM = 32
HALF_ROPE = ROPE_DIM // 2
V_DIM_A = 64
QK_DIM_A = NOPE_DIM + ROPE_DIM
Q_LORA = 384
KV_LORA = 256
ROPE_BASE = 10000.0
N_HEADS_B = 8
HEAD_DIM_B = 64
LEFT_CHUNKS = 8
BAND_WINDOW = LEFT_CHUNKS * CHUNK
REL_CLIP = 128
D_FF = -(-(8 * D_MODEL) // (3 * 256)) * 256
COL_QKV_B = 3 * N_HEADS_B * HEAD_DIM_B
N_GATE_COLS = 2 * D_MODEL

LANES = 128
HEAD_BLOCK = LANES
ROPE_LANE0 = NOPE_DIM
QA_COLS = N_HEADS_A * HEAD_BLOCK
VA_COLS = N_HEADS_A * V_DIM_A
B_COLS = N_HEADS_B * HEAD_DIM_B
MIX_COLS = Q_LORA + KV_LORA + LANES + COL_QKV_B
OFF_CKV = Q_LORA
OFF_KPE = Q_LORA + KV_LORA
OFF_QB = OFF_KPE + LANES
OFF_KB = OFF_QB + B_COLS
OFF_VB = OFF_KB + B_COLS
FF_CHUNK = 256
LOG2E = 1.4426950408889634
BAND_TQ = 256
BAND_WIN = BAND_WINDOW + BAND_TQ
VMEM_LIMIT = 60 * 1024 * 1024

BF16 = jnp.bfloat16
F32 = jnp.float32


def _cparams(n_axes):
    return pltpu.CompilerParams(dimension_semantics=("arbitrary",) * n_axes, vmem_limit_bytes=VMEM_LIMIT)


def _const_spec(shape):
    nd = len(shape)
    return pl.BlockSpec(shape, lambda *_: (0,) * nd, pipeline_mode=pl.Buffered(1))


def _dot(a, b):
    return jnp.dot(a, b, preferred_element_type=F32)


def _dot_nt(a, b):
    return lax.dot_general(a, b, (((1,), (1,)), ((), ())), preferred_element_type=F32)


def _pad_rows(x, rows):
    return jnp.concatenate([x, jnp.zeros((rows - x.shape[0], x.shape[1]), x.dtype)], axis=0)


def _row_rms(x, inv_n):
    return lax.rsqrt(jnp.sum(x * x, axis=-1, keepdims=True) * inv_n + EPS)


def _group_rms(x, g_ref, e2_ref, invcnt_ref):
    s = _dot((x * x).astype(BF16), g_ref[...])
    r = lax.rsqrt(s * invcnt_ref[...] + EPS)
    r_hi = r.astype(BF16)
    r_lo = (r - r_hi.astype(F32)).astype(BF16)
    return _dot(jnp.concatenate([r_hi, r_lo], axis=1), e2_ref[...])


def _rope_block(x, cos_t, sin_t):
    return x * cos_t + pltpu.roll(x, LANES - ROPE_DIM, 1) * sin_t


def _ada_kernel(c_ref, w_ref, b_ref, o_ref):
    c = c_ref[...]
    a = (c * jax.nn.sigmoid(c)).astype(BF16)
    o_ref[...] = _dot(a, w_ref[...].astype(BF16)) + b_ref[...]


def _ada_call(c_all, w_ada, b_ada):
    rows = c_all.shape[0]
    n_out = w_ada.shape[1]
    tn = D_MODEL
    return pl.pallas_call(
        _ada_kernel,
        grid=(n_out // tn,),
        in_specs=[pl.BlockSpec((rows, D_MODEL), lambda n: (0, 0)),
                  pl.BlockSpec((D_MODEL, tn), lambda n: (0, n)),
                  pl.BlockSpec((1, tn), lambda n: (0, n))],
        out_specs=pl.BlockSpec((rows, tn), lambda n: (0, n)),
        out_shape=jax.ShapeDtypeStruct((rows, n_out), F32),
        compiler_params=_cparams(1),
        name="ada",
    )(c_all, w_ada, b_ada.reshape(1, n_out))


def _w_in_prep_kernel(wt_ref, mix_ref, gate_ref):
    tk = wt_ref.shape[1]
    n_mix = OFF_KPE + ROPE_DIM + COL_QKV_B

    def put(dst_ref, col0, src0):
        dst_ref[:, col0:col0 + LANES] = wt_ref[src0:src0 + LANES, :].T.astype(BF16)

    for c in range(OFF_KPE // LANES):
        put(mix_ref, c * LANES, c * LANES)
    kpe = jnp.concatenate([jnp.zeros((ROPE_LANE0, tk), F32), wt_ref[OFF_KPE:OFF_KPE + ROPE_DIM, :],
                           wt_ref[OFF_KPE + HALF_ROPE:OFF_KPE + ROPE_DIM, :],
                           wt_ref[OFF_KPE:OFF_KPE + HALF_ROPE, :]], axis=0)
    mix_ref[:, OFF_KPE:OFF_QB] = kpe.T.astype(BF16)
    for c in range(COL_QKV_B // LANES):
        put(mix_ref, OFF_QB + c * LANES, OFF_KPE + ROPE_DIM + c * LANES)
    for c in range(N_GATE_COLS // LANES):
        put(gate_ref, c * LANES, n_mix + c * LANES)


def _w_in_prep_call(w_in_t, tk):
    n_cols, d_in = w_in_t.shape
    return pl.pallas_call(
        _w_in_prep_kernel,
        grid=(d_in // tk,),
        in_specs=[pl.BlockSpec((n_cols, tk), lambda i: (0, i))],
        out_specs=[pl.BlockSpec((tk, MIX_COLS), lambda i: (i, 0)), pl.BlockSpec((tk, N_GATE_COLS), lambda i: (i, 0))],
        out_shape=[jax.ShapeDtypeStruct((d_in, MIX_COLS), BF16), jax.ShapeDtypeStruct((d_in, N_GATE_COLS), BF16)],
        compiler_params=_cparams(1),
        name="w_in_prep",
    )(w_in_t)


def _mixer_in_kernel(sub, tail_feat_major,
                     x_ref, ada_ref, cos_ref, sin_ref, gmix_ref, w_in_ref, gql_ref, wq_ref, gkv_ref,
                     gq_row_ref, gkr_row_ref, gqb_row_ref, gkb_row_ref,
                     gq_ref, e2q_ref, icq_ref, gb_ref, e2b_ref, icb_ref, wkv_ref, gk_row_ref,
                     qa_ref, lat_ref, kr_ref, krt_ref, ka_ref, va_ref, qb_ref, kb_ref, vb_ref, kbt_ref, vbt_ref):
    n_sub = x_ref.shape[1] // sub
    for si in range(n_sub):
        rows = slice(si * sub, (si + 1) * sub)
        x = x_ref[0, rows, :]
        ada_rows = rows if ada_ref.shape[1] > 1 else slice(None)
        shift = ada_ref[0, ada_rows, 0:D_MODEL]
        scale = ada_ref[0, ada_rows, D_MODEL:2 * D_MODEL]
        h = x * _row_rms(x, 1.0 / D_MODEL) * gmix_ref[...]
        h = h * (1.0 + scale) + shift
        z = _dot(h.astype(BF16), w_in_ref[...])

        cos_t = cos_ref[rows, :]
        sin_t = sin_ref[rows, :]

        c_q = z[:, 0:Q_LORA]
        cqn = c_q * _row_rms(c_q, 1.0 / Q_LORA) * gql_ref[...]
        q_raw = _dot(cqn.astype(BF16), wq_ref[...])
        r_full = _group_rms(q_raw, gq_ref, e2q_ref, icq_ref)
        q_gain = gq_row_ref[...] * (QK_DIM_A ** -0.5 * LOG2E)
        for hd in range(N_HEADS_A):
            hb = slice(hd * HEAD_BLOCK, (hd + 1) * HEAD_BLOCK)
            qa_ref[0, rows, hb] = _rope_block(q_raw[:, hb] * r_full[:, hb] * q_gain, cos_t, sin_t).astype(BF16)

        c_kv = z[:, OFF_CKV:OFF_CKV + KV_LORA]
        lat = c_kv * _row_rms(c_kv, 1.0 / KV_LORA) * gkv_ref[...]
        lat_ref[0, rows, :] = lat

        kp = z[:, OFF_KPE:OFF_KPE + LANES]
        kr = _rope_block(kp * _row_rms(kp, 0.5 / ROPE_DIM) * gkr_row_ref[...], cos_t, sin_t)
        kr = jnp.where(lax.broadcasted_iota(jnp.int32, kr.shape, 1) < ROPE_LANE0 + ROPE_DIM, kr, 0.0)
        kr_ref[0, rows, :] = pltpu.roll(kr, LANES - ROPE_LANE0, 1)[:, 0:ROPE_DIM]
        krt_ref[0, :, rows] = kr.T[ROPE_LANE0:ROPE_LANE0 + ROPE_DIM, :]

        def store(hd, k_blk, v_blk):
            ka_ref[0, rows, hd * HEAD_BLOCK:(hd + 1) * HEAD_BLOCK] = k_blk
            va_ref[0, rows, hd * HEAD_BLOCK:(hd + 1) * HEAD_BLOCK] = v_blk

        _expand_kv(lat, kr, wkv_ref[...], gk_row_ref[...], store)

        zq = z[:, OFF_QB:OFF_QB + B_COLS]
        qb_ref[0, rows, :] = (zq * _group_rms(zq, gb_ref, e2b_ref, icb_ref)
                              * (gqb_row_ref[...] * (HEAD_DIM_B ** -0.5 * LOG2E))).astype(BF16)
        zk = z[:, OFF_KB:OFF_KB + B_COLS]
        k_b = zk * _group_rms(zk, gb_ref, e2b_ref, icb_ref) * gkb_row_ref[...]
        v_b = z[:, OFF_VB:OFF_VB + B_COLS]
        kb_ref[0, rows, :] = k_b.astype(BF16)
        vb_ref[0, rows, :] = v_b.astype(BF16)

        if si == n_sub - 1:
            @pl.when(pl.program_id(1) == pl.num_programs(1) - 1)
            def _():
                kbt_ref[0] = k_b.T if tail_feat_major else k_b
                vbt_ref[0] = v_b.T if tail_feat_major else v_b


def _mixer_in_call(x, ada, tables, consts, weights, tm, sub, tail_feat_major):
    nb, sb, _ = x.shape
    nj = sb // tm
    keep = sub
    assert tm % sub == 0 and sb % tm == 0
    ada_rows = ada.shape[1]
    if ada_rows == 1:
        ada_spec = pl.BlockSpec((1, 1, 6 * D_MODEL), lambda b, j: (b, 0, 0))
    else:
        ada_spec = pl.BlockSpec((1, tm, 6 * D_MODEL), lambda b, j: (b, j, 0))
    tab_spec = pl.BlockSpec((tm, LANES), lambda b, j: (j, 0))
    tok = lambda c: pl.BlockSpec((1, tm, c), lambda b, j: (b, j, 0))
    if tail_feat_major:
        tail = pl.BlockSpec((1, B_COLS, keep), lambda b, j: (b, 0, 0))
        tail_shape = jax.ShapeDtypeStruct((nb, B_COLS, keep), F32)
    else:
        tail = pl.BlockSpec((1, keep, B_COLS), lambda b, j: (b, 0, 0))
        tail_shape = jax.ShapeDtypeStruct((nb, keep, B_COLS), F32)
    const_in = [weights["g_mix"], weights["w_in_mix"], weights["g_q_lora"], weights["w_q_up"], weights["g_kv_lora"],
                weights["gq_row"], weights["gkr_row"], weights["gqb_row"], weights["gkb_row"],
                consts["g_q"], consts["e2_q"], consts["ic_q"], consts["g_b"], consts["e2_b"], consts["ic_b"],
                weights["w_kv_up"], weights["gk_row"]]
    out_shape = [jax.ShapeDtypeStruct((nb, sb, QA_COLS), BF16),
                 jax.ShapeDtypeStruct((nb, sb, KV_LORA), F32),
                 jax.ShapeDtypeStruct((nb, sb, ROPE_DIM), F32),
                 jax.ShapeDtypeStruct((nb, ROPE_DIM, sb), F32),
                 jax.ShapeDtypeStruct((nb, sb, QA_COLS), BF16),
                 jax.ShapeDtypeStruct((nb, sb, QA_COLS), BF16),
                 jax.ShapeDtypeStruct((nb, sb, B_COLS), BF16),
                 jax.ShapeDtypeStruct((nb, sb, B_COLS), BF16),
                 jax.ShapeDtypeStruct((nb, sb, B_COLS), BF16),
                 tail_shape, tail_shape]
    rope_t = pl.BlockSpec((1, ROPE_DIM, tm), lambda b, j: (b, 0, j))
    return pl.pallas_call(
        functools.partial(_mixer_in_kernel, sub, tail_feat_major),
        grid=(nb, nj),
        in_specs=[tok(D_MODEL), ada_spec, tab_spec, tab_spec] + [_const_spec(a.shape) for a in const_in],
        out_specs=[tok(QA_COLS), tok(KV_LORA), tok(ROPE_DIM), rope_t, tok(QA_COLS), tok(QA_COLS),
                   tok(B_COLS), tok(B_COLS), tok(B_COLS), tail, tail],
        out_shape=out_shape,
        compiler_params=_cparams(2),
        name="mixer_in",
    )(x, ada, tables[0], tables[1], *const_in)


def _expand_kv(lat, slot, wkv, gk_row, store):
    tm = lat.shape[0]
    kv = _dot(lat.astype(BF16), wkv)
    nope = lax.broadcasted_iota(jnp.int32, (tm, LANES), 1) < NOPE_DIM
    for hd in range(wkv.shape[1] // HEAD_BLOCK):
        blk = kv[:, hd * HEAD_BLOCK:(hd + 1) * HEAD_BLOCK]
        ssq = jnp.sum(jnp.where(nope, blk * blk, 0.0), axis=-1, keepdims=True)
        r = lax.rsqrt(ssq * (1.0 / NOPE_DIM) + EPS)
        store(hd, jnp.where(nope, blk * r * gk_row, slot).astype(BF16), jnp.where(nope, 1.0, blk).astype(BF16))


def _rope_slot_from_feature_major(krt):
    tm = krt.shape[1]
    kr_tok = jnp.concatenate([krt, jnp.zeros((LANES - ROPE_DIM, tm), F32)], axis=0).T
    return pltpu.roll(kr_tok, ROPE_LANE0, 1)


def _normalize_heads(acc0, acc1, lane):
    o0 = acc0 / pltpu.roll(acc0, V_DIM_A, 1)
    o1 = acc1 / pltpu.roll(acc1, V_DIM_A, 1)
    return jnp.where(lane < V_DIM_A, pltpu.roll(o0, V_DIM_A, 1), o1)


def _mla_prompt_kernel(tq, q_ref, k_ref, v_ref, o_ref):
    seq = q_ref.shape[1]
    n_heads = q_ref.shape[2] // HEAD_BLOCK
    lane = lax.broadcasted_iota(jnp.int32, (tq, LANES), 1)
    row_c = lax.broadcasted_iota(jnp.int32, (tq, tq), 0) // CHUNK
    col_c = lax.broadcasted_iota(jnp.int32, (tq, tq), 1) // CHUNK
    diag_ok = row_c >= col_c
    for qi in reversed(range(seq // tq)):
        r0 = qi * tq
        accs = []
        for hd in range(n_heads):
            hb = slice(hd * HEAD_BLOCK, (hd + 1) * HEAD_BLOCK)
            q = q_ref[0, r0:r0 + tq, hb]
            s_d = jnp.where(diag_ok, _dot_nt(q, k_ref[0, r0:r0 + tq, hb]), NEG_INF)
            m = jnp.max(s_d, axis=-1, keepdims=True)
            if qi > 0:
                s_f = _dot_nt(q, k_ref[0, 0:r0, hb])
                m = jnp.maximum(m, jnp.max(s_f, axis=-1, keepdims=True))
            acc = _dot(jnp.exp2(s_d - m).astype(BF16), v_ref[0, r0:r0 + tq, hb])
            if qi > 0:
                acc = acc + _dot(jnp.exp2(s_f - m).astype(BF16), v_ref[0, 0:r0, hb])
            accs.append(acc)
        for pr in range(n_heads // 2):
            o_ref[0, r0:r0 + tq, pr * LANES:(pr + 1) * LANES] = _normalize_heads(
                accs[2 * pr], accs[2 * pr + 1], lane).astype(BF16)


def _mla_prompt_call(q_a, k_a, v_a, tq, heads_per_step):
    nb, seq, _ = q_a.shape
    n_groups = N_HEADS_A // heads_per_step
    group = pl.BlockSpec((1, seq, heads_per_step * HEAD_BLOCK), lambda b, g: (b, 0, g))
    return pl.pallas_call(
        functools.partial(_mla_prompt_kernel, tq),
        grid=(nb, n_groups),
        in_specs=[group, group, group],
        out_specs=pl.BlockSpec((1, seq, heads_per_step * V_DIM_A), lambda b, g: (b, 0, g)),
        out_shape=jax.ShapeDtypeStruct((nb, seq, VA_COLS), BF16),
        compiler_params=_cparams(2),
        name="mla_prompt",
    )(q_a, k_a, v_a)


def _mla_sample_kernel(tc, q_ref, lat_ref, krt_ref, kn_ref, vn_ref, wkv_ref, gk_row_ref, o_ref, kc_ref, vc_ref):
    rows = q_ref.shape[1]
    past = lat_ref.shape[1]
    lane = lax.broadcasted_iota(jnp.int32, (rows, LANES), 1)
    for ci in range(past // tc):
        c0 = ci * tc

        def store(hd, k_blk, v_blk):
            kc_ref[c0:c0 + tc, hd * HEAD_BLOCK:(hd + 1) * HEAD_BLOCK] = k_blk
            vc_ref[c0:c0 + tc, hd * HEAD_BLOCK:(hd + 1) * HEAD_BLOCK] = v_blk

        _expand_kv(lat_ref[0, c0:c0 + tc, :], _rope_slot_from_feature_major(krt_ref[0, :, c0:c0 + tc]),
                   wkv_ref[...], gk_row_ref[...], store)
    accs = []
    for hd in range(N_HEADS_A):
        hb = slice(hd * HEAD_BLOCK, (hd + 1) * HEAD_BLOCK)
        q = q_ref[0, :, hb]
        s_c = _dot_nt(q, kc_ref[:, hb])
        s_n = jnp.where(lane < rows, _dot_nt(q, _pad_rows(kn_ref[0, :, hb], LANES)), NEG_INF)
        m = jnp.maximum(jnp.max(s_c, axis=-1, keepdims=True), jnp.max(s_n, axis=-1, keepdims=True))
        accs.append(_dot(jnp.exp2(s_c - m).astype(BF16), vc_ref[:, hb])
                    + _dot(jnp.exp2(s_n - m).astype(BF16), _pad_rows(vn_ref[0, :, hb], LANES)))
    for pr in range(N_HEADS_A // 2):
        o_ref[0, :, pr * LANES:(pr + 1) * LANES] = _normalize_heads(accs[2 * pr], accs[2 * pr + 1], lane).astype(BF16)


def _mla_sample_call(q_a, latent_cache, k_rope_cache_t, k_new, v_new, weights, tc):
    nb, rows, _ = q_a.shape
    past = latent_cache.shape[1]
    tok = pl.BlockSpec((1, rows, QA_COLS), lambda b: (b, 0, 0))
    return pl.pallas_call(
        functools.partial(_mla_sample_kernel, tc),
        grid=(nb,),
        in_specs=[tok, pl.BlockSpec((1, past, KV_LORA), lambda b: (b, 0, 0)),
                  pl.BlockSpec((1, ROPE_DIM, past), lambda b: (b, 0, 0)), tok, tok,
                  _const_spec(weights["w_kv_up"].shape), _const_spec(weights["gk_row"].shape)],
        out_specs=pl.BlockSpec((1, rows, VA_COLS), lambda b: (b, 0, 0)),
        out_shape=jax.ShapeDtypeStruct((nb, rows, VA_COLS), BF16),
        scratch_shapes=[pltpu.VMEM((past, QA_COLS), BF16), pltpu.VMEM((past, QA_COLS), BF16)],
        compiler_params=_cparams(1),
        name="mla_sample",
    )(q_a, latent_cache, k_rope_cache_t, k_new, v_new, weights["w_kv_up"], weights["gk_row"])


def _toeplitz_bias(g0, rows):
    far = g0[:, 0:1]
    x0 = jnp.broadcast_to(g0[:, 0:LANES], (rows, LANES))
    x1 = jnp.broadcast_to(g0[:, LANES:2 * LANES], (rows, LANES))
    row = lax.broadcasted_iota(jnp.int32, (rows, LANES), 0)
    lane = lax.broadcasted_iota(jnp.int32, (rows, LANES), 1)
    step = 1
    while step < rows:
        r0 = pltpu.roll(x0, step, 1)
        r1 = pltpu.roll(x1, step, 1)
        keep = lane >= step
        take = (row & step) != 0
        x0, x1 = jnp.where(take, jnp.where(keep, r0, r1), x0), jnp.where(take, jnp.where(keep, r1, r0), x1)
        step *= 2
    return jnp.where(lane < row, far, x0), x1, far


def _band_bias_kernel(rb_ref, bias_ref):
    hd = pl.program_id(0)
    tw0, tw1, far = _toeplitz_bias(rb_ref[pl.ds(hd, 1), :], LANES)
    far_blk = jnp.broadcast_to(far, (LANES, LANES))
    n_blk = BAND_WIN // LANES
    row_c = lax.broadcasted_iota(jnp.int32, (LANES, LANES), 0) // CHUNK
    lane = lax.broadcasted_iota(jnp.int32, (LANES, LANES), 1)
    for half in range(BAND_TQ // LANES):
        first_tw = BAND_WINDOW // LANES - 1 + half
        for cb in range(n_blk):
            blk = tw0 if cb == first_tw else (tw1 if cb == first_tw + 1 else far_blk)
            q_c = row_c + half * (LANES // CHUNK)
            col_c = (cb * LANES + lane) // CHUNK
            ok = (col_c >= q_c) & (col_c <= q_c + LEFT_CHUNKS)
            bias_ref[0, half * LANES:(half + 1) * LANES, cb * LANES:(cb + 1) * LANES] = jnp.where(ok, blk * LOG2E, NEG_INF)


def _band_bias_call(rb_rev):
    return pl.pallas_call(
        _band_bias_kernel,
        grid=(N_HEADS_B,),
        in_specs=[_const_spec(rb_rev.shape)],
        out_specs=pl.BlockSpec((1, BAND_TQ, BAND_WIN), lambda h: (h, 0, 0)),
        out_shape=jax.ShapeDtypeStruct((N_HEADS_B, BAND_TQ, BAND_WIN), F32),
        compiler_params=_cparams(1),
        name="band_bias",
    )(rb_rev)


def _split_heads(q, lane):
    zero = jnp.zeros_like(q)
    return jnp.concatenate([jnp.where(lane < HEAD_DIM_B, q, zero), jnp.where(lane >= HEAD_DIM_B, q, zero)], axis=0)


def _band_prompt_kernel(q_ref, k_ref, v_ref, bias_ref, o_ref, vext_ref):
    seq = q_ref.shape[1]
    n_pairs = q_ref.shape[2] // LANES
    lane_q = lax.broadcasted_iota(jnp.int32, (BAND_TQ, LANES), 1)
    for pr in range(n_pairs):
        vext_ref[pr, :, 0:LANES] = v_ref[0, :, pr * LANES:(pr + 1) * LANES]
        vext_ref[pr, :, LANES:2 * LANES] = jnp.ones((seq, LANES), BF16)
    for t in reversed(range(seq // BAND_TQ)):
        t0 = t * BAND_TQ
        k_lo = max(t0 - BAND_WINDOW, 0)
        w = t0 + BAND_TQ - k_lo
        for pr in range(n_pairs):
            cols = slice(pr * LANES, (pr + 1) * LANES)
            q2 = _split_heads(q_ref[0, t0:t0 + BAND_TQ, cols], lane_q)
            bias2 = jnp.concatenate([bias_ref[2 * pr, :, BAND_WIN - w:BAND_WIN],
                                     bias_ref[2 * pr + 1, :, BAND_WIN - w:BAND_WIN]], axis=0)
            s = _dot_nt(q2, k_ref[0, k_lo:t0 + BAND_TQ, cols]) + bias2
            p = jnp.exp2(s - jnp.max(s, axis=-1, keepdims=True))
            acc = _dot(p.astype(BF16), vext_ref[pr, k_lo:t0 + BAND_TQ, :])
            o2 = acc[:, 0:LANES] / acc[:, LANES:2 * LANES]
            o_ref[0, t0:t0 + BAND_TQ, cols] = jnp.where(lane_q < HEAD_DIM_B, o2[0:BAND_TQ],
                                                        o2[BAND_TQ:2 * BAND_TQ]).astype(BF16)


def _band_prompt_call(q_b, k_b, v_b, bias, pairs_per_step):
    nb, seq, _ = q_b.shape
    n_groups = N_HEADS_B // (2 * pairs_per_step)
    spec = pl.BlockSpec((1, seq, pairs_per_step * LANES), lambda b, p: (b, 0, p))
    return pl.pallas_call(
        _band_prompt_kernel,
        grid=(nb, n_groups),
        in_specs=[spec, spec, spec, pl.BlockSpec((2 * pairs_per_step, BAND_TQ, BAND_WIN), lambda b, p: (p, 0, 0),
                                                 pipeline_mode=pl.Buffered(1 if n_groups == 1 else 2))],
        out_specs=spec,
        out_shape=jax.ShapeDtypeStruct((nb, seq, B_COLS), BF16),
        scratch_shapes=[pltpu.VMEM((pairs_per_step, seq, 2 * LANES), BF16)],
        compiler_params=_cparams(2),
        name="band_prompt",
    )(q_b, k_b, v_b, bias)


def _band_sample_kernel(q_ref, kct_ref, vct_ref, kn_ref, vn_ref, bias_ref, o_ref):
    rows = q_ref.shape[1]
    n_cache = kct_ref.shape[2]
    lane = lax.broadcasted_iota(jnp.int32, (rows, LANES), 1)
    lane2 = lax.broadcasted_iota(jnp.int32, (2 * rows, LANES), 1)
    for pair in range(N_HEADS_B // 2):
        cols = slice(pair * LANES, (pair + 1) * LANES)
        q2 = _split_heads(q_ref[0, :, cols], lane)
        kct = kct_ref[0, cols, :].astype(BF16)
        vct = vct_ref[0, cols, :].astype(BF16)
        kn = _pad_rows(kn_ref[0, :, cols], LANES)
        vn = _pad_rows(vn_ref[0, :, cols], LANES)
        bias_c = jnp.concatenate([bias_ref[2 * pair, :, 0:n_cache], bias_ref[2 * pair + 1, :, 0:n_cache]], axis=0)
        bias_n = jnp.concatenate([bias_ref[2 * pair, :, n_cache:n_cache + LANES],
                                  bias_ref[2 * pair + 1, :, n_cache:n_cache + LANES]], axis=0)
        s_c = _dot(q2, kct) + bias_c
        s_n = jnp.where(lane2 < rows, _dot_nt(q2, kn) + bias_n, NEG_INF)
        m = jnp.maximum(jnp.max(s_c, axis=-1, keepdims=True), jnp.max(s_n, axis=-1, keepdims=True))
        p_c = jnp.exp2(s_c - m)
        p_n = jnp.exp2(s_n - m)
        l = jnp.sum(p_c, axis=-1, keepdims=True) + jnp.sum(p_n, axis=-1, keepdims=True)
        o2 = (_dot_nt(p_c.astype(BF16), vct) + _dot(p_n.astype(BF16), vn)) / l
        o_ref[0, :, cols] = jnp.where(lane < HEAD_DIM_B, o2[0:rows], o2[rows:2 * rows]).astype(BF16)


def _band_sample_call(q_b, k_cache_t, v_cache_t, k_new, v_new, bias):
    nb, rows, _ = q_b.shape
    n_cache = k_cache_t.shape[2]
    tok = pl.BlockSpec((1, rows, B_COLS), lambda b: (b, 0, 0))
    cache = pl.BlockSpec((1, B_COLS, n_cache), lambda b: (b, 0, 0))
    return pl.pallas_call(
        _band_sample_kernel,
        grid=(nb,),
        in_specs=[tok, cache, cache, tok, tok, pl.BlockSpec((N_HEADS_B, rows, BAND_WIN), lambda b: (0, 0, 0))],
        out_specs=tok,
        out_shape=jax.ShapeDtypeStruct((nb, rows, B_COLS), BF16),
        compiler_params=_cparams(1),
        name="band_sample",
    )(q_b, k_cache_t, v_cache_t, k_new, v_new, bias)


def _tail_rows(x_ref, oa_ref, ob_ref, ada_ref, y_ref, gmix_ref, wg_ref, woa_ref, wob_ref, wout_ref, gffn_ref,
               wgate_ref, wup_ref, wdown_ref):
    x = x_ref[0]
    ada = lambda k: ada_ref[0, :, k * D_MODEL:(k + 1) * D_MODEL]
    h = x * _row_rms(x, 1.0 / D_MODEL) * gmix_ref[...]
    h = (h * (1.0 + ada(1)) + ada(0)).astype(BF16)
    gates = jax.nn.sigmoid(_dot(h, wg_ref[...]))
    y_a = _dot(oa_ref[0], woa_ref[...])
    y_b = _dot(ob_ref[0], wob_ref[...])
    mixed = gates[:, 0:D_MODEL] * y_a + gates[:, D_MODEL:2 * D_MODEL] * y_b
    x1 = x + ada(2) * _dot(mixed.astype(BF16), wout_ref[...])
    h2 = x1 * _row_rms(x1, 1.0 / D_MODEL) * gffn_ref[...]
    h2 = (h2 * (1.0 + ada(4)) + ada(3)).astype(BF16)
    acc = jnp.zeros_like(x1)
    for c in range(D_FF // FF_CHUNK):
        cols = slice(c * FF_CHUNK, (c + 1) * FF_CHUNK)
        g = _dot(h2, wgate_ref[:, cols])
        u = _dot(h2, wup_ref[:, cols])
        act = (g * jax.nn.sigmoid(g) * u).astype(BF16)
        acc = acc + _dot(act, wdown_ref[cols, :])
    y_ref[0] = x1 + ada(5) * acc


def _tail_kernel(x_ref, oa_ref, ob_ref, ada_ref, xs_ref, oas_ref, obs_ref, adas_ref, *rest):
    weights, (y_ref, ys_ref) = rest[:-2], rest[-2:]
    _tail_rows(x_ref, oa_ref, ob_ref, ada_ref, y_ref, *weights)

    @pl.when((pl.program_id(0) == 0) & (pl.program_id(1) == 0))
    def _():
        _tail_rows(xs_ref, oas_ref, obs_ref, adas_ref, ys_ref, *weights)


def _tail_call(x, o_a, o_b, ada, xs, o_a_s, o_b_s, ada_s, weights, tm):
    nb, sb, _ = x.shape
    rows_s = xs.shape[1]
    tok = lambda c: pl.BlockSpec((1, tm, c), lambda b, j: (b, j, 0))
    whole = lambda c: pl.BlockSpec((1, rows_s, c), lambda b, j: (0, 0, 0))
    const_in = [weights["g_mix"], weights["w_in_gate"], weights["w_o_a"], weights["w_o_b"], weights["w_out"],
                weights["g_ffn"], weights["w_gate"], weights["w_up"], weights["w_down"]]
    return pl.pallas_call(
        _tail_kernel,
        grid=(nb, sb // tm),
        in_specs=[tok(D_MODEL), tok(VA_COLS), tok(B_COLS), pl.BlockSpec((1, 1, 6 * D_MODEL), lambda b, j: (b, 0, 0)),
                  whole(D_MODEL), whole(VA_COLS), whole(B_COLS), whole(6 * D_MODEL)]
        + [_const_spec(a.shape) for a in const_in],
        out_specs=[tok(D_MODEL), whole(D_MODEL)],
        out_shape=[jax.ShapeDtypeStruct((nb, sb, D_MODEL), F32), jax.ShapeDtypeStruct((1, rows_s, D_MODEL), F32)],
        compiler_params=_cparams(2),
        name="tail",
    )(x, o_a, o_b, ada, xs, o_a_s, o_b_s, ada_s, *const_in)


def _group_constants():
    def pack(g, e, inv_cnt):
        ic = np.ones((1, LANES), np.float32)
        ic[0, :len(inv_cnt)] = inv_cnt
        return jnp.asarray(g, BF16), jnp.asarray(np.concatenate([e, e], axis=0), BF16), jnp.asarray(ic)

    g_q = np.zeros((QA_COLS, LANES), np.float32)
    e_q = np.zeros((LANES, QA_COLS), np.float32)
    for hd in range(N_HEADS_A):
        lo = hd * HEAD_BLOCK
        g_q[lo:lo + NOPE_DIM, hd] = 1.0
        g_q[lo + ROPE_LANE0:lo + ROPE_LANE0 + ROPE_DIM, N_HEADS_A + hd] = 1.0
        e_q[hd, lo:lo + NOPE_DIM] = 1.0
        e_q[N_HEADS_A + hd, lo + ROPE_LANE0:lo + HEAD_BLOCK] = 1.0
    g_b = np.zeros((B_COLS, LANES), np.float32)
    for hd in range(N_HEADS_B):
        g_b[hd * HEAD_DIM_B:(hd + 1) * HEAD_DIM_B, hd] = 1.0
    c = {}
    c["g_q"], c["e2_q"], c["ic_q"] = pack(g_q, e_q, [1.0 / NOPE_DIM] * N_HEADS_A + [1.0 / ROPE_DIM] * N_HEADS_A)
    c["g_b"], c["e2_b"], c["ic_b"] = pack(g_b, g_b.T, [1.0 / HEAD_DIM_B] * N_HEADS_B)
    return c


def _swap_halves(v):
    return jnp.concatenate([v[..., HALF_ROPE:], v[..., :HALF_ROPE]], axis=-1)


def _rope_tables(pos):
    inv_freq = ROPE_BASE ** (-jnp.arange(HALF_ROPE, dtype=F32) / HALF_ROPE)
    ang = pos.astype(F32)[:, None] * inv_freq[None, :]
    cos, sin = jnp.cos(ang), jnp.sin(ang)
    n = pos.shape[0]
    tail = LANES - ROPE_LANE0 - ROPE_DIM
    cos_t = jnp.concatenate([jnp.ones((n, ROPE_LANE0), F32), cos, cos, jnp.ones((n, tail), F32)], axis=1)
    sin_t = jnp.concatenate([jnp.zeros((n, ROPE_LANE0), F32), -sin, sin, jnp.zeros((n, tail), F32)], axis=1)
    return cos_t, sin_t


def _layer_weights(l, w_in, g_norm_mix, g_q_lora, w_q_up, g_kv_lora, w_kv_up, g_qn_a, g_kn_a, g_qr_a, g_kr_a,
                   g_q_b, g_k_b, w_o_a, w_o_b, w_out, g_norm_ffn, w_gate, w_up, w_down):
    w = {}
    w["w_in_mix"], w["w_in_gate"] = _w_in_prep_call(jnp.transpose(w_in[l]), 256)
    wq3 = w_q_up[l].reshape(Q_LORA, N_HEADS_A, QK_DIM_A)
    w["w_q_up"] = jnp.concatenate([wq3, _swap_halves(wq3[..., NOPE_DIM:])], axis=-1).reshape(Q_LORA, QA_COLS).astype(BF16)
    w["w_kv_up"] = w_kv_up[l].astype(BF16)
    w["gq_row"] = jnp.concatenate([g_qn_a[l], g_qr_a[l], _swap_halves(g_qr_a[l])]).reshape(1, HEAD_BLOCK)
    w["gk_row"] = jnp.concatenate([g_kn_a[l], jnp.zeros((HEAD_BLOCK - NOPE_DIM,), F32)]).reshape(1, HEAD_BLOCK)
    w["gkr_row"] = jnp.concatenate([jnp.zeros((ROPE_LANE0,), F32), g_kr_a[l], _swap_halves(g_kr_a[l])]).reshape(1, LANES)
    w["gqb_row"] = jnp.tile(g_q_b[l], N_HEADS_B).reshape(1, B_COLS)
    w["gkb_row"] = jnp.tile(g_k_b[l], N_HEADS_B).reshape(1, B_COLS)
    w["g_mix"] = g_norm_mix[l].reshape(1, D_MODEL)
    w["g_q_lora"] = g_q_lora[l].reshape(1, Q_LORA)
    w["g_kv_lora"] = g_kv_lora[l].reshape(1, KV_LORA)
    w["g_ffn"] = g_norm_ffn[l].reshape(1, D_MODEL)
    w["w_o_a"] = w_o_a[l].astype(BF16)
    w["w_o_b"] = w_o_b[l].astype(BF16)
    w["w_out"] = w_out[l].astype(BF16)
    w["w_gate"] = w_gate[l].astype(BF16)
    w["w_up"] = w_up[l].astype(BF16)
    w["w_down"] = w_down[l].astype(BF16)
    return w


def kernel(x_prompt, x_sample, c_prompt, c_sample, cache_kv_latent, cache_k_rope, cache_band_k, cache_band_v, w_ada, b_ada, g_norm_mix, w_in, g_q_lora, w_q_up, g_kv_lora, w_kv_up, g_qn_a, g_kn_a, g_qr_a, g_kr_a, g_q_b, g_k_b, rel_bias, w_o_a, w_o_b, w_out, g_norm_ffn, w_gate, w_up, w_down):
    depth = w_in.shape[0]
    nb, seq, _ = x_prompt.shape
    nbs, sd, _ = x_sample.shape
    past = cache_kv_latent.shape[2]
    n_buf = cache_band_k.shape[2]
    keep = min(BAND_WINDOW, seq)
    assert depth == 1 and nbs * sd == LANES and n_buf == BAND_WINDOW and seq % 512 == 0 and past % 512 == 0
    tm = 512
    rows_s = nbs * sd

    consts = _group_constants()
    tab_p = _rope_tables(jnp.arange(seq))
    tab_s = _rope_tables(past + (jnp.arange(rows_s) % sd))
    xs = x_sample.reshape(1, rows_s, D_MODEL)

    l = 0
    wts = _layer_weights(l, w_in, g_norm_mix, g_q_lora, w_q_up, g_kv_lora, w_kv_up, g_qn_a, g_kn_a, g_qr_a, g_kr_a,
                         g_q_b, g_k_b, w_o_a, w_o_b, w_out, g_norm_ffn, w_gate, w_up, w_down)
    band_bias = _band_bias_call(rel_bias[l][:, 2 * REL_CLIP:0:-1])

    ada = _ada_call(jnp.concatenate([c_prompt, c_sample], axis=0), w_ada[l], b_ada[l])
    ada_p = ada[:nb].reshape(nb, 1, 6 * D_MODEL)
    ada_s = jnp.repeat(ada[nb:], sd, axis=0).reshape(1, rows_s, 6 * D_MODEL)

    assert keep == tm
    qa, lat, _, krt, k_a, v_a, qb, kb, vb, kbt_tail, vbt_tail = _mixer_in_call(x_prompt, ada_p, tab_p, consts, wts,
                                                                                2 * tm, tm, True)
    o_a = _mla_prompt_call(qa, k_a, v_a, 512, 4)
    o_b = _band_prompt_call(qb, kb, vb, band_bias, 4)

    qa_s, lat_s, kr_s, _, kn, vn, qb_s, kb_s, vb_s, kb_s32, vb_s32 = _mixer_in_call(xs, ada_s, tab_s, consts, wts,
                                                                                    rows_s, rows_s, False)
    o_a_s = _mla_sample_call(qa_s.reshape(nbs, sd, QA_COLS), cache_kv_latent[l],
                             jnp.transpose(cache_k_rope[l], (0, 2, 1)),
                             kn.reshape(nbs, sd, QA_COLS), vn.reshape(nbs, sd, QA_COLS), wts, tm)
    feat_major = lambda c: jnp.transpose(c, (0, 2, 3, 1)).reshape(nbs, B_COLS, n_buf)
    o_b_s = _band_sample_call(qb_s.reshape(nbs, sd, B_COLS), feat_major(cache_band_k[l]), feat_major(cache_band_v[l]),
                              kb_s.reshape(nbs, sd, B_COLS), vb_s.reshape(nbs, sd, B_COLS), band_bias)

    y_p, y_s = _tail_call(x_prompt, o_a, o_b, ada_p, xs, o_a_s.reshape(1, rows_s, VA_COLS),
                          o_b_s.reshape(1, rows_s, B_COLS), ada_s, wts, tm)

    tok_major = lambda t: jnp.transpose(t.reshape(nb, N_HEADS_B, HEAD_DIM_B, keep), (0, 3, 1, 2))[None]
    return (y_p, y_s.reshape(nbs, sd, D_MODEL),
            lat.reshape(1, nb, seq, KV_LORA), jnp.transpose(krt, (0, 2, 1))[None],
            tok_major(kbt_tail), tok_major(vbt_tail),
            lat_s.reshape(1, nbs, sd, KV_LORA), kr_s.reshape(1, nbs, sd, ROPE_DIM),
            kb_s32.reshape(1, nbs, sd, N_HEADS_B, HEAD_DIM_B), vb_s32.reshape(1, nbs, sd, N_HEADS_B, HEAD_DIM_B))
```

```python
import functools

import jax
import jax.numpy as jnp
import numpy as np
from jax import lax
from jax.experimental import pallas as pl
from jax.experimental.pallas import tpu as pltpu

D_MODEL = 1024
CHUNK = 64
EPS = 1e-6
NEG_INF = -1e30
N_HEADS_A = 8
NOPE_DIM = 64
ROPE_DIM = 32
HALF_ROPE = ROPE_DIM // 2
V_DIM_A = 64
QK_DIM_A = NOPE_DIM + ROPE_DIM
Q_LORA = 384
KV_LORA = 256
ROPE_BASE = 10000.0
N_HEADS_B = 8
HEAD_DIM_B = 64
LEFT_CHUNKS = 8
BAND_WINDOW = LEFT_CHUNKS * CHUNK
REL_CLIP = 128
D_FF = -(-(8 * D_MODEL) // (3 * 256)) * 256
COL_QKV_B = 3 * N_HEADS_B * HEAD_DIM_B
N_GATE_COLS = 2 * D_MODEL

LANES = 128
HEAD_BLOCK = LANES
ROPE_LANE0 = NOPE_DIM
QA_COLS = N_HEADS_A * HEAD_BLOCK
VA_COLS = N_HEADS_A * V_DIM_A
B_COLS = N_HEADS_B * HEAD_DIM_B
MIX_COLS = Q_LORA + KV_LORA + LANES + COL_QKV_B
OFF_CKV = Q_LORA
OFF_KPE = Q_LORA + KV_LORA
OFF_QB = OFF_KPE + LANES
OFF_KB = OFF_QB + B_COLS
OFF_VB = OFF_KB + B_COLS
FF_CHUNK = 256
LOG2E = 1.4426950408889634
BAND_TQ = 256
BAND_WIN = BAND_WINDOW + BAND_TQ
VMEM_LIMIT = 60 * 1024 * 1024

BF16 = jnp.bfloat16
F32 = jnp.float32


def _cparams(n_axes):
    return pltpu.CompilerParams(dimension_semantics=("arbitrary",) * n_axes, vmem_limit_bytes=VMEM_LIMIT)


def _const_spec(shape):
    nd = len(shape)
    return pl.BlockSpec(shape, lambda *_: (0,) * nd, pipeline_mode=pl.Buffered(1))


def _dot(a, b):
    return jnp.dot(a, b, preferred_element_type=F32)


def _dot_nt(a, b):
    return lax.dot_general(a, b, (((1,), (1,)), ((), ())), preferred_element_type=F32)


def _pad_rows(x, rows):
    return jnp.concatenate([x, jnp.zeros((rows - x.shape[0], x.shape[1]), x.dtype)], axis=0)


def _row_rms(x, inv_n):
    return lax.rsqrt(jnp.sum(x * x, axis=-1, keepdims=True) * inv_n + EPS)


def _group_rms(x, g_ref, e2_ref, invcnt_ref):
    s = _dot((x * x).astype(BF16), g_ref[...])
    r = lax.rsqrt(s * invcnt_ref[...] + EPS)
    r_hi = r.astype(BF16)
    r_lo = (r - r_hi.astype(F32)).astype(BF16)
    return _dot(jnp.concatenate([r_hi, r_lo], axis=1), e2_ref[...])


def _rope_block(x, cos_t, sin_a, sin_b):
    return x * cos_t + pltpu.roll(x, LANES - HALF_ROPE, 1) * sin_a + pltpu.roll(x, HALF_ROPE, 1) * sin_b


def _ada_kernel(c_ref, w_ref, b_ref, o_ref):
    c = c_ref[...]
    a = (c * jax.nn.sigmoid(c)).astype(BF16)
    o_ref[...] = _dot(a, w_ref[...].astype(BF16)) + b_ref[...]


def _ada_call(c_all, w_ada, b_ada):
    rows = c_all.shape[0]
    n_out = w_ada.shape[1]
    tn = D_MODEL
    return pl.pallas_call(
        _ada_kernel,
        grid=(n_out // tn,),
        in_specs=[pl.BlockSpec((rows, D_MODEL), lambda n: (0, 0)),
                  pl.BlockSpec((D_MODEL, tn), lambda n: (0, n)),
                  pl.BlockSpec((1, tn), lambda n: (0, n))],
        out_specs=pl.BlockSpec((rows, tn), lambda n: (0, n)),
        out_shape=jax.ShapeDtypeStruct((rows, n_out), F32),
        compiler_params=_cparams(1),
        name="ada",
    )(c_all, w_ada, b_ada.reshape(1, n_out))


def _w_in_prep_kernel(wt_ref, mix_ref, gate_ref):
    tk = wt_ref.shape[1]
    n_mix = OFF_KPE + ROPE_DIM + COL_QKV_B

    def put(dst_ref, col0, src0):
        dst_ref[:, col0:col0 + LANES] = wt_ref[src0:src0 + LANES, :].T.astype(BF16)

    for c in range(OFF_KPE // LANES):
        put(mix_ref, c * LANES, c * LANES)
    kpe = jnp.concatenate([jnp.zeros((ROPE_LANE0, tk), F32), wt_ref[OFF_KPE:OFF_KPE + ROPE_DIM, :],
                           jnp.zeros((LANES - ROPE_LANE0 - ROPE_DIM, tk), F32)], axis=0)
    mix_ref[:, OFF_KPE:OFF_QB] = kpe.T.astype(BF16)
    for c in range(COL_QKV_B // LANES):
        put(mix_ref, OFF_QB + c * LANES, OFF_KPE + ROPE_DIM + c * LANES)
    for c in range(N_GATE_COLS // LANES):
        put(gate_ref, c * LANES, n_mix + c * LANES)


def _w_in_prep_call(w_in_t, tk):
    n_cols, d_in = w_in_t.shape
    return pl.pallas_call(
        _w_in_prep_kernel,
        grid=(d_in // tk,),
        in_specs=[pl.BlockSpec((n_cols, tk), lambda i: (0, i))],
        out_specs=[pl.BlockSpec((tk, MIX_COLS), lambda i: (i, 0)), pl.BlockSpec((tk, N_GATE_COLS), lambda i: (i, 0))],
        out_shape=[jax.ShapeDtypeStruct((d_in, MIX_COLS), BF16), jax.ShapeDtypeStruct((d_in, N_GATE_COLS), BF16)],
        compiler_params=_cparams(1),
        name="w_in_prep",
    )(w_in_t)


def _mixer_in_kernel(sub, tail_feat_major,
                     x_ref, ada_ref, cos_ref, sina_ref, sinb_ref, gmix_ref, w_in_ref, gql_ref, wq_ref, gkv_ref,
                     gq_row_ref, gkr_row_ref, gqb_row_ref, gkb_row_ref,
                     gq_ref, e2q_ref, icq_ref, gb_ref, e2b_ref, icb_ref, wkv_ref, gk_row_ref,
                     qa_ref, lat_ref, kr_ref, krt_ref, ka_ref, va_ref, qb_ref, kb_ref, vb_ref, kbt_ref, vbt_ref):
    n_sub = x_ref.shape[1] // sub
    for si in range(n_sub):
        rows = slice(si * sub, (si + 1) * sub)
        x = x_ref[0, rows, :]
        ada_rows = rows if ada_ref.shape[1] > 1 else slice(None)
        shift = ada_ref[0, ada_rows, 0:D_MODEL]
        scale = ada_ref[0, ada_rows, D_MODEL:2 * D_MODEL]
        h = x * _row_rms(x, 1.0 / D_MODEL) * (gmix_ref[...] * (1.0 + scale)) + shift
        z = _dot(h.astype(BF16), w_in_ref[...])

        cos_t = cos_ref[rows, :]
        sin_a = sina_ref[rows, :]
        sin_b = sinb_ref[rows, :]

        c_q = z[:, 0:Q_LORA]
        cqn = c_q * _row_rms(c_q, 1.0 / Q_LORA) * gql_ref[...]
        q_raw = _dot(cqn.astype(BF16), wq_ref[...])
        r_full = _group_rms(q_raw, gq_ref, e2q_ref, icq_ref)
        q_gain = gq_row_ref[...] * (QK_DIM_A ** -0.5 * LOG2E)
        cos_b, sin_a_b, sin_b_b = cos_t.astype(BF16), sin_a.astype(BF16), sin_b.astype(BF16)
        for hd in range(N_HEADS_A):
            hb = slice(hd * HEAD_BLOCK, (hd + 1) * HEAD_BLOCK)
            qn = (q_raw[:, hb] * r_full[:, hb] * q_gain).astype(BF16)
            qa_ref[0, rows, hb] = _rope_block(qn, cos_b, sin_a_b, sin_b_b)

        c_kv = z[:, OFF_CKV:OFF_CKV + KV_LORA]
        lat = c_kv * _row_rms(c_kv, 1.0 / KV_LORA) * gkv_ref[...]
        lat_ref[0, rows, :] = lat

        kp = z[:, OFF_KPE:OFF_KPE + LANES]
        kr = _rope_block(kp * _row_rms(kp, 1.0 / ROPE_DIM) * gkr_row_ref[...], cos_t, sin_a, sin_b)
        kr_ref[0, rows, :] = pltpu.roll(kr, LANES - ROPE_LANE0, 1)[:, 0:ROPE_DIM]
        krt_ref[0, :, rows] = kr.T[ROPE_LANE0:ROPE_LANE0 + ROPE_DIM, :]

        def store(hd, k_blk, v_blk):
            ka_ref[0, rows, hd * HEAD_BLOCK:(hd + 1) * HEAD_BLOCK] = k_blk
            va_ref[0, rows, hd * HEAD_BLOCK:(hd + 1) * HEAD_BLOCK] = v_blk

        _expand_kv(lat, kr, wkv_ref[...], gk_row_ref[...], store)

        zq = z[:, OFF_QB:OFF_QB + B_COLS]
        qb_ref[0, rows, :] = (zq * _group_rms(zq, gb_ref, e2b_ref, icb_ref)
                              * (gqb_row_ref[...] * (HEAD_DIM_B ** -0.5 * LOG2E))).astype(BF16)
        zk = z[:, OFF_KB:OFF_KB + B_COLS]
        k_b = zk * _group_rms(zk, gb_ref, e2b_ref, icb_ref) * gkb_row_ref[...]
        v_b = z[:, OFF_VB:OFF_VB + B_COLS]
        kb_ref[0, rows, :] = k_b.astype(BF16)
        vb_ref[0, rows, :] = v_b.astype(BF16)

        if si == n_sub - 1:
            @pl.when(pl.program_id(1) == pl.num_programs(1) - 1)
            def _():
                kbt_ref[0] = k_b.T if tail_feat_major else k_b
                vbt_ref[0] = v_b.T if tail_feat_major else v_b


def _mixer_in_call(x, ada, tables, consts, weights, tm, sub, tail_feat_major):
    nb, sb, _ = x.shape
    nj = sb // tm
    keep = sub
    assert tm % sub == 0 and sb % tm == 0
    ada_rows = ada.shape[1]
    if ada_rows == 1:
        ada_spec = pl.BlockSpec((1, 1, 6 * D_MODEL), lambda b, j: (b, 0, 0))
    else:
        ada_spec = pl.BlockSpec((1, tm, 6 * D_MODEL), lambda b, j: (b, j, 0))
    tab_spec = pl.BlockSpec((tm, LANES), lambda b, j: (j, 0))
    tok = lambda c: pl.BlockSpec((1, tm, c), lambda b, j: (b, j, 0))
    if tail_feat_major:
        tail = pl.BlockSpec((1, B_COLS, keep), lambda b, j: (b, 0, 0))
        tail_shape = jax.ShapeDtypeStruct((nb, B_COLS, keep), F32)
    else:
        tail = pl.BlockSpec((1, keep, B_COLS), lambda b, j: (b, 0, 0))
        tail_shape = jax.ShapeDtypeStruct((nb, keep, B_COLS), F32)
    const_in = [weights["g_mix"], weights["w_in_mix"], weights["g_q_lora"], weights["w_q_up"], weights["g_kv_lora"],
                weights["gq_row"], weights["gkr_row"], weights["gqb_row"], weights["gkb_row"],
                consts["g_q"], consts["e2_q"], consts["ic_q"], consts["g_b"], consts["e2_b"], consts["ic_b"],
                weights["w_kv_up"], weights["gk_row"]]
    out_shape = [jax.ShapeDtypeStruct((nb, sb, QA_COLS), BF16),
                 jax.ShapeDtypeStruct((nb, sb, KV_LORA), F32),
                 jax.ShapeDtypeStruct((nb, sb, ROPE_DIM), F32),
                 jax.ShapeDtypeStruct((nb, ROPE_DIM, sb), F32),
                 jax.ShapeDtypeStruct((nb, sb, QA_COLS), BF16),
                 jax.ShapeDtypeStruct((nb, sb, QA_COLS), BF16),
                 jax.ShapeDtypeStruct((nb, sb, B_COLS), BF16),
                 jax.ShapeDtypeStruct((nb, sb, B_COLS), BF16),
                 jax.ShapeDtypeStruct((nb, sb, B_COLS), BF16),
                 tail_shape, tail_shape]
    rope_t = pl.BlockSpec((1, ROPE_DIM, tm), lambda b, j: (b, 0, j))
    return pl.pallas_call(
        functools.partial(_mixer_in_kernel, sub, tail_feat_major),
        grid=(nb, nj),
        in_specs=[tok(D_MODEL), ada_spec, tab_spec, tab_spec, tab_spec] + [_const_spec(a.shape) for a in const_in],
        out_specs=[tok(QA_COLS), tok(KV_LORA), tok(ROPE_DIM), rope_t, tok(QA_COLS), tok(QA_COLS),
                   tok(B_COLS), tok(B_COLS), tok(B_COLS), tail, tail],
        out_shape=out_shape,
        compiler_params=_cparams(2),
        name="mixer_in",
    )(x, ada, tables[0], tables[1], tables[2], *const_in)


def _expand_kv(lat, slot, wkv, gk_row, store):
    tm = lat.shape[0]
    kv = _dot(lat.astype(BF16), wkv)
    nope = lax.broadcasted_iota(jnp.int32, (tm, LANES), 1) < NOPE_DIM
    for hd in range(wkv.shape[1] // HEAD_BLOCK):
        blk = kv[:, hd * HEAD_BLOCK:(hd + 1) * HEAD_BLOCK]
        ssq = jnp.sum(jnp.where(nope, blk * blk, 0.0), axis=-1, keepdims=True)
        r = lax.rsqrt(ssq * (1.0 / NOPE_DIM) + EPS)
        store(hd, jnp.where(nope, blk * r * gk_row, slot).astype(BF16), jnp.where(nope, 1.0, blk).astype(BF16))


def _rope_slot_from_feature_major(krt):
    tm = krt.shape[1]
    kr_tok = jnp.concatenate([krt, jnp.zeros((LANES - ROPE_DIM, tm), F32)], axis=0).T
    return pltpu.roll(kr_tok, ROPE_LANE0, 1)


def _normalize_heads(acc0, acc1, lane):
    o0 = acc0 / pltpu.roll(acc0, V_DIM_A, 1)
    o1 = acc1 / pltpu.roll(acc1, V_DIM_A, 1)
    return jnp.where(lane < V_DIM_A, pltpu.roll(o0, V_DIM_A, 1), o1)


def _mla_prompt_kernel(tq, q_ref, k_ref, v_ref, o_ref):
    seq = q_ref.shape[1]
    n_heads = q_ref.shape[2] // HEAD_BLOCK
    lane = lax.broadcasted_iota(jnp.int32, (tq, LANES), 1)
    row_c = lax.broadcasted_iota(jnp.int32, (tq, tq), 0) // CHUNK
    col_c = lax.broadcasted_iota(jnp.int32, (tq, tq), 1) // CHUNK
    diag_ok = row_c >= col_c
    for qi in reversed(range(seq // tq)):
        r0 = qi * tq
        accs = []
        for hd in range(n_heads):
            hb = slice(hd * HEAD_BLOCK, (hd + 1) * HEAD_BLOCK)
            q = q_ref[0, r0:r0 + tq, hb]
            s_d = jnp.where(diag_ok, _dot_nt(q, k_ref[0, r0:r0 + tq, hb]), NEG_INF)
            m = jnp.max(s_d, axis=-1, keepdims=True)
            if qi > 0:
                s_f = _dot_nt(q, k_ref[0, 0:r0, hb])
                m = jnp.maximum(m, jnp.max(s_f, axis=-1, keepdims=True))
            acc = _dot(jnp.exp2(s_d - m).astype(BF16), v_ref[0, r0:r0 + tq, hb])
            if qi > 0:
                acc = acc + _dot(jnp.exp2(s_f - m).astype(BF16), v_ref[0, 0:r0, hb])
            accs.append(acc)
        for pr in range(n_heads // 2):
            o_ref[0, r0:r0 + tq, pr * LANES:(pr + 1) * LANES] = _normalize_heads(
                accs[2 * pr], accs[2 * pr + 1], lane).astype(BF16)


def _mla_prompt_call(q_a, k_a, v_a, tq, heads_per_step):
    nb, seq, _ = q_a.shape
    n_groups = N_HEADS_A // heads_per_step
    group = pl.BlockSpec((1, seq, heads_per_step * HEAD_BLOCK), lambda b, g: (b, 0, g))
    return pl.pallas_call(
        functools.partial(_mla_prompt_kernel, tq),
        grid=(nb, n_groups),
        in_specs=[group, group, group],
        out_specs=pl.BlockSpec((1, seq, heads_per_step * V_DIM_A), lambda b, g: (b, 0, g)),
        out_shape=jax.ShapeDtypeStruct((nb, seq, VA_COLS), BF16),
        compiler_params=_cparams(2),
        name="mla_prompt",
    )(q_a, k_a, v_a)


def _mla_sample_kernel(tc, q_ref, lat_ref, krt_ref, kn_ref, vn_ref, wkv_ref, gk_row_ref, o_ref, kc_ref, vc_ref):
    rows = q_ref.shape[1]
    past = lat_ref.shape[1]
    lane = lax.broadcasted_iota(jnp.int32, (rows, LANES), 1)
    for ci in range(past // tc):
        c0 = ci * tc

        def store(hd, k_blk, v_blk):
            kc_ref[c0:c0 + tc, hd * HEAD_BLOCK:(hd + 1) * HEAD_BLOCK] = k_blk
            vc_ref[c0:c0 + tc, hd * HEAD_BLOCK:(hd + 1) * HEAD_BLOCK] = v_blk

        _expand_kv(lat_ref[0, c0:c0 + tc, :], _rope_slot_from_feature_major(krt_ref[0, :, c0:c0 + tc]),
                   wkv_ref[...], gk_row_ref[...], store)
    accs = []
    for hd in range(N_HEADS_A):
        hb = slice(hd * HEAD_BLOCK, (hd + 1) * HEAD_BLOCK)
        q = q_ref[0, :, hb]
        s_c = _dot_nt(q, kc_ref[:, hb])
        s_n = jnp.where(lane < rows, _dot_nt(q, _pad_rows(kn_ref[0, :, hb], LANES)), NEG_INF)
        m = jnp.maximum(jnp.max(s_c, axis=-1, keepdims=True), jnp.max(s_n, axis=-1, keepdims=True))
        accs.append(_dot(jnp.exp2(s_c - m).astype(BF16), vc_ref[:, hb])
                    + _dot(jnp.exp2(s_n - m).astype(BF16), _pad_rows(vn_ref[0, :, hb], LANES)))
    for pr in range(N_HEADS_A // 2):
        o_ref[0, :, pr * LANES:(pr + 1) * LANES] = _normalize_heads(accs[2 * pr], accs[2 * pr + 1], lane).astype(BF16)


def _mla_sample_call(q_a, latent_cache, k_rope_cache_t, k_new, v_new, weights, tc):
    nb, rows, _ = q_a.shape
    past = latent_cache.shape[1]
    tok = pl.BlockSpec((1, rows, QA_COLS), lambda b: (b, 0, 0))
    return pl.pallas_call(
        functools.partial(_mla_sample_kernel, tc),
        grid=(nb,),
        in_specs=[tok, pl.BlockSpec((1, past, KV_LORA), lambda b: (b, 0, 0)),
                  pl.BlockSpec((1, ROPE_DIM, past), lambda b: (b, 0, 0)), tok, tok,
                  _const_spec(weights["w_kv_up"].shape), _const_spec(weights["gk_row"].shape)],
        out_specs=pl.BlockSpec((1, rows, VA_COLS), lambda b: (b, 0, 0)),
        out_shape=jax.ShapeDtypeStruct((nb, rows, VA_COLS), BF16),
        scratch_shapes=[pltpu.VMEM((past, QA_COLS), BF16), pltpu.VMEM((past, QA_COLS), BF16)],
        compiler_params=_cparams(1),
        name="mla_sample",
    )(q_a, latent_cache, k_rope_cache_t, k_new, v_new, weights["w_kv_up"], weights["gk_row"])


def _toeplitz_bias(g0, rows):
    far = g0[:, 0:1]
    x0 = jnp.broadcast_to(g0[:, 0:LANES], (rows, LANES))
    x1 = jnp.broadcast_to(g0[:, LANES:2 * LANES], (rows, LANES))
    row = lax.broadcasted_iota(jnp.int32, (rows, LANES), 0)
    lane = lax.broadcasted_iota(jnp.int32, (rows, LANES), 1)
    step = 1
    while step < rows:
        r0 = pltpu.roll(x0, step, 1)
        r1 = pltpu.roll(x1, step, 1)
        keep = lane >= step
        take = (row & step) != 0
        x0, x1 = jnp.where(take, jnp.where(keep, r0, r1), x0), jnp.where(take, jnp.where(keep, r1, r0), x1)
        step *= 2
    return jnp.where(lane < row, far, x0), x1, far


def _band_bias_kernel(rb_ref, bias_ref):
    hd = pl.program_id(0)
    tw0, tw1, far = _toeplitz_bias(rb_ref[pl.ds(hd, 1), :], LANES)
    far_blk = jnp.broadcast_to(far, (LANES, LANES))
    n_blk = BAND_WIN // LANES
    row_c = lax.broadcasted_iota(jnp.int32, (LANES, LANES), 0) // CHUNK
    lane = lax.broadcasted_iota(jnp.int32, (LANES, LANES), 1)
    for half in range(BAND_TQ // LANES):
        first_tw = BAND_WINDOW // LANES - 1 + half
        for cb in range(n_blk):
            blk = tw0 if cb == first_tw else (tw1 if cb == first_tw + 1 else far_blk)
            q_c = row_c + half * (LANES // CHUNK)
            col_c = (cb * LANES + lane) // CHUNK
            ok = (col_c >= q_c) & (col_c <= q_c + LEFT_CHUNKS)
            bias_ref[0, half * LANES:(half + 1) * LANES, cb * LANES:(cb + 1) * LANES] = jnp.where(ok, blk * LOG2E, NEG_INF)


def _band_bias_call(rb_rev):
    return pl.pallas_call(
        _band_bias_kernel,
        grid=(N_HEADS_B,),
        in_specs=[_const_spec(rb_rev.shape)],
        out_specs=pl.BlockSpec((1, BAND_TQ, BAND_WIN), lambda h: (h, 0, 0)),
        out_shape=jax.ShapeDtypeStruct((N_HEADS_B, BAND_TQ, BAND_WIN), F32),
        compiler_params=_cparams(1),
        name="band_bias",
    )(rb_rev)


def _split_heads(q, lane):
    zero = jnp.zeros_like(q)
    return jnp.concatenate([jnp.where(lane < HEAD_DIM_B, q, zero), jnp.where(lane >= HEAD_DIM_B, q, zero)], axis=0)


def _band_prompt_kernel(q_ref, k_ref, v_ref, bias_ref, o_ref, vext_ref):
    seq = q_ref.shape[1]
    n_pairs = q_ref.shape[2] // LANES
    lane_q = lax.broadcasted_iota(jnp.int32, (BAND_TQ, LANES), 1)
    for pr in range(n_pairs):
        vext_ref[pr, :, 0:LANES] = v_ref[0, :, pr * LANES:(pr + 1) * LANES]
        vext_ref[pr, :, LANES:2 * LANES] = jnp.ones((seq, LANES), BF16)
    for t in reversed(range(seq // BAND_TQ)):
        t0 = t * BAND_TQ
        k_lo = max(t0 - BAND_WINDOW, 0)
        w = t0 + BAND_TQ - k_lo
        for pr in range(n_pairs):
            cols = slice(pr * LANES, (pr + 1) * LANES)
            q2 = _split_heads(q_ref[0, t0:t0 + BAND_TQ, cols], lane_q)
            bias2 = jnp.concatenate([bias_ref[2 * pr, :, BAND_WIN - w:BAND_WIN],
                                     bias_ref[2 * pr + 1, :, BAND_WIN - w:BAND_WIN]], axis=0)
            s = _dot_nt(q2, k_ref[0, k_lo:t0 + BAND_TQ, cols]) + bias2
            p = jnp.exp2(s - jnp.max(s, axis=-1, keepdims=True))
            acc = _dot(p.astype(BF16), vext_ref[pr, k_lo:t0 + BAND_TQ, :])
            o2 = acc[:, 0:LANES] / acc[:, LANES:2 * LANES]
            o_ref[0, t0:t0 + BAND_TQ, cols] = jnp.where(lane_q < HEAD_DIM_B, o2[0:BAND_TQ],
                                                        o2[BAND_TQ:2 * BAND_TQ]).astype(BF16)


def _band_prompt_call(q_b, k_b, v_b, bias, pairs_per_step):
    nb, seq, _ = q_b.shape
    n_groups = N_HEADS_B // (2 * pairs_per_step)
    spec = pl.BlockSpec((1, seq, pairs_per_step * LANES), lambda b, p: (b, 0, p))
    return pl.pallas_call(
        _band_prompt_kernel,
        grid=(nb, n_groups),
        in_specs=[spec, spec, spec, pl.BlockSpec((2 * pairs_per_step, BAND_TQ, BAND_WIN), lambda b, p: (p, 0, 0),
                                                 pipeline_mode=pl.Buffered(1 if n_groups == 1 else 2))],
        out_specs=spec,
        out_shape=jax.ShapeDtypeStruct((nb, seq, B_COLS), BF16),
        scratch_shapes=[pltpu.VMEM((pairs_per_step, seq, 2 * LANES), BF16)],
        compiler_params=_cparams(2),
        name="band_prompt",
    )(q_b, k_b, v_b, bias)


def _band_sample_kernel(q_ref, kct_ref, vct_ref, kn_ref, vn_ref, bias_ref, o_ref):
    rows = q_ref.shape[1]
    n_cache = kct_ref.shape[2]
    lane = lax.broadcasted_iota(jnp.int32, (rows, LANES), 1)
    lane2 = lax.broadcasted_iota(jnp.int32, (2 * rows, LANES), 1)
    for pair in range(N_HEADS_B // 2):
        cols = slice(pair * LANES, (pair + 1) * LANES)
        q2 = _split_heads(q_ref[0, :, cols], lane)
        kct = kct_ref[0, cols, :].astype(BF16)
        vct = vct_ref[0, cols, :].astype(BF16)
        kn = _pad_rows(kn_ref[0, :, cols], LANES)
        vn = _pad_rows(vn_ref[0, :, cols], LANES)
        bias_c = jnp.concatenate([bias_ref[2 * pair, :, 0:n_cache], bias_ref[2 * pair + 1, :, 0:n_cache]], axis=0)
        bias_n = jnp.concatenate([bias_ref[2 * pair, :, n_cache:n_cache + LANES],
                                  bias_ref[2 * pair + 1, :, n_cache:n_cache + LANES]], axis=0)
        s_c = _dot(q2, kct) + bias_c
        s_n = jnp.where(lane2 < rows, _dot_nt(q2, kn) + bias_n, NEG_INF)
        m = jnp.maximum(jnp.max(s_c, axis=-1, keepdims=True), jnp.max(s_n, axis=-1, keepdims=True))
        p_c = jnp.exp2(s_c - m)
        p_n = jnp.exp2(s_n - m)
        l = jnp.sum(p_c, axis=-1, keepdims=True) + jnp.sum(p_n, axis=-1, keepdims=True)
        o2 = (_dot_nt(p_c.astype(BF16), vct) + _dot(p_n.astype(BF16), vn)) / l
        o_ref[0, :, cols] = jnp.where(lane < HEAD_DIM_B, o2[0:rows], o2[rows:2 * rows]).astype(BF16)


def _band_sample_call(q_b, k_cache_t, v_cache_t, k_new, v_new, bias):
    nb, rows, _ = q_b.shape
    n_cache = k_cache_t.shape[2]
    tok = pl.BlockSpec((1, rows, B_COLS), lambda b: (b, 0, 0))
    cache = pl.BlockSpec((1, B_COLS, n_cache), lambda b: (b, 0, 0))
    return pl.pallas_call(
        _band_sample_kernel,
        grid=(nb,),
        in_specs=[tok, cache, cache, tok, tok, pl.BlockSpec((N_HEADS_B, rows, BAND_WIN), lambda b: (0, 0, 0))],
        out_specs=tok,
        out_shape=jax.ShapeDtypeStruct((nb, rows, B_COLS), BF16),
        compiler_params=_cparams(1),
        name="band_sample",
    )(q_b, k_cache_t, v_cache_t, k_new, v_new, bias)


def _tail_rows(x_ref, oa_ref, ob_ref, ada_ref, y_ref, gmix_ref, wg_ref, woa_ref, wob_ref, wout_ref, gffn_ref,
               wgate_ref, wup_ref, wdown_ref):
    x = x_ref[0]
    ada = lambda k: ada_ref[0, :, k * D_MODEL:(k + 1) * D_MODEL]
    h = x * _row_rms(x, 1.0 / D_MODEL) * gmix_ref[...]
    h = (h * (1.0 + ada(1)) + ada(0)).astype(BF16)
    gates = jax.nn.sigmoid(_dot(h, wg_ref[...]))
    y_a = _dot(oa_ref[0], woa_ref[...])
    y_b = _dot(ob_ref[0], wob_ref[...])
    mixed = gates[:, 0:D_MODEL] * y_a + gates[:, D_MODEL:2 * D_MODEL] * y_b
    x1 = x + ada(2) * _dot(mixed.astype(BF16), wout_ref[...])
    h2 = x1 * _row_rms(x1, 1.0 / D_MODEL) * gffn_ref[...]
    h2 = (h2 * (1.0 + ada(4)) + ada(3)).astype(BF16)
    acc = jnp.zeros_like(x1)
    for c in range(D_FF // FF_CHUNK):
        cols = slice(c * FF_CHUNK, (c + 1) * FF_CHUNK)
        g = _dot(h2, wgate_ref[:, cols])
        u = _dot(h2, wup_ref[:, cols])
        act = (g * jax.nn.sigmoid(g) * u).astype(BF16)
        acc = acc + _dot(act, wdown_ref[cols, :])
    y_ref[0] = x1 + ada(5) * acc


def _tail_kernel(x_ref, oa_ref, ob_ref, ada_ref, xs_ref, oas_ref, obs_ref, adas_ref, *rest):
    weights, (y_ref, ys_ref) = rest[:-2], rest[-2:]
    _tail_rows(x_ref, oa_ref, ob_ref, ada_ref, y_ref, *weights)

    @pl.when((pl.program_id(0) == 0) & (pl.program_id(1) == 0))
    def _():
        _tail_rows(xs_ref, oas_ref, obs_ref, adas_ref, ys_ref, *weights)


def _tail_call(x, o_a, o_b, ada, xs, o_a_s, o_b_s, ada_s, weights, tm):
    nb, sb, _ = x.shape
    rows_s = xs.shape[1]
    tok = lambda c: pl.BlockSpec((1, tm, c), lambda b, j: (b, j, 0))
    whole = lambda c: pl.BlockSpec((1, rows_s, c), lambda b, j: (0, 0, 0))
    const_in = [weights["g_mix"], weights["w_in_gate"], weights["w_o_a"], weights["w_o_b"], weights["w_out"],
                weights["g_ffn"], weights["w_gate"], weights["w_up"], weights["w_down"]]
    return pl.pallas_call(
        _tail_kernel,
        grid=(nb, sb // tm),
        in_specs=[tok(D_MODEL), tok(VA_COLS), tok(B_COLS), pl.BlockSpec((1, 1, 6 * D_MODEL), lambda b, j: (b, 0, 0)),
                  whole(D_MODEL), whole(VA_COLS), whole(B_COLS), whole(6 * D_MODEL)]
        + [_const_spec(a.shape) for a in const_in],
        out_specs=[tok(D_MODEL), whole(D_MODEL)],
        out_shape=[jax.ShapeDtypeStruct((nb, sb, D_MODEL), F32), jax.ShapeDtypeStruct((1, rows_s, D_MODEL), F32)],
        compiler_params=_cparams(2),
        name="tail",
    )(x, o_a, o_b, ada, xs, o_a_s, o_b_s, ada_s, *const_in)


def _group_constants():
    def pack(g, inv_cnt):
        ic = np.ones((1, LANES), np.float32)
        ic[0, :len(inv_cnt)] = inv_cnt
        return jnp.asarray(g, BF16), jnp.asarray(np.concatenate([g.T, g.T], axis=0), BF16), jnp.asarray(ic)

    g_q = np.zeros((QA_COLS, LANES), np.float32)
    for hd in range(N_HEADS_A):
        g_q[hd * HEAD_BLOCK:hd * HEAD_BLOCK + NOPE_DIM, hd] = 1.0
        g_q[hd * HEAD_BLOCK + ROPE_LANE0:hd * HEAD_BLOCK + ROPE_LANE0 + ROPE_DIM, N_HEADS_A + hd] = 1.0
    g_b = np.zeros((B_COLS, LANES), np.float32)
    for hd in range(N_HEADS_B):
        g_b[hd * HEAD_DIM_B:(hd + 1) * HEAD_DIM_B, hd] = 1.0
    c = {}
    c["g_q"], c["e2_q"], c["ic_q"] = pack(g_q, [1.0 / NOPE_DIM] * N_HEADS_A + [1.0 / ROPE_DIM] * N_HEADS_A)
    c["g_b"], c["e2_b"], c["ic_b"] = pack(g_b, [1.0 / HEAD_DIM_B] * N_HEADS_B)
    return c


def _rope_tables(pos):
    inv_freq = ROPE_BASE ** (-jnp.arange(HALF_ROPE, dtype=F32) / HALF_ROPE)
    ang = pos.astype(F32)[:, None] * inv_freq[None, :]
    cos, sin = jnp.cos(ang), jnp.sin(ang)
    n = pos.shape[0]
    ones = jnp.ones((n, ROPE_LANE0), F32)
    zeros = jnp.zeros((n, ROPE_LANE0), F32)
    pad1 = jnp.ones((n, LANES - ROPE_LANE0 - ROPE_DIM), F32)
    pad0 = jnp.zeros((n, LANES - ROPE_LANE0 - ROPE_DIM), F32)
    z16 = jnp.zeros((n, HALF_ROPE), F32)
    cos_t = jnp.concatenate([ones, cos, cos, pad1], axis=1)
    sin_a = jnp.concatenate([zeros, -sin, z16, pad0], axis=1)
    sin_b = jnp.concatenate([zeros, z16, sin, pad0], axis=1)
    return cos_t, sin_a, sin_b


def _layer_weights(l, w_in, g_norm_mix, g_q_lora, w_q_up, g_kv_lora, w_kv_up, g_qn_a, g_kn_a, g_qr_a, g_kr_a,
                   g_q_b, g_k_b, w_o_a, w_o_b, w_out, g_norm_ffn, w_gate, w_up, w_down):
    w = {}
    w["w_in_mix"], w["w_in_gate"] = _w_in_prep_call(jnp.transpose(w_in[l]), 256)
    wq3 = w_q_up[l].reshape(Q_LORA, N_HEADS_A, QK_DIM_A)
    w["w_q_up"] = jnp.pad(wq3, ((0, 0), (0, 0), (0, HEAD_BLOCK - QK_DIM_A))).reshape(Q_LORA, QA_COLS).astype(BF16)
    w["w_kv_up"] = w_kv_up[l].astype(BF16)
    zpad = jnp.zeros((HEAD_BLOCK - QK_DIM_A,), F32)
    w["gq_row"] = jnp.concatenate([g_qn_a[l], g_qr_a[l], zpad]).reshape(1, HEAD_BLOCK)
    w["gk_row"] = jnp.concatenate([g_kn_a[l], jnp.zeros((HEAD_BLOCK - NOPE_DIM,), F32)]).reshape(1, HEAD_BLOCK)
    w["gkr_row"] = jnp.concatenate([jnp.zeros((ROPE_LANE0,), F32), g_kr_a[l], zpad]).reshape(1, LANES)
    w["gqb_row"] = jnp.tile(g_q_b[l], N_HEADS_B).reshape(1, B_COLS)
    w["gkb_row"] = jnp.tile(g_k_b[l], N_HEADS_B).reshape(1, B_COLS)
    w["g_mix"] = g_norm_mix[l].reshape(1, D_MODEL)
    w["g_q_lora"] = g_q_lora[l].reshape(1, Q_LORA)
    w["g_kv_lora"] = g_kv_lora[l].reshape(1, KV_LORA)
    w["g_ffn"] = g_norm_ffn[l].reshape(1, D_MODEL)
    w["w_o_a"] = w_o_a[l].astype(BF16)
    w["w_o_b"] = w_o_b[l].astype(BF16)
    w["w_out"] = w_out[l].astype(BF16)
    w["w_gate"] = w_gate[l].astype(BF16)
    w["w_up"] = w_up[l].astype(BF16)
    w["w_down"] = w_down[l].astype(BF16)
    return w


def kernel(x_prompt, x_sample, c_prompt, c_sample, cache_kv_latent, cache_k_rope, cache_band_k, cache_band_v, w_ada, b_ada, g_norm_mix, w_in, g_q_lora, w_q_up, g_kv_lora, w_kv_up, g_qn_a, g_kn_a, g_qr_a, g_kr_a, g_q_b, g_k_b, rel_bias, w_o_a, w_o_b, w_out, g_norm_ffn, w_gate, w_up, w_down):
    depth = w_in.shape[0]
    nb, seq, _ = x_prompt.shape
    nbs, sd, _ = x_sample.shape
    past = cache_kv_latent.shape[2]
    n_buf = cache_band_k.shape[2]
    keep = min(BAND_WINDOW, seq)
    assert depth == 1 and nbs * sd == LANES and n_buf == BAND_WINDOW and seq % 512 == 0 and past % 512 == 0
    tm = 512
    rows_s = nbs * sd

    consts = _group_constants()
    tab_p = _rope_tables(jnp.arange(seq))
    tab_s = _rope_tables(past + (jnp.arange(rows_s) % sd))
    xs = x_sample.reshape(1, rows_s, D_MODEL)

    l = 0
    wts = _layer_weights(l, w_in, g_norm_mix, g_q_lora, w_q_up, g_kv_lora, w_kv_up, g_qn_a, g_kn_a, g_qr_a, g_kr_a,
                         g_q_b, g_k_b, w_o_a, w_o_b, w_out, g_norm_ffn, w_gate, w_up, w_down)
    band_bias = _band_bias_call(rel_bias[l][:, 2 * REL_CLIP:0:-1])

    ada = _ada_call(jnp.concatenate([c_prompt, c_sample], axis=0), w_ada[l], b_ada[l])
    ada_p = ada[:nb].reshape(nb, 1, 6 * D_MODEL)
    ada_s = jnp.repeat(ada[nb:], sd, axis=0).reshape(1, rows_s, 6 * D_MODEL)

    assert keep == tm
    qa, lat, _, krt, k_a, v_a, qb, kb, vb, kbt_tail, vbt_tail = _mixer_in_call(x_prompt, ada_p, tab_p, consts, wts,
                                                                                2 * tm, tm, True)
    o_a = _mla_prompt_call(qa, k_a, v_a, 512, 4)
    o_b = _band_prompt_call(qb, kb, vb, band_bias, 4)

    qa_s, lat_s, kr_s, _, kn, vn, qb_s, kb_s, vb_s, kb_s32, vb_s32 = _mixer_in_call(xs, ada_s, tab_s, consts, wts,
                                                                                    rows_s, rows_s, False)
    o_a_s = _mla_sample_call(qa_s.reshape(nbs, sd, QA_COLS), cache_kv_latent[l],
                             jnp.transpose(cache_k_rope[l], (0, 2, 1)),
                             kn.reshape(nbs, sd, QA_COLS), vn.reshape(nbs, sd, QA_COLS), wts, tm)
    feat_major = lambda c: jnp.transpose(c, (0, 2, 3, 1)).reshape(nbs, B_COLS, n_buf)
    o_b_s = _band_sample_call(qb_s.reshape(nbs, sd, B_COLS), feat_major(cache_band_k[l]), feat_major(cache_band_v[l]),
                              kb_s.reshape(nbs, sd, B_COLS), vb_s.reshape(nbs, sd, B_COLS), band_bias)

    y_p, y_s = _tail_call(x_prompt, o_a, o_b, ada_p, xs, o_a_s.reshape(1, rows_s, VA_COLS),
                          o_b_s.reshape(1, rows_s, B_COLS), ada_s, wts, tm)

    tok_major = lambda t: jnp.transpose(t.reshape(nb, N_HEADS_B, HEAD_DIM_B, keep), (0, 3, 1, 2))[None]
    return (y_p, y_s.reshape(nbs, sd, D_MODEL),
            lat.reshape(1, nb, seq, KV_LORA), jnp.transpose(krt, (0, 2, 1))[None],
            tok_major(kbt_tail), tok_major(vbt_tail),
            lat_s.reshape(1, nbs, sd, KV_LORA), kr_s.reshape(1, nbs, sd, ROPE_DIM),
            kb_s32.reshape(1, nbs, sd, N_HEADS_B, HEAD_DIM_B), vb_s32.reshape(1, nbs, sd, N_HEADS_B, HEAD_DIM_B))
```

```python
import functools

import jax
import jax.numpy as jnp
import numpy as np
from jax import lax
from jax.experimental import pallas as pl
from jax.experimental.pallas import tpu as pltpu

D_MODEL = 1024
CHUNK = 64
EPS = 1e-6
NEG_INF = -1e30
N_HEADS_A = 8
NOPE_DIM = 64
ROPE_DIM = 32
HALF_ROPE = ROPE_DIM // 2
V_DIM_A = 64
QK_DIM_A = NOPE_DIM + ROPE_DIM
Q_LORA = 384
KV_LORA = 256
ROPE_BASE = 10000.0
N_HEADS_B = 8
HEAD_DIM_B = 64
LEFT_CHUNKS = 8
BAND_WINDOW = LEFT_CHUNKS * CHUNK
REL_CLIP = 128
D_FF = -(-(8 * D_MODEL) // (3 * 256)) * 256
COL_QKV_B = 3 * N_HEADS_B * HEAD_DIM_B
N_GATE_COLS = 2 * D_MODEL

LANES = 128
HEAD_BLOCK = LANES
ROPE_LANE0 = NOPE_DIM
QA_COLS = N_HEADS_A * HEAD_BLOCK
VA_COLS = N_HEADS_A * V_DIM_A
B_COLS = N_HEADS_B * HEAD_DIM_B
MIX_COLS = Q_LORA + KV_LORA + LANES + COL_QKV_B
OFF_CKV = Q_LORA
OFF_KPE = Q_LORA + KV_LORA
OFF_QB = OFF_KPE + LANES
OFF_KB = OFF_QB + B_COLS
OFF_VB = OFF_KB + B_COLS
FF_CHUNK = 256
LOG2E = 1.4426950408889634
BAND_TQ = 256
BAND_WIN = BAND_WINDOW + BAND_TQ
VMEM_LIMIT = 60 * 1024 * 1024

BF16 = jnp.bfloat16
F32 = jnp.float32


def _cparams(n_axes):
    return pltpu.CompilerParams(dimension_semantics=("arbitrary",) * n_axes, vmem_limit_bytes=VMEM_LIMIT)


def _const_spec(shape):
    nd = len(shape)
    return pl.BlockSpec(shape, lambda *_: (0,) * nd, pipeline_mode=pl.Buffered(1))


def _dot(a, b):
    return jnp.dot(a, b, preferred_element_type=F32)


def _dot_nt(a, b):
    return lax.dot_general(a, b, (((1,), (1,)), ((), ())), preferred_element_type=F32)


def _pad_rows(x, rows):
    return jnp.concatenate([x, jnp.zeros((rows - x.shape[0], x.shape[1]), x.dtype)], axis=0)


def _row_rms(x, inv_n):
    return lax.rsqrt(jnp.sum(x * x, axis=-1, keepdims=True) * inv_n + EPS)


def _group_rms(x, g_ref, e2_ref, invcnt_ref):
    s = _dot((x * x).astype(BF16), g_ref[...])
    r = lax.rsqrt(s * invcnt_ref[...] + EPS)
    r_hi = r.astype(BF16)
    r_lo = (r - r_hi.astype(F32)).astype(BF16)
    return _dot(jnp.concatenate([r_hi, r_lo], axis=1), e2_ref[...])


def _rope_block(x, cos_t, sin_a, sin_b):
    return x * cos_t + pltpu.roll(x, LANES - HALF_ROPE, 1) * sin_a + pltpu.roll(x, HALF_ROPE, 1) * sin_b


def _ada_kernel(c_ref, w_ref, b_ref, o_ref):
    c = c_ref[...]
    a = (c * jax.nn.sigmoid(c)).astype(BF16)
    o_ref[...] = _dot(a, w_ref[...].astype(BF16)) + b_ref[...]


def _ada_call(c_all, w_ada, b_ada):
    rows = c_all.shape[0]
    n_out = w_ada.shape[1]
    tn = D_MODEL
    return pl.pallas_call(
        _ada_kernel,
        grid=(n_out // tn,),
        in_specs=[pl.BlockSpec((rows, D_MODEL), lambda n: (0, 0)),
                  pl.BlockSpec((D_MODEL, tn), lambda n: (0, n)),
                  pl.BlockSpec((1, tn), lambda n: (0, n))],
        out_specs=pl.BlockSpec((rows, tn), lambda n: (0, n)),
        out_shape=jax.ShapeDtypeStruct((rows, n_out), F32),
        compiler_params=_cparams(1),
        name="ada",
    )(c_all, w_ada, b_ada.reshape(1, n_out))


def _w_in_prep_kernel(wt_ref, mix_ref, gate_ref):
    tk = wt_ref.shape[1]
    n_mix = OFF_KPE + ROPE_DIM + COL_QKV_B

    def put(dst_ref, col0, src0):
        dst_ref[:, col0:col0 + LANES] = wt_ref[src0:src0 + LANES, :].T.astype(BF16)

    for c in range(OFF_KPE // LANES):
        put(mix_ref, c * LANES, c * LANES)
    kpe = jnp.concatenate([jnp.zeros((ROPE_LANE0, tk), F32), wt_ref[OFF_KPE:OFF_KPE + ROPE_DIM, :],
                           jnp.zeros((LANES - ROPE_LANE0 - ROPE_DIM, tk), F32)], axis=0)
    mix_ref[:, OFF_KPE:OFF_QB] = kpe.T.astype(BF16)
    for c in range(COL_QKV_B // LANES):
        put(mix_ref, OFF_QB + c * LANES, OFF_KPE + ROPE_DIM + c * LANES)
    for c in range(N_GATE_COLS // LANES):
        put(gate_ref, c * LANES, n_mix + c * LANES)


def _w_in_prep_call(w_in_t, tk):
    n_cols, d_in = w_in_t.shape
    return pl.pallas_call(
        _w_in_prep_kernel,
        grid=(d_in // tk,),
        in_specs=[pl.BlockSpec((n_cols, tk), lambda i: (0, i))],
        out_specs=[pl.BlockSpec((tk, MIX_COLS), lambda i: (i, 0)), pl.BlockSpec((tk, N_GATE_COLS), lambda i: (i, 0))],
        out_shape=[jax.ShapeDtypeStruct((d_in, MIX_COLS), BF16), jax.ShapeDtypeStruct((d_in, N_GATE_COLS), BF16)],
        compiler_params=_cparams(1),
        name="w_in_prep",
    )(w_in_t)


def _mixer_in_kernel(sub, tail_feat_major,
                     x_ref, ada_ref, cos_ref, sina_ref, sinb_ref, gmix_ref, w_in_ref, gql_ref, wq_ref, gkv_ref,
                     gq_row_ref, gkr_row_ref, gqb_row_ref, gkb_row_ref,
                     gq_ref, e2q_ref, icq_ref, gb_ref, e2b_ref, icb_ref, wkv_ref, gk_row_ref,
                     qa_ref, lat_ref, kr_ref, krt_ref, ka_ref, va_ref, qb_ref, kb_ref, vb_ref, kbt_ref, vbt_ref):
    n_sub = x_ref.shape[1] // sub
    for si in range(n_sub):
        rows = slice(si * sub, (si + 1) * sub)
        x = x_ref[0, rows, :]
        ada_rows = rows if ada_ref.shape[1] > 1 else slice(None)
        shift = ada_ref[0, ada_rows, 0:D_MODEL]
        scale = ada_ref[0, ada_rows, D_MODEL:2 * D_MODEL]
        h = x * _row_rms(x, 1.0 / D_MODEL) * (gmix_ref[...] * (1.0 + scale)) + shift
        z = _dot(h.astype(BF16), w_in_ref[...])

        cos_t = cos_ref[rows, :]
        sin_a = sina_ref[rows, :]
        sin_b = sinb_ref[rows, :]

        c_q = z[:, 0:Q_LORA]
        cqn = c_q * _row_rms(c_q, 1.0 / Q_LORA) * gql_ref[...]
        q_raw = _dot(cqn.astype(BF16), wq_ref[...])
        r_full = _group_rms(q_raw, gq_ref, e2q_ref, icq_ref)
        q_gain = gq_row_ref[...] * (QK_DIM_A ** -0.5 * LOG2E)
        cos_b, sin_a_b, sin_b_b = cos_t.astype(BF16), sin_a.astype(BF16), sin_b.astype(BF16)
        for hd in range(N_HEADS_A):
            hb = slice(hd * HEAD_BLOCK, (hd + 1) * HEAD_BLOCK)
            qn = (q_raw[:, hb] * r_full[:, hb] * q_gain).astype(BF16)
            qa_ref[0, rows, hb] = _rope_block(qn, cos_b, sin_a_b, sin_b_b)

        c_kv = z[:, OFF_CKV:OFF_CKV + KV_LORA]
        lat = c_kv * _row_rms(c_kv, 1.0 / KV_LORA) * gkv_ref[...]
        lat_ref[0, rows, :] = lat

        kp = z[:, OFF_KPE:OFF_KPE + LANES]
        kr = _rope_block(kp * _row_rms(kp, 1.0 / ROPE_DIM) * gkr_row_ref[...], cos_t, sin_a, sin_b)
        kr_ref[0, rows, :] = pltpu.roll(kr, LANES - ROPE_LANE0, 1)[:, 0:ROPE_DIM]
        krt_ref[0, :, rows] = kr.T[ROPE_LANE0:ROPE_LANE0 + ROPE_DIM, :]

        def store(hd, k_blk, v_blk):
            ka_ref[0, rows, hd * HEAD_BLOCK:(hd + 1) * HEAD_BLOCK] = k_blk
            va_ref[0, rows, hd * HEAD_BLOCK:(hd + 1) * HEAD_BLOCK] = v_blk

        _expand_kv(lat, kr, wkv_ref[...], gk_row_ref[...], store)

        zq = z[:, OFF_QB:OFF_QB + B_COLS]
        qb_ref[0, rows, :] = (zq * _group_rms(zq, gb_ref, e2b_ref, icb_ref)
                              * (gqb_row_ref[...] * (HEAD_DIM_B ** -0.5 * LOG2E))).astype(BF16)
        zk = z[:, OFF_KB:OFF_KB + B_COLS]
        k_b = zk * _group_rms(zk, gb_ref, e2b_ref, icb_ref) * gkb_row_ref[...]
        v_b = z[:, OFF_VB:OFF_VB + B_COLS]
        kb_ref[0, rows, :] = k_b.astype(BF16)
        vb_ref[0, rows, :] = v_b.astype(BF16)

        if si == n_sub - 1:
            @pl.when(pl.program_id(1) == pl.num_programs(1) - 1)
            def _():
                kbt_ref[0] = k_b.T if tail_feat_major else k_b
                vbt_ref[0] = v_b.T if tail_feat_major else v_b


def _mixer_in_call(x, ada, tables, consts, weights, tm, sub, tail_feat_major):
    nb, sb, _ = x.shape
    nj = sb // tm
    keep = sub
    assert tm % sub == 0 and sb % tm == 0
    ada_rows = ada.shape[1]
    if ada_rows == 1:
        ada_spec = pl.BlockSpec((1, 1, 6 * D_MODEL), lambda b, j: (b, 0, 0))
    else:
        ada_spec = pl.BlockSpec((1, tm, 6 * D_MODEL), lambda b, j: (b, j, 0))
    tab_spec = pl.BlockSpec((tm, LANES), lambda b, j: (j, 0))
    tok = lambda c: pl.BlockSpec((1, tm, c), lambda b, j: (b, j, 0))
    if tail_feat_major:
        tail = pl.BlockSpec((1, B_COLS, keep), lambda b, j: (b, 0, 0))
        tail_shape = jax.ShapeDtypeStruct((nb, B_COLS, keep), F32)
    else:
        tail = pl.BlockSpec((1, keep, B_COLS), lambda b, j: (b, 0, 0))
        tail_shape = jax.ShapeDtypeStruct((nb, keep, B_COLS), F32)
    const_in = [weights["g_mix"], weights["w_in_mix"], weights["g_q_lora"], weights["w_q_up"], weights["g_kv_lora"],
                weights["gq_row"], weights["gkr_row"], weights["gqb_row"], weights["gkb_row"],
                consts["g_q"], consts["e2_q"], consts["ic_q"], consts["g_b"], consts["e2_b"], consts["ic_b"],
                weights["w_kv_up"], weights["gk_row"]]
    out_shape = [jax.ShapeDtypeStruct((nb, sb, QA_COLS), BF16),
                 jax.ShapeDtypeStruct((nb, sb, KV_LORA), F32),
                 jax.ShapeDtypeStruct((nb, sb, ROPE_DIM), F32),
                 jax.ShapeDtypeStruct((nb, ROPE_DIM, sb), F32),
                 jax.ShapeDtypeStruct((nb, sb, QA_COLS), BF16),
                 jax.ShapeDtypeStruct((nb, sb, QA_COLS), BF16),
                 jax.ShapeDtypeStruct((nb, sb, B_COLS), BF16),
                 jax.ShapeDtypeStruct((nb, sb, B_COLS), BF16),
                 jax.ShapeDtypeStruct((nb, sb, B_COLS), BF16),
                 tail_shape, tail_shape]
    rope_t = pl.BlockSpec((1, ROPE_DIM, tm), lambda b, j: (b, 0, j))
    return pl.pallas_call(
        functools.partial(_mixer_in_kernel, sub, tail_feat_major),
        grid=(nb, nj),
        in_specs=[tok(D_MODEL), ada_spec, tab_spec, tab_spec, tab_spec] + [_const_spec(a.shape) for a in const_in],
        out_specs=[tok(QA_COLS), tok(KV_LORA), tok(ROPE_DIM), rope_t, tok(QA_COLS), tok(QA_COLS),
                   tok(B_COLS), tok(B_COLS), tok(B_COLS), tail, tail],
        out_shape=out_shape,
        compiler_params=_cparams(2),
        name="mixer_in",
    )(x, ada, tables[0], tables[1], tables[2], *const_in)


def _expand_kv(lat, slot, wkv, gk_row, store):
    tm = lat.shape[0]
    kv = _dot(lat.astype(BF16), wkv)
    nope = lax.broadcasted_iota(jnp.int32, (tm, LANES), 1) < NOPE_DIM
    for hd in range(wkv.shape[1] // HEAD_BLOCK):
        blk = kv[:, hd * HEAD_BLOCK:(hd + 1) * HEAD_BLOCK]
        ssq = jnp.sum(jnp.where(nope, blk * blk, 0.0), axis=-1, keepdims=True)
        r = lax.rsqrt(ssq * (1.0 / NOPE_DIM) + EPS)
        store(hd, jnp.where(nope, blk * r * gk_row, slot).astype(BF16), jnp.where(nope, 1.0, blk).astype(BF16))


def _rope_slot_from_feature_major(krt):
    tm = krt.shape[1]
    kr_tok = jnp.concatenate([krt, jnp.zeros((LANES - ROPE_DIM, tm), F32)], axis=0).T
    return pltpu.roll(kr_tok, ROPE_LANE0, 1)


def _normalize_heads(acc0, acc1, lane):
    o0 = acc0 / pltpu.roll(acc0, V_DIM_A, 1)
    o1 = acc1 / pltpu.roll(acc1, V_DIM_A, 1)
    return jnp.where(lane < V_DIM_A, pltpu.roll(o0, V_DIM_A, 1), o1)


def _mla_prompt_kernel(tq, q_ref, k_ref, v_ref, o_ref):
    seq = q_ref.shape[1]
    n_heads = q_ref.shape[2] // HEAD_BLOCK
    lane = lax.broadcasted_iota(jnp.int32, (tq, LANES), 1)
    row_c = lax.broadcasted_iota(jnp.int32, (tq, tq), 0) // CHUNK
    col_c = lax.broadcasted_iota(jnp.int32, (tq, tq), 1) // CHUNK
    diag_ok = row_c >= col_c
    for qi in reversed(range(seq // tq)):
        r0 = qi * tq
        accs = []
        for hd in range(n_heads):
            hb = slice(hd * HEAD_BLOCK, (hd + 1) * HEAD_BLOCK)
            q = q_ref[0, r0:r0 + tq, hb]
            s_d = jnp.where(diag_ok, _dot_nt(q, k_ref[0, r0:r0 + tq, hb]), NEG_INF)
            m = jnp.max(s_d, axis=-1, keepdims=True)
            if qi > 0:
                s_f = _dot_nt(q, k_ref[0, 0:r0, hb])
                m = jnp.maximum(m, jnp.max(s_f, axis=-1, keepdims=True))
            acc = _dot(jnp.exp2(s_d - m).astype(BF16), v_ref[0, r0:r0 + tq, hb])
            if qi > 0:
                acc = acc + _dot(jnp.exp2(s_f - m).astype(BF16), v_ref[0, 0:r0, hb])
            accs.append(acc)
        for pr in range(n_heads // 2):
            o_ref[0, r0:r0 + tq, pr * LANES:(pr + 1) * LANES] = _normalize_heads(
                accs[2 * pr], accs[2 * pr + 1], lane).astype(BF16)


def _mla_prompt_call(q_a, k_a, v_a, tq, heads_per_step):
    nb, seq, _ = q_a.shape
    n_groups = N_HEADS_A // heads_per_step
    group = pl.BlockSpec((1, seq, heads_per_step * HEAD_BLOCK), lambda b, g: (b, 0, g))
    return pl.pallas_call(
        functools.partial(_mla_prompt_kernel, tq),
        grid=(nb, n_groups),
        in_specs=[group, group, group],
        out_specs=pl.BlockSpec((1, seq, heads_per_step * V_DIM_A), lambda b, g: (b, 0, g)),
        out_shape=jax.ShapeDtypeStruct((nb, seq, VA_COLS), BF16),
        compiler_params=_cparams(2),
        name="mla_prompt",
    )(q_a, k_a, v_a)


def _mla_sample_kernel(tc, q_ref, lat_ref, krt_ref, kn_ref, vn_ref, wkv_ref, gk_row_ref, o_ref, kc_ref, vc_ref):
    rows = q_ref.shape[1]
    past = lat_ref.shape[1]
    lane = lax.broadcasted_iota(jnp.int32, (rows, LANES), 1)
    for ci in range(past // tc):
        c0 = ci * tc

        def store(hd, k_blk, v_blk):
            kc_ref[c0:c0 + tc, hd * HEAD_BLOCK:(hd + 1) * HEAD_BLOCK] = k_blk
            vc_ref[c0:c0 + tc, hd * HEAD_BLOCK:(hd + 1) * HEAD_BLOCK] = v_blk

        _expand_kv(lat_ref[0, c0:c0 + tc, :], _rope_slot_from_feature_major(krt_ref[0, :, c0:c0 + tc]),
                   wkv_ref[...], gk_row_ref[...], store)
    accs = []
    for hd in range(N_HEADS_A):
        hb = slice(hd * HEAD_BLOCK, (hd + 1) * HEAD_BLOCK)
        q = q_ref[0, :, hb]
        s_c = _dot_nt(q, kc_ref[:, hb])
        s_n = jnp.where(lane < rows, _dot_nt(q, _pad_rows(kn_ref[0, :, hb], LANES)), NEG_INF)
        m = jnp.maximum(jnp.max(s_c, axis=-1, keepdims=True), jnp.max(s_n, axis=-1, keepdims=True))
        accs.append(_dot(jnp.exp2(s_c - m).astype(BF16), vc_ref[:, hb])
                    + _dot(jnp.exp2(s_n - m).astype(BF16), _pad_rows(vn_ref[0, :, hb], LANES)))
    for pr in range(N_HEADS_A // 2):
        o_ref[0, :, pr * LANES:(pr + 1) * LANES] = _normalize_heads(accs[2 * pr], accs[2 * pr + 1], lane).astype(BF16)


def _mla_sample_call(q_a, latent_cache, k_rope_cache_t, k_new, v_new, weights, tc):
    nb, rows, _ = q_a.shape
    past = latent_cache.shape[1]
    tok = pl.BlockSpec((1, rows, QA_COLS), lambda b: (b, 0, 0))
    return pl.pallas_call(
        functools.partial(_mla_sample_kernel, tc),
        grid=(nb,),
        in_specs=[tok, pl.BlockSpec((1, past, KV_LORA), lambda b: (b, 0, 0)),
                  pl.BlockSpec((1, ROPE_DIM, past), lambda b: (b, 0, 0)), tok, tok,
                  _const_spec(weights["w_kv_up"].shape), _const_spec(weights["gk_row"].shape)],
        out_specs=pl.BlockSpec((1, rows, VA_COLS), lambda b: (b, 0, 0)),
        out_shape=jax.ShapeDtypeStruct((nb, rows, VA_COLS), BF16),
        scratch_shapes=[pltpu.VMEM((past, QA_COLS), BF16), pltpu.VMEM((past, QA_COLS), BF16)],
        compiler_params=_cparams(1),
        name="mla_sample",
    )(q_a, latent_cache, k_rope_cache_t, k_new, v_new, weights["w_kv_up"], weights["gk_row"])


def _toeplitz_bias(g0, rows):
    far = g0[:, 0:1]
    x0 = jnp.broadcast_to(g0[:, 0:LANES], (rows, LANES))
    x1 = jnp.broadcast_to(g0[:, LANES:2 * LANES], (rows, LANES))
    row = lax.broadcasted_iota(jnp.int32, (rows, LANES), 0)
    lane = lax.broadcasted_iota(jnp.int32, (rows, LANES), 1)
    step = 1
    while step < rows:
        r0 = pltpu.roll(x0, step, 1)
        r1 = pltpu.roll(x1, step, 1)
        keep = lane >= step
        take = (row & step) != 0
        x0, x1 = jnp.where(take, jnp.where(keep, r0, r1), x0), jnp.where(take, jnp.where(keep, r1, r0), x1)
        step *= 2
    return jnp.where(lane < row, far, x0), x1, far


def _band_bias_kernel(rb_ref, bias_ref):
    hd = pl.program_id(0)
    tw0, tw1, far = _toeplitz_bias(rb_ref[pl.ds(hd, 1), :], LANES)
    far_blk = jnp.broadcast_to(far, (LANES, LANES))
    n_blk = BAND_WIN // LANES
    row_c = lax.broadcasted_iota(jnp.int32, (LANES, LANES), 0) // CHUNK
    lane = lax.broadcasted_iota(jnp.int32, (LANES, LANES), 1)
    for half in range(BAND_TQ // LANES):
        first_tw = BAND_WINDOW // LANES - 1 + half
        for cb in range(n_blk):
            blk = tw0 if cb == first_tw else (tw1 if cb == first_tw + 1 else far_blk)
            q_c = row_c + half * (LANES // CHUNK)
            col_c = (cb * LANES + lane) // CHUNK
            ok = (col_c >= q_c) & (col_c <= q_c + LEFT_CHUNKS)
            bias_ref[0, half * LANES:(half + 1) * LANES, cb * LANES:(cb + 1) * LANES] = jnp.where(ok, blk * LOG2E, NEG_INF)


def _band_bias_call(rb_rev):
    return pl.pallas_call(
        _band_bias_kernel,
        grid=(N_HEADS_B,),
        in_specs=[_const_spec(rb_rev.shape)],
        out_specs=pl.BlockSpec((1, BAND_TQ, BAND_WIN), lambda h: (h, 0, 0)),
        out_shape=jax.ShapeDtypeStruct((N_HEADS_B, BAND_TQ, BAND_WIN), F32),
        compiler_params=_cparams(1),
        name="band_bias",
    )(rb_rev)


def _split_heads(q, lane):
    zero = jnp.zeros_like(q)
    return jnp.concatenate([jnp.where(lane < HEAD_DIM_B, q, zero), jnp.where(lane >= HEAD_DIM_B, q, zero)], axis=0)


def _band_prompt_kernel(q_ref, k_ref, v_ref, bias_ref, o_ref, vext_ref):
    seq = q_ref.shape[1]
    n_pairs = q_ref.shape[2] // LANES
    lane_q = lax.broadcasted_iota(jnp.int32, (BAND_TQ, LANES), 1)
    for pr in range(n_pairs):
        vext_ref[pr, :, 0:LANES] = v_ref[0, :, pr * LANES:(pr + 1) * LANES]
        vext_ref[pr, :, LANES:2 * LANES] = jnp.ones((seq, LANES), BF16)
    for t in reversed(range(seq // BAND_TQ)):
        t0 = t * BAND_TQ
        k_lo = max(t0 - BAND_WINDOW, 0)
        w = t0 + BAND_TQ - k_lo
        for pr in range(n_pairs):
            cols = slice(pr * LANES, (pr + 1) * LANES)
            q2 = _split_heads(q_ref[0, t0:t0 + BAND_TQ, cols], lane_q)
            bias2 = jnp.concatenate([bias_ref[2 * pr, :, BAND_WIN - w:BAND_WIN],
                                     bias_ref[2 * pr + 1, :, BAND_WIN - w:BAND_WIN]], axis=0)
            s = _dot_nt(q2, k_ref[0, k_lo:t0 + BAND_TQ, cols]) + bias2
            p = jnp.exp2(s - jnp.max(s, axis=-1, keepdims=True))
            acc = _dot(p.astype(BF16), vext_ref[pr, k_lo:t0 + BAND_TQ, :])
            o2 = acc[:, 0:LANES] / acc[:, LANES:2 * LANES]
            o_ref[0, t0:t0 + BAND_TQ, cols] = jnp.where(lane_q < HEAD_DIM_B, o2[0:BAND_TQ],
                                                        o2[BAND_TQ:2 * BAND_TQ]).astype(BF16)


def _band_prompt_call(q_b, k_b, v_b, bias, pairs_per_step):
    nb, seq, _ = q_b.shape
    n_groups = N_HEADS_B // (2 * pairs_per_step)
    spec = pl.BlockSpec((1, seq, pairs_per_step * LANES), lambda b, p: (b, 0, p))
    return pl.pallas_call(
        _band_prompt_kernel,
        grid=(nb, n_groups),
        in_specs=[spec, spec, spec, pl.BlockSpec((2 * pairs_per_step, BAND_TQ, BAND_WIN), lambda b, p: (p, 0, 0),
                                                 pipeline_mode=pl.Buffered(1 if n_groups == 1 else 2))],
        out_specs=spec,
        out_shape=jax.ShapeDtypeStruct((nb, seq, B_COLS), BF16),
        scratch_shapes=[pltpu.VMEM((pairs_per_step, seq, 2 * LANES), BF16)],
        compiler_params=_cparams(2),
        name="band_prompt",
    )(q_b, k_b, v_b, bias)


def _band_sample_kernel(q_ref, kct_ref, vct_ref, kn_ref, vn_ref, bias_ref, o_ref):
    rows = q_ref.shape[1]
    n_cache = kct_ref.shape[2]
    lane = lax.broadcasted_iota(jnp.int32, (rows, LANES), 1)
    lane2 = lax.broadcasted_iota(jnp.int32, (2 * rows, LANES), 1)
    for pair in range(N_HEADS_B // 2):
        cols = slice(pair * LANES, (pair + 1) * LANES)
        q2 = _split_heads(q_ref[0, :, cols], lane)
        kct = kct_ref[0, cols, :].astype(BF16)
        vct = vct_ref[0, cols, :].astype(BF16)
        kn = _pad_rows(kn_ref[0, :, cols], LANES)
        vn = _pad_rows(vn_ref[0, :, cols], LANES)
        bias_c = jnp.concatenate([bias_ref[2 * pair, :, 0:n_cache], bias_ref[2 * pair + 1, :, 0:n_cache]], axis=0)
        bias_n = jnp.concatenate([bias_ref[2 * pair, :, n_cache:n_cache + LANES],
                                  bias_ref[2 * pair + 1, :, n_cache:n_cache + LANES]], axis=0)
        s_c = _dot(q2, kct) + bias_c
        s_n = jnp.where(lane2 < rows, _dot_nt(q2, kn) + bias_n, NEG_INF)
        m = jnp.maximum(jnp.max(s_c, axis=-1, keepdims=True), jnp.max(s_n, axis=-1, keepdims=True))
        p_c = jnp.exp2(s_c - m)
        p_n = jnp.exp2(s_n - m)
        l = jnp.sum(p_c, axis=-1, keepdims=True) + jnp.sum(p_n, axis=-1, keepdims=True)
        o2 = (_dot_nt(p_c.astype(BF16), vct) + _dot(p_n.astype(BF16), vn)) / l
        o_ref[0, :, cols] = jnp.where(lane < HEAD_DIM_B, o2[0:rows], o2[rows:2 * rows]).astype(BF16)


def _band_sample_call(q_b, k_cache_t, v_cache_t, k_new, v_new, bias):
    nb, rows, _ = q_b.shape
    n_cache = k_cache_t.shape[2]
    tok = pl.BlockSpec((1, rows, B_COLS), lambda b: (b, 0, 0))
    cache = pl.BlockSpec((1, B_COLS, n_cache), lambda b: (b, 0, 0))
    return pl.pallas_call(
        _band_sample_kernel,
        grid=(nb,),
        in_specs=[tok, cache, cache, tok, tok, pl.BlockSpec((N_HEADS_B, rows, BAND_WIN), lambda b: (0, 0, 0))],
        out_specs=tok,
        out_shape=jax.ShapeDtypeStruct((nb, rows, B_COLS), BF16),
        compiler_params=_cparams(1),
        name="band_sample",
    )(q_b, k_cache_t, v_cache_t, k_new, v_new, bias)


def _tail_rows(rows, x_ref, oa_ref, ob_ref, ada_ref, y_ref, gmix_ref, wg_ref, woa_ref, wob_ref, wout_ref, gffn_ref,
               wgate_ref, wup_ref, wdown_ref):
    x = x_ref[0, rows, :]
    ada_rows = rows if ada_ref.shape[1] > 1 else slice(None)
    ada = lambda k: ada_ref[0, ada_rows, k * D_MODEL:(k + 1) * D_MODEL]
    h = x * _row_rms(x, 1.0 / D_MODEL) * gmix_ref[...]
    h = (h * (1.0 + ada(1)) + ada(0)).astype(BF16)
    gates = jax.nn.sigmoid(_dot(h, wg_ref[...]))
    y_a = _dot(oa_ref[0, rows, :], woa_ref[...])
    y_b = _dot(ob_ref[0, rows, :], wob_ref[...])
    mixed = gates[:, 0:D_MODEL] * y_a + gates[:, D_MODEL:2 * D_MODEL] * y_b
    x1 = x + ada(2) * _dot(mixed.astype(BF16), wout_ref[...])
    h2 = x1 * _row_rms(x1, 1.0 / D_MODEL) * gffn_ref[...]
    h2 = (h2 * (1.0 + ada(4)) + ada(3)).astype(BF16)
    acc = jnp.zeros_like(x1)
    for c in range(D_FF // FF_CHUNK):
        cols = slice(c * FF_CHUNK, (c + 1) * FF_CHUNK)
        g = _dot(h2, wgate_ref[:, cols])
        u = _dot(h2, wup_ref[:, cols])
        act = (g * jax.nn.sigmoid(g) * u).astype(BF16)
        acc = acc + _dot(act, wdown_ref[cols, :])
    y_ref[0, rows, :] = x1 + ada(5) * acc


def _tail_kernel(sub, x_ref, oa_ref, ob_ref, ada_ref, xs_ref, oas_ref, obs_ref, adas_ref, *rest):
    weights, (y_ref, ys_ref) = rest[:-2], rest[-2:]
    for si in range(x_ref.shape[1] // sub):
        _tail_rows(slice(si * sub, (si + 1) * sub), x_ref, oa_ref, ob_ref, ada_ref, y_ref, *weights)

    @pl.when((pl.program_id(0) == 0) & (pl.program_id(1) == 0))
    def _():
        _tail_rows(slice(None), xs_ref, oas_ref, obs_ref, adas_ref, ys_ref, *weights)


def _tail_call(x, o_a, o_b, ada, xs, o_a_s, o_b_s, ada_s, weights, tm, sub):
    nb, sb, _ = x.shape
    rows_s = xs.shape[1]
    tok = lambda c: pl.BlockSpec((1, tm, c), lambda b, j: (b, j, 0))
    whole = lambda c: pl.BlockSpec((1, rows_s, c), lambda b, j: (0, 0, 0))
    const_in = [weights["g_mix"], weights["w_in_gate"], weights["w_o_a"], weights["w_o_b"], weights["w_out"],
                weights["g_ffn"], weights["w_gate"], weights["w_up"], weights["w_down"]]
    return pl.pallas_call(
        functools.partial(_tail_kernel, sub),
        grid=(nb, sb // tm),
        in_specs=[tok(D_MODEL), tok(VA_COLS), tok(B_COLS), pl.BlockSpec((1, 1, 6 * D_MODEL), lambda b, j: (b, 0, 0)),
                  whole(D_MODEL), whole(VA_COLS), whole(B_COLS), whole(6 * D_MODEL)]
        + [_const_spec(a.shape) for a in const_in],
        out_specs=[tok(D_MODEL), whole(D_MODEL)],
        out_shape=[jax.ShapeDtypeStruct((nb, sb, D_MODEL), F32), jax.ShapeDtypeStruct((1, rows_s, D_MODEL), F32)],
        compiler_params=_cparams(2),
        name="tail",
    )(x, o_a, o_b, ada, xs, o_a_s, o_b_s, ada_s, *const_in)


def _group_constants():
    def pack(g, inv_cnt):
        ic = np.ones((1, LANES), np.float32)
        ic[0, :len(inv_cnt)] = inv_cnt
        return jnp.asarray(g, BF16), jnp.asarray(np.concatenate([g.T, g.T], axis=0), BF16), jnp.asarray(ic)

    g_q = np.zeros((QA_COLS, LANES), np.float32)
    for hd in range(N_HEADS_A):
        g_q[hd * HEAD_BLOCK:hd * HEAD_BLOCK + NOPE_DIM, hd] = 1.0
        g_q[hd * HEAD_BLOCK + ROPE_LANE0:hd * HEAD_BLOCK + ROPE_LANE0 + ROPE_DIM, N_HEADS_A + hd] = 1.0
    g_b = np.zeros((B_COLS, LANES), np.float32)
    for hd in range(N_HEADS_B):
        g_b[hd * HEAD_DIM_B:(hd + 1) * HEAD_DIM_B, hd] = 1.0
    c = {}
    c["g_q"], c["e2_q"], c["ic_q"] = pack(g_q, [1.0 / NOPE_DIM] * N_HEADS_A + [1.0 / ROPE_DIM] * N_HEADS_A)
    c["g_b"], c["e2_b"], c["ic_b"] = pack(g_b, [1.0 / HEAD_DIM_B] * N_HEADS_B)
    return c


def _rope_tables(pos):
    inv_freq = ROPE_BASE ** (-jnp.arange(HALF_ROPE, dtype=F32) / HALF_ROPE)
    ang = pos.astype(F32)[:, None] * inv_freq[None, :]
    cos, sin = jnp.cos(ang), jnp.sin(ang)
    n = pos.shape[0]
    ones = jnp.ones((n, ROPE_LANE0), F32)
    zeros = jnp.zeros((n, ROPE_LANE0), F32)
    pad1 = jnp.ones((n, LANES - ROPE_LANE0 - ROPE_DIM), F32)
    pad0 = jnp.zeros((n, LANES - ROPE_LANE0 - ROPE_DIM), F32)
    z16 = jnp.zeros((n, HALF_ROPE), F32)
    cos_t = jnp.concatenate([ones, cos, cos, pad1], axis=1)
    sin_a = jnp.concatenate([zeros, -sin, z16, pad0], axis=1)
    sin_b = jnp.concatenate([zeros, z16, sin, pad0], axis=1)
    return cos_t, sin_a, sin_b


def _layer_weights(l, w_in, g_norm_mix, g_q_lora, w_q_up, g_kv_lora, w_kv_up, g_qn_a, g_kn_a, g_qr_a, g_kr_a,
                   g_q_b, g_k_b, w_o_a, w_o_b, w_out, g_norm_ffn, w_gate, w_up, w_down):
    w = {}
    w["w_in_mix"], w["w_in_gate"] = _w_in_prep_call(jnp.transpose(w_in[l]), 256)
    wq3 = w_q_up[l].reshape(Q_LORA, N_HEADS_A, QK_DIM_A)
    w["w_q_up"] = jnp.pad(wq3, ((0, 0), (0, 0), (0, HEAD_BLOCK - QK_DIM_A))).reshape(Q_LORA, QA_COLS).astype(BF16)
    w["w_kv_up"] = w_kv_up[l].astype(BF16)
    zpad = jnp.zeros((HEAD_BLOCK - QK_DIM_A,), F32)
    w["gq_row"] = jnp.concatenate([g_qn_a[l], g_qr_a[l], zpad]).reshape(1, HEAD_BLOCK)
    w["gk_row"] = jnp.concatenate([g_kn_a[l], jnp.zeros((HEAD_BLOCK - NOPE_DIM,), F32)]).reshape(1, HEAD_BLOCK)
    w["gkr_row"] = jnp.concatenate([jnp.zeros((ROPE_LANE0,), F32), g_kr_a[l], zpad]).reshape(1, LANES)
    w["gqb_row"] = jnp.tile(g_q_b[l], N_HEADS_B).reshape(1, B_COLS)
    w["gkb_row"] = jnp.tile(g_k_b[l], N_HEADS_B).reshape(1, B_COLS)
    w["g_mix"] = g_norm_mix[l].reshape(1, D_MODEL)
    w["g_q_lora"] = g_q_lora[l].reshape(1, Q_LORA)
    w["g_kv_lora"] = g_kv_lora[l].reshape(1, KV_LORA)
    w["g_ffn"] = g_norm_ffn[l].reshape(1, D_MODEL)
    w["w_o_a"] = w_o_a[l].astype(BF16)
    w["w_o_b"] = w_o_b[l].astype(BF16)
    w["w_out"] = w_out[l].astype(BF16)
    w["w_gate"] = w_gate[l].astype(BF16)
    w["w_up"] = w_up[l].astype(BF16)
    w["w_down"] = w_down[l].astype(BF16)
    return w


def kernel(x_prompt, x_sample, c_prompt, c_sample, cache_kv_latent, cache_k_rope, cache_band_k, cache_band_v, w_ada, b_ada, g_norm_mix, w_in, g_q_lora, w_q_up, g_kv_lora, w_kv_up, g_qn_a, g_kn_a, g_qr_a, g_kr_a, g_q_b, g_k_b, rel_bias, w_o_a, w_o_b, w_out, g_norm_ffn, w_gate, w_up, w_down):
    depth = w_in.shape[0]
    nb, seq, _ = x_prompt.shape
    nbs, sd, _ = x_sample.shape
    past = cache_kv_latent.shape[2]
    n_buf = cache_band_k.shape[2]
    keep = min(BAND_WINDOW, seq)
    assert depth == 1 and nbs * sd == LANES and n_buf == BAND_WINDOW and seq % 512 == 0 and past % 512 == 0
    tm = 512
    rows_s = nbs * sd

    consts = _group_constants()
    tab_p = _rope_tables(jnp.arange(seq))
    tab_s = _rope_tables(past + (jnp.arange(rows_s) % sd))
    xs = x_sample.reshape(1, rows_s, D_MODEL)

    l = 0
    wts = _layer_weights(l, w_in, g_norm_mix, g_q_lora, w_q_up, g_kv_lora, w_kv_up, g_qn_a, g_kn_a, g_qr_a, g_kr_a,
                         g_q_b, g_k_b, w_o_a, w_o_b, w_out, g_norm_ffn, w_gate, w_up, w_down)
    band_bias = _band_bias_call(rel_bias[l][:, 2 * REL_CLIP:0:-1])

    ada = _ada_call(jnp.concatenate([c_prompt, c_sample], axis=0), w_ada[l], b_ada[l])
    ada_p = ada[:nb].reshape(nb, 1, 6 * D_MODEL)
    ada_s = jnp.repeat(ada[nb:], sd, axis=0).reshape(1, rows_s, 6 * D_MODEL)

    assert keep == tm
    qa, lat, _, krt, k_a, v_a, qb, kb, vb, kbt_tail, vbt_tail = _mixer_in_call(x_prompt, ada_p, tab_p, consts, wts,
                                                                                2 * tm, tm, True)
    o_a = _mla_prompt_call(qa, k_a, v_a, 512, 4)
    o_b = _band_prompt_call(qb, kb, vb, band_bias, 4)

    qa_s, lat_s, kr_s, _, kn, vn, qb_s, kb_s, vb_s, kb_s32, vb_s32 = _mixer_in_call(xs, ada_s, tab_s, consts, wts,
                                                                                    rows_s, rows_s, False)
    o_a_s = _mla_sample_call(qa_s.reshape(nbs, sd, QA_COLS), cache_kv_latent[l],
                             jnp.transpose(cache_k_rope[l], (0, 2, 1)),
                             kn.reshape(nbs, sd, QA_COLS), vn.reshape(nbs, sd, QA_COLS), wts, tm)
    feat_major = lambda c: jnp.transpose(c, (0, 2, 3, 1)).reshape(nbs, B_COLS, n_buf)
    o_b_s = _band_sample_call(qb_s.reshape(nbs, sd, B_COLS), feat_major(cache_band_k[l]), feat_major(cache_band_v[l]),
                              kb_s.reshape(nbs, sd, B_COLS), vb_s.reshape(nbs, sd, B_COLS), band_bias)

    y_p, y_s = _tail_call(x_prompt, o_a, o_b, ada_p, xs, o_a_s.reshape(1, rows_s, VA_COLS),
                          o_b_s.reshape(1, rows_s, B_COLS), ada_s, wts, 2 * tm, tm)

    tok_major = lambda t: jnp.transpose(t.reshape(nb, N_HEADS_B, HEAD_DIM_B, keep), (0, 3, 1, 2))[None]
    return (y_p, y_s.reshape(nbs, sd, D_MODEL),
            lat.reshape(1, nb, seq, KV_LORA), jnp.transpose(krt, (0, 2, 1))[None],
            tok_major(kbt_tail), tok_major(vbt_tail),
            lat_s.reshape(1, nbs, sd, KV_LORA), kr_s.reshape(1, nbs, sd, ROPE_DIM),
            kb_s32.reshape(1, nbs, sd, N_HEADS_B, HEAD_DIM_B), vb_s32.reshape(1, nbs, sd, N_HEADS_B, HEAD_DIM_B))
```

```python
import functools

import jax
import jax.numpy as jnp
import numpy as np
from jax import lax
from jax.experimental import pallas as pl
from jax.experimental.pallas import tpu as pltpu

D_MODEL = 1024
CHUNK = 64
EPS = 1e-6
NEG_INF = -1e30
N_HEADS_A = 8
NOPE_DIM = 64
ROPE_DIM = 32
HALF_ROPE = ROPE_DIM // 2
V_DIM_A = 64
QK_DIM_A = NOPE_DIM + ROPE_DIM
Q_LORA = 384
KV_LORA = 256
ROPE_BASE = 10000.0
N_HEADS_B = 8
HEAD_DIM_B = 64
LEFT_CHUNKS = 8
BAND_WINDOW = LEFT_CHUNKS * CHUNK
REL_CLIP = 128
D_FF = -(-(8 * D_MODEL) // (3 * 256)) * 256
COL_QKV_B = 3 * N_HEADS_B * HEAD_DIM_B
N_GATE_COLS = 2 * D_MODEL

LANES = 128
HEAD_BLOCK = LANES
ROPE_LANE0 = NOPE_DIM
QA_COLS = N_HEADS_A * HEAD_BLOCK
VA_COLS = N_HEADS_A * V_DIM_A
B_COLS = N_HEADS_B * HEAD_DIM_B
MIX_COLS = Q_LORA + KV_LORA + LANES + COL_QKV_B
OFF_CKV = Q_LORA
OFF_KPE = Q_LORA + KV_LORA
OFF_QB = OFF_KPE + LANES
OFF_KB = OFF_QB + B_COLS
OFF_VB = OFF_KB + B_COLS
FF_CHUNK = 256
LOG2E = 1.4426950408889634
BAND_TQ = 256
BAND_WIN = BAND_WINDOW + BAND_TQ
VMEM_LIMIT = 60 * 1024 * 1024

BF16 = jnp.bfloat16
F32 = jnp.float32


def _cparams(n_axes):
    return pltpu.CompilerParams(dimension_semantics=("arbitrary",) * n_axes, vmem_limit_bytes=VMEM_LIMIT)


def _const_spec(shape):
    nd = len(shape)
    return pl.BlockSpec(shape, lambda *_: (0,) * nd, pipeline_mode=pl.Buffered(1))


def _dot(a, b):
    return jnp.dot(a, b, preferred_element_type=F32)


def _dot_nt(a, b):
    return lax.dot_general(a, b, (((1,), (1,)), ((), ())), preferred_element_type=F32)


def _pad_rows(x, rows):
    return jnp.concatenate([x, jnp.zeros((rows - x.shape[0], x.shape[1]), x.dtype)], axis=0)


def _row_rms(x, inv_n):
    return lax.rsqrt(jnp.sum(x * x, axis=-1, keepdims=True) * inv_n + EPS)


def _group_rms(x, g_ref, e2_ref, invcnt_ref):
    s = _dot((x * x).astype(BF16), g_ref[...])
    r = lax.rsqrt(s * invcnt_ref[...] + EPS)
    r_hi = r.astype(BF16)
    r_lo = (r - r_hi.astype(F32)).astype(BF16)
    return _dot(jnp.concatenate([r_hi, r_lo], axis=1), e2_ref[...])


def _rope_block(x, cos_t, sin_a, sin_b):
    return x * cos_t + pltpu.roll(x, LANES - HALF_ROPE, 1) * sin_a + pltpu.roll(x, HALF_ROPE, 1) * sin_b


def _ada_kernel(c_ref, w_ref, b_ref, o_ref):
    c = c_ref[...]
    a = (c * jax.nn.sigmoid(c)).astype(BF16)
    o_ref[...] = _dot(a, w_ref[...].astype(BF16)) + b_ref[...]


def _ada_call(c_all, w_ada, b_ada):
    rows = c_all.shape[0]
    n_out = w_ada.shape[1]
    tn = D_MODEL
    return pl.pallas_call(
        _ada_kernel,
        grid=(n_out // tn,),
        in_specs=[pl.BlockSpec((rows, D_MODEL), lambda n: (0, 0)),
                  pl.BlockSpec((D_MODEL, tn), lambda n: (0, n)),
                  pl.BlockSpec((1, tn), lambda n: (0, n))],
        out_specs=pl.BlockSpec((rows, tn), lambda n: (0, n)),
        out_shape=jax.ShapeDtypeStruct((rows, n_out), F32),
        compiler_params=_cparams(1),
        name="ada",
    )(c_all, w_ada, b_ada.reshape(1, n_out))


def _w_in_prep_kernel(wt_ref, mix_ref, gate_ref):
    tk = wt_ref.shape[1]
    n_mix = OFF_KPE + ROPE_DIM + COL_QKV_B

    def put(dst_ref, col0, src0):
        dst_ref[:, col0:col0 + LANES] = wt_ref[src0:src0 + LANES, :].T.astype(BF16)

    for c in range(OFF_KPE // LANES):
        put(mix_ref, c * LANES, c * LANES)
    kpe = jnp.concatenate([jnp.zeros((ROPE_LANE0, tk), F32), wt_ref[OFF_KPE:OFF_KPE + ROPE_DIM, :],
                           jnp.zeros((LANES - ROPE_LANE0 - ROPE_DIM, tk), F32)], axis=0)
    mix_ref[:, OFF_KPE:OFF_QB] = kpe.T.astype(BF16)
    for c in range(COL_QKV_B // LANES):
        put(mix_ref, OFF_QB + c * LANES, OFF_KPE + ROPE_DIM + c * LANES)
    for c in range(N_GATE_COLS // LANES):
        put(gate_ref, c * LANES, n_mix + c * LANES)


def _w_in_prep_call(w_in_t, tk):
    n_cols, d_in = w_in_t.shape
    return pl.pallas_call(
        _w_in_prep_kernel,
        grid=(d_in // tk,),
        in_specs=[pl.BlockSpec((n_cols, tk), lambda i: (0, i))],
        out_specs=[pl.BlockSpec((tk, MIX_COLS), lambda i: (i, 0)), pl.BlockSpec((tk, N_GATE_COLS), lambda i: (i, 0))],
        out_shape=[jax.ShapeDtypeStruct((d_in, MIX_COLS), BF16), jax.ShapeDtypeStruct((d_in, N_GATE_COLS), BF16)],
        compiler_params=_cparams(1),
        name="w_in_prep",
    )(w_in_t)


def _mixer_in_kernel(sub, tail_feat_major,
                     x_ref, ada_ref, cos_ref, sina_ref, sinb_ref, gmix_ref, w_in_ref, gql_ref, wq_ref, gkv_ref,
                     gq_row_ref, gkr_row_ref, gqb_row_ref, gkb_row_ref,
                     gq_ref, e2q_ref, icq_ref, gb_ref, e2b_ref, icb_ref, wkv_ref, gk_row_ref,
                     qa_ref, lat_ref, kr_ref, krt_ref, ka_ref, va_ref, qb_ref, kb_ref, vb_ref, kbt_ref, vbt_ref):
    n_sub = x_ref.shape[1] // sub
    for si in range(n_sub):
        rows = slice(si * sub, (si + 1) * sub)
        x = x_ref[0, rows, :]
        ada_rows = rows if ada_ref.shape[1] > 1 else slice(None)
        shift = ada_ref[0, ada_rows, 0:D_MODEL]
        scale = ada_ref[0, ada_rows, D_MODEL:2 * D_MODEL]
        h = x * _row_rms(x, 1.0 / D_MODEL) * (gmix_ref[...] * (1.0 + scale)) + shift
        z = _dot(h.astype(BF16), w_in_ref[...])

        cos_t = cos_ref[rows, :]
        sin_a = sina_ref[rows, :]
        sin_b = sinb_ref[rows, :]

        c_q = z[:, 0:Q_LORA]
        cqn = c_q * _row_rms(c_q, 1.0 / Q_LORA) * gql_ref[...]
        q_raw = _dot(cqn.astype(BF16), wq_ref[...])
        r_full = _group_rms(q_raw, gq_ref, e2q_ref, icq_ref)
        q_gain = gq_row_ref[...] * (QK_DIM_A ** -0.5 * LOG2E)
        cos_b, sin_a_b, sin_b_b = cos_t.astype(BF16), sin_a.astype(BF16), sin_b.astype(BF16)
        for hd in range(N_HEADS_A):
            hb = slice(hd * HEAD_BLOCK, (hd + 1) * HEAD_BLOCK)
            qn = (q_raw[:, hb] * r_full[:, hb] * q_gain).astype(BF16)
            qa_ref[0, rows, hb] = _rope_block(qn, cos_b, sin_a_b, sin_b_b)

        c_kv = z[:, OFF_CKV:OFF_CKV + KV_LORA]
        lat = c_kv * _row_rms(c_kv, 1.0 / KV_LORA) * gkv_ref[...]
        lat_ref[0, rows, :] = lat

        kp = z[:, OFF_KPE:OFF_KPE + LANES]
        kr = _rope_block(kp * _row_rms(kp, 1.0 / ROPE_DIM) * gkr_row_ref[...], cos_t, sin_a, sin_b)
        kr_ref[0, rows, :] = pltpu.roll(kr, LANES - ROPE_LANE0, 1)[:, 0:ROPE_DIM]
        krt_ref[0, :, rows] = kr.T[ROPE_LANE0:ROPE_LANE0 + ROPE_DIM, :]

        def store(hd, k_blk, v_blk):
            ka_ref[0, rows, hd * HEAD_BLOCK:(hd + 1) * HEAD_BLOCK] = k_blk
            va_ref[0, rows, hd * HEAD_BLOCK:(hd + 1) * HEAD_BLOCK] = v_blk

        _expand_kv(lat, kr, wkv_ref[...], gk_row_ref[...], store)

        zq = z[:, OFF_QB:OFF_QB + B_COLS]
        qb_ref[0, rows, :] = (zq * _group_rms(zq, gb_ref, e2b_ref, icb_ref)
                              * (gqb_row_ref[...] * (HEAD_DIM_B ** -0.5 * LOG2E))).astype(BF16)
        zk = z[:, OFF_KB:OFF_KB + B_COLS]
        k_b = zk * _group_rms(zk, gb_ref, e2b_ref, icb_ref) * gkb_row_ref[...]
        v_b = z[:, OFF_VB:OFF_VB + B_COLS]
        kb_ref[0, rows, :] = k_b.astype(BF16)
        vb_ref[0, rows, :] = v_b.astype(BF16)

        if si == n_sub - 1:
            @pl.when(pl.program_id(1) == pl.num_programs(1) - 1)
            def _():
                kbt_ref[0] = k_b.T if tail_feat_major else k_b
                vbt_ref[0] = v_b.T if tail_feat_major else v_b


def _mixer_in_call(x, ada, tables, consts, weights, tm, sub, tail_feat_major):
    nb, sb, _ = x.shape
    nj = sb // tm
    keep = sub
    assert tm % sub == 0 and sb % tm == 0
    ada_rows = ada.shape[1]
    if ada_rows == 1:
        ada_spec = pl.BlockSpec((1, 1, 6 * D_MODEL), lambda b, j: (b, 0, 0))
    else:
        ada_spec = pl.BlockSpec((1, tm, 6 * D_MODEL), lambda b, j: (b, j, 0))
    tab_spec = pl.BlockSpec((tm, LANES), lambda b, j: (j, 0))
    tok = lambda c: pl.BlockSpec((1, tm, c), lambda b, j: (b, j, 0))
    if tail_feat_major:
        tail = pl.BlockSpec((1, B_COLS, keep), lambda b, j: (b, 0, 0))
        tail_shape = jax.ShapeDtypeStruct((nb, B_COLS, keep), F32)
    else:
        tail = pl.BlockSpec((1, keep, B_COLS), lambda b, j: (b, 0, 0))
        tail_shape = jax.ShapeDtypeStruct((nb, keep, B_COLS), F32)
    const_in = [weights["g_mix"], weights["w_in_mix"], weights["g_q_lora"], weights["w_q_up"], weights["g_kv_lora"],
                weights["gq_row"], weights["gkr_row"], weights["gqb_row"], weights["gkb_row"],
                consts["g_q"], consts["e2_q"], consts["ic_q"], consts["g_b"], consts["e2_b"], consts["ic_b"],
                weights["w_kv_up"], weights["gk_row"]]
    out_shape = [jax.ShapeDtypeStruct((nb, sb, QA_COLS), BF16),
                 jax.ShapeDtypeStruct((nb, sb, KV_LORA), F32),
                 jax.ShapeDtypeStruct((nb, sb, ROPE_DIM), F32),
                 jax.ShapeDtypeStruct((nb, ROPE_DIM, sb), F32),
                 jax.ShapeDtypeStruct((nb, sb, QA_COLS), BF16),
                 jax.ShapeDtypeStruct((nb, sb, QA_COLS), BF16),
                 jax.ShapeDtypeStruct((nb, sb, B_COLS), BF16),
                 jax.ShapeDtypeStruct((nb, sb, B_COLS), BF16),
                 jax.ShapeDtypeStruct((nb, sb, B_COLS), BF16),
                 tail_shape, tail_shape]
    rope_t = pl.BlockSpec((1, ROPE_DIM, tm), lambda b, j: (b, 0, j))
    return pl.pallas_call(
        functools.partial(_mixer_in_kernel, sub, tail_feat_major),
        grid=(nb, nj),
        in_specs=[tok(D_MODEL), ada_spec, tab_spec, tab_spec, tab_spec] + [_const_spec(a.shape) for a in const_in],
        out_specs=[tok(QA_COLS), tok(KV_LORA), tok(ROPE_DIM), rope_t, tok(QA_COLS), tok(QA_COLS),
                   tok(B_COLS), tok(B_COLS), tok(B_COLS), tail, tail],
        out_shape=out_shape,
        compiler_params=_cparams(2),
        name="mixer_in",
    )(x, ada, tables[0], tables[1], tables[2], *const_in)


def _expand_kv(lat, slot, wkv, gk_row, store):
    tm = lat.shape[0]
    kv = _dot(lat.astype(BF16), wkv)
    nope = lax.broadcasted_iota(jnp.int32, (tm, LANES), 1) < NOPE_DIM
    for hd in range(wkv.shape[1] // HEAD_BLOCK):
        blk = kv[:, hd * HEAD_BLOCK:(hd + 1) * HEAD_BLOCK]
        ssq = jnp.sum(jnp.where(nope, blk * blk, 0.0), axis=-1, keepdims=True)
        r = lax.rsqrt(ssq * (1.0 / NOPE_DIM) + EPS)
        store(hd, jnp.where(nope, blk * r * gk_row, slot).astype(BF16), jnp.where(nope, 1.0, blk).astype(BF16))


def _normalize_heads(acc0, acc1, lane):
    o0 = acc0 / pltpu.roll(acc0, V_DIM_A, 1)
    o1 = acc1 / pltpu.roll(acc1, V_DIM_A, 1)
    return jnp.where(lane < V_DIM_A, pltpu.roll(o0, V_DIM_A, 1), o1)


def _mla_prompt_kernel(tq, q_ref, k_ref, v_ref, o_ref):
    seq = q_ref.shape[1]
    n_heads = q_ref.shape[2] // HEAD_BLOCK
    lane = lax.broadcasted_iota(jnp.int32, (tq, LANES), 1)
    row_c = lax.broadcasted_iota(jnp.int32, (tq, tq), 0) // CHUNK
    col_c = lax.broadcasted_iota(jnp.int32, (tq, tq), 1) // CHUNK
    diag_ok = row_c >= col_c
    for qi in reversed(range(seq // tq)):
        r0 = qi * tq
        accs = []
        for hd in range(n_heads):
            hb = slice(hd * HEAD_BLOCK, (hd + 1) * HEAD_BLOCK)
            q = q_ref[0, r0:r0 + tq, hb]
            s_d = jnp.where(diag_ok, _dot_nt(q, k_ref[0, r0:r0 + tq, hb]), NEG_INF)
            m = jnp.max(s_d, axis=-1, keepdims=True)
            if qi > 0:
                s_f = _dot_nt(q, k_ref[0, 0:r0, hb])
                m = jnp.maximum(m, jnp.max(s_f, axis=-1, keepdims=True))
            acc = _dot(jnp.exp2(s_d - m).astype(BF16), v_ref[0, r0:r0 + tq, hb])
            if qi > 0:
                acc = acc + _dot(jnp.exp2(s_f - m).astype(BF16), v_ref[0, 0:r0, hb])
            accs.append(acc)
        for pr in range(n_heads // 2):
            o_ref[0, r0:r0 + tq, pr * LANES:(pr + 1) * LANES] = _normalize_heads(
                accs[2 * pr], accs[2 * pr + 1], lane).astype(BF16)


def _mla_prompt_call(q_a, k_a, v_a, tq, heads_per_step):
    nb, seq, _ = q_a.shape
    n_groups = N_HEADS_A // heads_per_step
    group = pl.BlockSpec((1, seq, heads_per_step * HEAD_BLOCK), lambda b, g: (b, 0, g))
    return pl.pallas_call(
        functools.partial(_mla_prompt_kernel, tq),
        grid=(nb, n_groups),
        in_specs=[group, group, group],
        out_specs=pl.BlockSpec((1, seq, heads_per_step * V_DIM_A), lambda b, g: (b, 0, g)),
        out_shape=jax.ShapeDtypeStruct((nb, seq, VA_COLS), BF16),
        compiler_params=_cparams(2),
        name="mla_prompt",
    )(q_a, k_a, v_a)


def _mla_sample_kernel(q_ref, lat_ref, krt_ref, kn_ref, vn_ref, wkv_ref, wk_ref, gk_row_ref, gsum_ref, o_ref):
    rows = q_ref.shape[1]
    past = lat_ref.shape[1]
    n_stack = N_HEADS_A * rows
    lane = lax.broadcasted_iota(jnp.int32, (rows, LANES), 1)
    lat_b = lat_ref[0].astype(BF16)

    kn_raw = _dot(lat_b, wk_ref[...])
    ssq = _dot((kn_raw * kn_raw).astype(BF16), gsum_ref[...])
    r_t = lax.rsqrt(ssq * (1.0 / NOPE_DIM) + EPS).T

    q_blocks = [q_ref[0, :, hd * HEAD_BLOCK:(hd + 1) * HEAD_BLOCK] for hd in range(N_HEADS_A)]
    q_abs = jnp.concatenate(
        [_dot_nt((q_blocks[hd].astype(F32) * gk_row_ref[...]).astype(BF16), wkv_ref[:, hd * HEAD_BLOCK:(hd + 1) * HEAD_BLOCK])
         for hd in range(N_HEADS_A)], axis=0).astype(BF16)
    r_rows = jnp.concatenate([jnp.broadcast_to(r_t[hd:hd + 1, :], (rows, past)) for hd in range(N_HEADS_A)], axis=0)
    krt_pad = jnp.concatenate([jnp.zeros((ROPE_LANE0, past), F32), krt_ref[0],
                               jnp.zeros((LANES - ROPE_LANE0 - ROPE_DIM, past), F32)], axis=0).astype(BF16)
    q_stack = jnp.concatenate(q_blocks, axis=0)
    s_c = _dot_nt(q_abs, lat_b) * r_rows + _dot(q_stack, krt_pad)

    s_n = jnp.concatenate(
        [jnp.where(lane < rows, _dot_nt(q_blocks[hd], _pad_rows(kn_ref[0, :, hd * HEAD_BLOCK:(hd + 1) * HEAD_BLOCK], LANES)),
                   NEG_INF) for hd in range(N_HEADS_A)], axis=0)
    m = jnp.maximum(jnp.max(s_c, axis=-1, keepdims=True), jnp.max(s_n, axis=-1, keepdims=True))
    p_c = jnp.exp2(s_c - m)
    p_n = jnp.exp2(s_n - m)
    l = jnp.sum(p_c, axis=-1, keepdims=True) + jnp.sum(p_n, axis=-1, keepdims=True)
    ctx = _dot(p_c.astype(BF16), lat_b).astype(BF16)
    p_n = p_n.astype(BF16)

    outs = []
    for hd in range(N_HEADS_A):
        hr = slice(hd * rows, (hd + 1) * rows)
        hb = slice(hd * HEAD_BLOCK, (hd + 1) * HEAD_BLOCK)
        acc = _dot(ctx[hr], wkv_ref[:, hb]) + _dot(p_n[hr], _pad_rows(vn_ref[0, :, hb], LANES))
        outs.append(acc / l[hr])
    for pr in range(N_HEADS_A // 2):
        o_ref[0, :, pr * LANES:(pr + 1) * LANES] = jnp.where(
            lane < V_DIM_A, pltpu.roll(outs[2 * pr], V_DIM_A, 1), outs[2 * pr + 1]).astype(BF16)


def _mla_sample_call(q_a, latent_cache, k_rope_cache_t, k_new, v_new, weights, consts):
    nb, rows, _ = q_a.shape
    past = latent_cache.shape[1]
    tok = pl.BlockSpec((1, rows, QA_COLS), lambda b: (b, 0, 0))
    const_in = [weights["w_kv_up"], weights["w_k_only"], weights["gk_row"], consts["g_b"]]
    return pl.pallas_call(
        _mla_sample_kernel,
        grid=(nb,),
        in_specs=[tok, pl.BlockSpec((1, past, KV_LORA), lambda b: (b, 0, 0)),
                  pl.BlockSpec((1, ROPE_DIM, past), lambda b: (b, 0, 0)), tok, tok]
        + [_const_spec(a.shape) for a in const_in],
        out_specs=pl.BlockSpec((1, rows, VA_COLS), lambda b: (b, 0, 0)),
        out_shape=jax.ShapeDtypeStruct((nb, rows, VA_COLS), BF16),
        compiler_params=_cparams(1),
        name="mla_sample",
    )(q_a, latent_cache, k_rope_cache_t, k_new, v_new, *const_in)


def _toeplitz_bias(g0, rows):
    far = g0[:, 0:1]
    x0 = jnp.broadcast_to(g0[:, 0:LANES], (rows, LANES))
    x1 = jnp.broadcast_to(g0[:, LANES:2 * LANES], (rows, LANES))
    row = lax.broadcasted_iota(jnp.int32, (rows, LANES), 0)
    lane = lax.broadcasted_iota(jnp.int32, (rows, LANES), 1)
    step = 1
    while step < rows:
        r0 = pltpu.roll(x0, step, 1)
        r1 = pltpu.roll(x1, step, 1)
        keep = lane >= step
        take = (row & step) != 0
        x0, x1 = jnp.where(take, jnp.where(keep, r0, r1), x0), jnp.where(take, jnp.where(keep, r1, r0), x1)
        step *= 2
    return jnp.where(lane < row, far, x0), x1, far


def _band_bias_kernel(rb_ref, bias_ref):
    hd = pl.program_id(0)
    tw0, tw1, far = _toeplitz_bias(rb_ref[pl.ds(hd, 1), :], LANES)
    far_blk = jnp.broadcast_to(far, (LANES, LANES))
    n_blk = BAND_WIN // LANES
    row_c = lax.broadcasted_iota(jnp.int32, (LANES, LANES), 0) // CHUNK
    lane = lax.broadcasted_iota(jnp.int32, (LANES, LANES), 1)
    for half in range(BAND_TQ // LANES):
        first_tw = BAND_WINDOW // LANES - 1 + half
        for cb in range(n_blk):
            blk = tw0 if cb == first_tw else (tw1 if cb == first_tw + 1 else far_blk)
            q_c = row_c + half * (LANES // CHUNK)
            col_c = (cb * LANES + lane) // CHUNK
            ok = (col_c >= q_c) & (col_c <= q_c + LEFT_CHUNKS)
            bias_ref[0, half * LANES:(half + 1) * LANES, cb * LANES:(cb + 1) * LANES] = jnp.where(ok, blk * LOG2E, NEG_INF)


def _band_bias_call(rb_rev):
    return pl.pallas_call(
        _band_bias_kernel,
        grid=(N_HEADS_B,),
        in_specs=[_const_spec(rb_rev.shape)],
        out_specs=pl.BlockSpec((1, BAND_TQ, BAND_WIN), lambda h: (h, 0, 0)),
        out_shape=jax.ShapeDtypeStruct((N_HEADS_B, BAND_TQ, BAND_WIN), F32),
        compiler_params=_cparams(1),
        name="band_bias",
    )(rb_rev)


def _split_heads(q, lane):
    zero = jnp.zeros_like(q)
    return jnp.concatenate([jnp.where(lane < HEAD_DIM_B, q, zero), jnp.where(lane >= HEAD_DIM_B, q, zero)], axis=0)


def _band_prompt_kernel(q_ref, k_ref, v_ref, bias_ref, o_ref, vext_ref):
    seq = q_ref.shape[1]
    n_pairs = q_ref.shape[2] // LANES
    lane_q = lax.broadcasted_iota(jnp.int32, (BAND_TQ, LANES), 1)
    for pr in range(n_pairs):
        vext_ref[pr, :, 0:LANES] = v_ref[0, :, pr * LANES:(pr + 1) * LANES]
        vext_ref[pr, :, LANES:2 * LANES] = jnp.ones((seq, LANES), BF16)
    for t in reversed(range(seq // BAND_TQ)):
        t0 = t * BAND_TQ
        k_lo = max(t0 - BAND_WINDOW, 0)
        w = t0 + BAND_TQ - k_lo
        for pr in range(n_pairs):
            cols = slice(pr * LANES, (pr + 1) * LANES)
            q2 = _split_heads(q_ref[0, t0:t0 + BAND_TQ, cols], lane_q)
            bias2 = jnp.concatenate([bias_ref[2 * pr, :, BAND_WIN - w:BAND_WIN],
                                     bias_ref[2 * pr + 1, :, BAND_WIN - w:BAND_WIN]], axis=0)
            s = _dot_nt(q2, k_ref[0, k_lo:t0 + BAND_TQ, cols]) + bias2
            p = jnp.exp2(s - jnp.max(s, axis=-1, keepdims=True))
            acc = _dot(p.astype(BF16), vext_ref[pr, k_lo:t0 + BAND_TQ, :])
            o2 = acc[:, 0:LANES] / acc[:, LANES:2 * LANES]
            o_ref[0, t0:t0 + BAND_TQ, cols] = jnp.where(lane_q < HEAD_DIM_B, o2[0:BAND_TQ],
                                                        o2[BAND_TQ:2 * BAND_TQ]).astype(BF16)


def _band_prompt_call(q_b, k_b, v_b, bias, pairs_per_step):
    nb, seq, _ = q_b.shape
    n_groups = N_HEADS_B // (2 * pairs_per_step)
    spec = pl.BlockSpec((1, seq, pairs_per_step * LANES), lambda b, p: (b, 0, p))
    return pl.pallas_call(
        _band_prompt_kernel,
        grid=(nb, n_groups),
        in_specs=[spec, spec, spec, pl.BlockSpec((2 * pairs_per_step, BAND_TQ, BAND_WIN), lambda b, p: (p, 0, 0),
                                                 pipeline_mode=pl.Buffered(1 if n_groups == 1 else 2))],
        out_specs=spec,
        out_shape=jax.ShapeDtypeStruct((nb, seq, B_COLS), BF16),
        scratch_shapes=[pltpu.VMEM((pairs_per_step, seq, 2 * LANES), BF16)],
        compiler_params=_cparams(2),
        name="band_prompt",
    )(q_b, k_b, v_b, bias)


def _band_sample_kernel(q_ref, kct_ref, vct_ref, kn_ref, vn_ref, bias_ref, o_ref):
    rows = q_ref.shape[1]
    n_cache = kct_ref.shape[2]
    lane = lax.broadcasted_iota(jnp.int32, (rows, LANES), 1)
    lane2 = lax.broadcasted_iota(jnp.int32, (2 * rows, LANES), 1)
    for pair in range(N_HEADS_B // 2):
        cols = slice(pair * LANES, (pair + 1) * LANES)
        q2 = _split_heads(q_ref[0, :, cols], lane)
        kct = kct_ref[0, cols, :].astype(BF16)
        vct = vct_ref[0, cols, :].astype(BF16)
        kn = _pad_rows(kn_ref[0, :, cols], LANES)
        vn = _pad_rows(vn_ref[0, :, cols], LANES)
        bias_c = jnp.concatenate([bias_ref[2 * pair, :, 0:n_cache], bias_ref[2 * pair + 1, :, 0:n_cache]], axis=0)
        bias_n = jnp.concatenate([bias_ref[2 * pair, :, n_cache:n_cache + LANES],
                                  bias_ref[2 * pair + 1, :, n_cache:n_cache + LANES]], axis=0)
        s_c = _dot(q2, kct) + bias_c
        s_n = jnp.where(lane2 < rows, _dot_nt(q2, kn) + bias_n, NEG_INF)
        m = jnp.maximum(jnp.max(s_c, axis=-1, keepdims=True), jnp.max(s_n, axis=-1, keepdims=True))
        p_c = jnp.exp2(s_c - m)
        p_n = jnp.exp2(s_n - m)
        l = jnp.sum(p_c, axis=-1, keepdims=True) + jnp.sum(p_n, axis=-1, keepdims=True)
        o2 = (_dot_nt(p_c.astype(BF16), vct) + _dot(p_n.astype(BF16), vn)) / l
        o_ref[0, :, cols] = jnp.where(lane < HEAD_DIM_B, o2[0:rows], o2[rows:2 * rows]).astype(BF16)


def _band_sample_call(q_b, k_cache_t, v_cache_t, k_new, v_new, bias):
    nb, rows, _ = q_b.shape
    n_cache = k_cache_t.shape[2]
    tok = pl.BlockSpec((1, rows, B_COLS), lambda b: (b, 0, 0))
    cache = pl.BlockSpec((1, B_COLS, n_cache), lambda b: (b, 0, 0))
    return pl.pallas_call(
        _band_sample_kernel,
        grid=(nb,),
        in_specs=[tok, cache, cache, tok, tok, pl.BlockSpec((N_HEADS_B, rows, BAND_WIN), lambda b: (0, 0, 0))],
        out_specs=tok,
        out_shape=jax.ShapeDtypeStruct((nb, rows, B_COLS), BF16),
        compiler_params=_cparams(1),
        name="band_sample",
    )(q_b, k_cache_t, v_cache_t, k_new, v_new, bias)


def _tail_rows(rows, x_ref, oa_ref, ob_ref, ada_ref, y_ref, gmix_ref, wg_ref, woa_ref, wob_ref, wout_ref, gffn_ref,
               wgate_ref, wup_ref, wdown_ref):
    x = x_ref[0, rows, :]
    ada_rows = rows if ada_ref.shape[1] > 1 else slice(None)
    ada = lambda k: ada_ref[0, ada_rows, k * D_MODEL:(k + 1) * D_MODEL]
    h = x * _row_rms(x, 1.0 / D_MODEL) * gmix_ref[...]
    h = (h * (1.0 + ada(1)) + ada(0)).astype(BF16)
    gates = jax.nn.sigmoid(_dot(h, wg_ref[...]))
    y_a = _dot(oa_ref[0, rows, :], woa_ref[...])
    y_b = _dot(ob_ref[0, rows, :], wob_ref[...])
    mixed = gates[:, 0:D_MODEL] * y_a + gates[:, D_MODEL:2 * D_MODEL] * y_b
    x1 = x + ada(2) * _dot(mixed.astype(BF16), wout_ref[...])
    h2 = x1 * _row_rms(x1, 1.0 / D_MODEL) * gffn_ref[...]
    h2 = (h2 * (1.0 + ada(4)) + ada(3)).astype(BF16)
    acc = jnp.zeros_like(x1)
    for c in range(D_FF // FF_CHUNK):
        cols = slice(c * FF_CHUNK, (c + 1) * FF_CHUNK)
        g = _dot(h2, wgate_ref[:, cols])
        u = _dot(h2, wup_ref[:, cols])
        act = (g * jax.nn.sigmoid(g) * u).astype(BF16)
        acc = acc + _dot(act, wdown_ref[cols, :])
    y_ref[0, rows, :] = x1 + ada(5) * acc


def _tail_kernel(sub, x_ref, oa_ref, ob_ref, ada_ref, xs_ref, oas_ref, obs_ref, adas_ref, *rest):
    weights, (y_ref, ys_ref) = rest[:-2], rest[-2:]
    for si in range(x_ref.shape[1] // sub):
        _tail_rows(slice(si * sub, (si + 1) * sub), x_ref, oa_ref, ob_ref, ada_ref, y_ref, *weights)

    @pl.when((pl.program_id(0) == 0) & (pl.program_id(1) == 0))
    def _():
        _tail_rows(slice(None), xs_ref, oas_ref, obs_ref, adas_ref, ys_ref, *weights)


def _tail_call(x, o_a, o_b, ada, xs, o_a_s, o_b_s, ada_s, weights, tm, sub):
    nb, sb, _ = x.shape
    rows_s = xs.shape[1]
    tok = lambda c: pl.BlockSpec((1, tm, c), lambda b, j: (b, j, 0))
    whole = lambda c: pl.BlockSpec((1, rows_s, c), lambda b, j: (0, 0, 0))
    const_in = [weights["g_mix"], weights["w_in_gate"], weights["w_o_a"], weights["w_o_b"], weights["w_out"],
                weights["g_ffn"], weights["w_gate"], weights["w_up"], weights["w_down"]]
    return pl.pallas_call(
        functools.partial(_tail_kernel, sub),
        grid=(nb, sb // tm),
        in_specs=[tok(D_MODEL), tok(VA_COLS), tok(B_COLS), pl.BlockSpec((1, 1, 6 * D_MODEL), lambda b, j: (b, 0, 0)),
                  whole(D_MODEL), whole(VA_COLS), whole(B_COLS), whole(6 * D_MODEL)]
        + [_const_spec(a.shape) for a in const_in],
        out_specs=[tok(D_MODEL), whole(D_MODEL)],
        out_shape=[jax.ShapeDtypeStruct((nb, sb, D_MODEL), F32), jax.ShapeDtypeStruct((1, rows_s, D_MODEL), F32)],
        compiler_params=_cparams(2),
        name="tail",
    )(x, o_a, o_b, ada, xs, o_a_s, o_b_s, ada_s, *const_in)


def _group_constants():
    def pack(g, inv_cnt):
        ic = np.ones((1, LANES), np.float32)
        ic[0, :len(inv_cnt)] = inv_cnt
        return jnp.asarray(g, BF16), jnp.asarray(np.concatenate([g.T, g.T], axis=0), BF16), jnp.asarray(ic)

    g_q = np.zeros((QA_COLS, LANES), np.float32)
    for hd in range(N_HEADS_A):
        g_q[hd * HEAD_BLOCK:hd * HEAD_BLOCK + NOPE_DIM, hd] = 1.0
        g_q[hd * HEAD_BLOCK + ROPE_LANE0:hd * HEAD_BLOCK + ROPE_LANE0 + ROPE_DIM, N_HEADS_A + hd] = 1.0
    g_b = np.zeros((B_COLS, LANES), np.float32)
    for hd in range(N_HEADS_B):
        g_b[hd * HEAD_DIM_B:(hd + 1) * HEAD_DIM_B, hd] = 1.0
    c = {}
    c["g_q"], c["e2_q"], c["ic_q"] = pack(g_q, [1.0 / NOPE_DIM] * N_HEADS_A + [1.0 / ROPE_DIM] * N_HEADS_A)
    c["g_b"], c["e2_b"], c["ic_b"] = pack(g_b, [1.0 / HEAD_DIM_B] * N_HEADS_B)
    return c


def _rope_tables(pos):
    inv_freq = ROPE_BASE ** (-jnp.arange(HALF_ROPE, dtype=F32) / HALF_ROPE)
    ang = pos.astype(F32)[:, None] * inv_freq[None, :]
    cos, sin = jnp.cos(ang), jnp.sin(ang)
    n = pos.shape[0]
    ones = jnp.ones((n, ROPE_LANE0), F32)
    zeros = jnp.zeros((n, ROPE_LANE0), F32)
    pad1 = jnp.ones((n, LANES - ROPE_LANE0 - ROPE_DIM), F32)
    pad0 = jnp.zeros((n, LANES - ROPE_LANE0 - ROPE_DIM), F32)
    z16 = jnp.zeros((n, HALF_ROPE), F32)
    cos_t = jnp.concatenate([ones, cos, cos, pad1], axis=1)
    sin_a = jnp.concatenate([zeros, -sin, z16, pad0], axis=1)
    sin_b = jnp.concatenate([zeros, z16, sin, pad0], axis=1)
    return cos_t, sin_a, sin_b


def _layer_weights(l, w_in, g_norm_mix, g_q_lora, w_q_up, g_kv_lora, w_kv_up, g_qn_a, g_kn_a, g_qr_a, g_kr_a,
                   g_q_b, g_k_b, w_o_a, w_o_b, w_out, g_norm_ffn, w_gate, w_up, w_down):
    w = {}
    w["w_in_mix"], w["w_in_gate"] = _w_in_prep_call(jnp.transpose(w_in[l]), 256)
    wq3 = w_q_up[l].reshape(Q_LORA, N_HEADS_A, QK_DIM_A)
    w["w_q_up"] = jnp.pad(wq3, ((0, 0), (0, 0), (0, HEAD_BLOCK - QK_DIM_A))).reshape(Q_LORA, QA_COLS).astype(BF16)
    w["w_kv_up"] = w_kv_up[l].astype(BF16)
    w["w_k_only"] = w_kv_up[l].reshape(KV_LORA, N_HEADS_A, NOPE_DIM + V_DIM_A)[..., :NOPE_DIM].reshape(
        KV_LORA, N_HEADS_A * NOPE_DIM).astype(BF16)
    zpad = jnp.zeros((HEAD_BLOCK - QK_DIM_A,), F32)
    w["gq_row"] = jnp.concatenate([g_qn_a[l], g_qr_a[l], zpad]).reshape(1, HEAD_BLOCK)
    w["gk_row"] = jnp.concatenate([g_kn_a[l], jnp.zeros((HEAD_BLOCK - NOPE_DIM,), F32)]).reshape(1, HEAD_BLOCK)
    w["gkr_row"] = jnp.concatenate([jnp.zeros((ROPE_LANE0,), F32), g_kr_a[l], zpad]).reshape(1, LANES)
    w["gqb_row"] = jnp.tile(g_q_b[l], N_HEADS_B).reshape(1, B_COLS)
    w["gkb_row"] = jnp.tile(g_k_b[l], N_HEADS_B).reshape(1, B_COLS)
    w["g_mix"] = g_norm_mix[l].reshape(1, D_MODEL)
    w["g_q_lora"] = g_q_lora[l].reshape(1, Q_LORA)
    w["g_kv_lora"] = g_kv_lora[l].reshape(1, KV_LORA)
    w["g_ffn"] = g_norm_ffn[l].reshape(1, D_MODEL)
    w["w_o_a"] = w_o_a[l].astype(BF16)
    w["w_o_b"] = w_o_b[l].astype(BF16)
    w["w_out"] = w_out[l].astype(BF16)
    w["w_gate"] = w_gate[l].astype(BF16)
    w["w_up"] = w_up[l].astype(BF16)
    w["w_down"] = w_down[l].astype(BF16)
    return w


def kernel(x_prompt, x_sample, c_prompt, c_sample, cache_kv_latent, cache_k_rope, cache_band_k, cache_band_v, w_ada, b_ada, g_norm_mix, w_in, g_q_lora, w_q_up, g_kv_lora, w_kv_up, g_qn_a, g_kn_a, g_qr_a, g_kr_a, g_q_b, g_k_b, rel_bias, w_o_a, w_o_b, w_out, g_norm_ffn, w_gate, w_up, w_down):
    depth = w_in.shape[0]
    nb, seq, _ = x_prompt.shape
    nbs, sd, _ = x_sample.shape
    past = cache_kv_latent.shape[2]
    n_buf = cache_band_k.shape[2]
    keep = min(BAND_WINDOW, seq)
    assert depth == 1 and nbs * sd == LANES and n_buf == BAND_WINDOW and seq % 512 == 0 and past % 512 == 0
    tm = 512
    rows_s = nbs * sd

    consts = _group_constants()
    tab_p = _rope_tables(jnp.arange(seq))
    tab_s = _rope_tables(past + (jnp.arange(rows_s) % sd))
    xs = x_sample.reshape(1, rows_s, D_MODEL)

    l = 0
    wts = _layer_weights(l, w_in, g_norm_mix, g_q_lora, w_q_up, g_kv_lora, w_kv_up, g_qn_a, g_kn_a, g_qr_a, g_kr_a,
                         g_q_b, g_k_b, w_o_a, w_o_b, w_out, g_norm_ffn, w_gate, w_up, w_down)
    band_bias = _band_bias_call(rel_bias[l][:, 2 * REL_CLIP:0:-1])

    ada = _ada_call(jnp.concatenate([c_prompt, c_sample], axis=0), w_ada[l], b_ada[l])
    ada_p = ada[:nb].reshape(nb, 1, 6 * D_MODEL)
    ada_s = jnp.repeat(ada[nb:], sd, axis=0).reshape(1, rows_s, 6 * D_MODEL)

    assert keep == tm
    qa, lat, _, krt, k_a, v_a, qb, kb, vb, kbt_tail, vbt_tail = _mixer_in_call(x_prompt, ada_p, tab_p, consts, wts,
                                                                                2 * tm, tm, True)
    o_a = _mla_prompt_call(qa, k_a, v_a, 512, 4)
    o_b = _band_prompt_call(qb, kb, vb, band_bias, 4)

    qa_s, lat_s, kr_s, _, kn, vn, qb_s, kb_s, vb_s, kb_s32, vb_s32 = _mixer_in_call(xs, ada_s, tab_s, consts, wts,
                                                                                    rows_s, rows_s, False)
    o_a_s = _mla_sample_call(qa_s.reshape(nbs, sd, QA_COLS), cache_kv_latent[l],
                             jnp.transpose(cache_k_rope[l], (0, 2, 1)),
                             kn.reshape(nbs, sd, QA_COLS), vn.reshape(nbs, sd, QA_COLS), wts, consts)
    feat_major = lambda c: jnp.transpose(c, (0, 2, 3, 1)).reshape(nbs, B_COLS, n_buf)
    o_b_s = _band_sample_call(qb_s.reshape(nbs, sd, B_COLS), feat_major(cache_band_k[l]), feat_major(cache_band_v[l]),
                              kb_s.reshape(nbs, sd, B_COLS), vb_s.reshape(nbs, sd, B_COLS), band_bias)

    y_p, y_s = _tail_call(x_prompt, o_a, o_b, ada_p, xs, o_a_s.reshape(1, rows_s, VA_COLS),
                          o_b_s.reshape(1, rows_s, B_COLS), ada_s, wts, tm, tm)

    tok_major = lambda t: jnp.transpose(t.reshape(nb, N_HEADS_B, HEAD_DIM_B, keep), (0, 3, 1, 2))[None]
    return (y_p, y_s.reshape(nbs, sd, D_MODEL),
            lat.reshape(1, nb, seq, KV_LORA), jnp.transpose(krt, (0, 2, 1))[None],
            tok_major(kbt_tail), tok_major(vbt_tail),
            lat_s.reshape(1, nbs, sd, KV_LORA), kr_s.reshape(1, nbs, sd, ROPE_DIM),
            kb_s32.reshape(1, nbs, sd, N_HEADS_B, HEAD_DIM_B), vb_s32.reshape(1, nbs, sd, N_HEADS_B, HEAD_DIM_B))
```

```python
import functools

import jax
import jax.numpy as jnp
import numpy as np
from jax import lax
from jax.experimental import pallas as pl
from jax.experimental.pallas import tpu as pltpu

D_MODEL = 1024
CHUNK = 64
EPS = 1e-6
NEG_INF = -1e30
N_HEADS_A = 8
NOPE_DIM = 64
ROPE_DIM = 32
HALF_ROPE = ROPE_DIM // 2
V_DIM_A = 64
QK_DIM_A = NOPE_DIM + ROPE_DIM
Q_LORA = 384
KV_LORA = 256
ROPE_BASE = 10000.0
N_HEADS_B = 8
HEAD_DIM_B = 64
LEFT_CHUNKS = 8
BAND_WINDOW = LEFT_CHUNKS * CHUNK
REL_CLIP = 128
D_FF = -(-(8 * D_MODEL) // (3 * 256)) * 256
COL_QKV_B = 3 * N_HEADS_B * HEAD_DIM_B
N_GATE_COLS = 2 * D_MODEL

LANES = 128
HEAD_BLOCK = LANES
ROPE_LANE0 = NOPE_DIM
QA_COLS = N_HEADS_A * HEAD_BLOCK
VA_COLS = N_HEADS_A * V_DIM_A
B_COLS = N_HEADS_B * HEAD_DIM_B
MIX_COLS = Q_LORA + KV_LORA + LANES + COL_QKV_B
OFF_CKV = Q_LORA
OFF_KPE = Q_LORA + KV_LORA
OFF_QB = OFF_KPE + LANES
OFF_KB = OFF_QB + B_COLS
OFF_VB = OFF_KB + B_COLS
FF_CHUNK = 256
LOG2E = 1.4426950408889634
BAND_TQ = 256
BAND_WIN = BAND_WINDOW + BAND_TQ
VMEM_LIMIT = 60 * 1024 * 1024

BF16 = jnp.bfloat16
F32 = jnp.float32


def _cparams(n_axes):
    return pltpu.CompilerParams(dimension_semantics=("arbitrary",) * n_axes, vmem_limit_bytes=VMEM_LIMIT)


def _const_spec(shape):
    nd = len(shape)
    return pl.BlockSpec(shape, lambda *_: (0,) * nd, pipeline_mode=pl.Buffered(1))


def _dot(a, b):
    return jnp.dot(a, b, preferred_element_type=F32)


def _dot_nt(a, b):
    return lax.dot_general(a, b, (((1,), (1,)), ((), ())), preferred_element_type=F32)


def _pad_rows(x, rows):
    return jnp.concatenate([x, jnp.zeros((rows - x.shape[0], x.shape[1]), x.dtype)], axis=0)


def _row_rms(x, inv_n):
    return lax.rsqrt(jnp.sum(x * x, axis=-1, keepdims=True) * inv_n + EPS)


def _group_rms(x, g_ref, e2_ref, invcnt_ref):
    s = _dot((x * x).astype(BF16), g_ref[...])
    r = lax.rsqrt(s * invcnt_ref[...] + EPS)
    r_hi = r.astype(BF16)
    r_lo = (r - r_hi.astype(F32)).astype(BF16)
    return _dot(jnp.concatenate([r_hi, r_lo], axis=1), e2_ref[...])


def _rope_block(x, cos_t, sin_a, sin_b):
    return x * cos_t + pltpu.roll(x, LANES - HALF_ROPE, 1) * sin_a + pltpu.roll(x, HALF_ROPE, 1) * sin_b


def _ada_kernel(c_ref, w_ref, b_ref, o_ref):
    c = c_ref[...]
    a = (c * jax.nn.sigmoid(c)).astype(BF16)
    o_ref[...] = _dot(a, w_ref[...].astype(BF16)) + b_ref[...]


def _ada_call(c_all, w_ada, b_ada):
    rows = c_all.shape[0]
    n_out = w_ada.shape[1]
    tn = D_MODEL
    return pl.pallas_call(
        _ada_kernel,
        grid=(n_out // tn,),
        in_specs=[pl.BlockSpec((rows, D_MODEL), lambda n: (0, 0)),
                  pl.BlockSpec((D_MODEL, tn), lambda n: (0, n)),
                  pl.BlockSpec((1, tn), lambda n: (0, n))],
        out_specs=pl.BlockSpec((rows, tn), lambda n: (0, n)),
        out_shape=jax.ShapeDtypeStruct((rows, n_out), F32),
        compiler_params=_cparams(1),
        name="ada",
    )(c_all, w_ada, b_ada.reshape(1, n_out))


def _w_in_prep_kernel(wt_ref, mix_ref, gate_ref):
    tk = wt_ref.shape[1]
    n_mix = OFF_KPE + ROPE_DIM + COL_QKV_B

    def put(dst_ref, col0, src0):
        dst_ref[:, col0:col0 + LANES] = wt_ref[src0:src0 + LANES, :].T.astype(BF16)

    for c in range(OFF_KPE // LANES):
        put(mix_ref, c * LANES, c * LANES)
    kpe = jnp.concatenate([jnp.zeros((ROPE_LANE0, tk), F32), wt_ref[OFF_KPE:OFF_KPE + ROPE_DIM, :],
                           jnp.zeros((LANES - ROPE_LANE0 - ROPE_DIM, tk), F32)], axis=0)
    mix_ref[:, OFF_KPE:OFF_QB] = kpe.T.astype(BF16)
    for c in range(COL_QKV_B // LANES):
        put(mix_ref, OFF_QB + c * LANES, OFF_KPE + ROPE_DIM + c * LANES)
    for c in range(N_GATE_COLS // LANES):
        put(gate_ref, c * LANES, n_mix + c * LANES)


def _w_in_prep_call(w_in_t, tk):
    n_cols, d_in = w_in_t.shape
    return pl.pallas_call(
        _w_in_prep_kernel,
        grid=(d_in // tk,),
        in_specs=[pl.BlockSpec((n_cols, tk), lambda i: (0, i))],
        out_specs=[pl.BlockSpec((tk, MIX_COLS), lambda i: (i, 0)), pl.BlockSpec((tk, N_GATE_COLS), lambda i: (i, 0))],
        out_shape=[jax.ShapeDtypeStruct((d_in, MIX_COLS), BF16), jax.ShapeDtypeStruct((d_in, N_GATE_COLS), BF16)],
        compiler_params=_cparams(1),
        name="w_in_prep",
    )(w_in_t)


def _mixer_in_kernel(sub, tail_feat_major,
                     x_ref, ada_ref, cos_ref, sina_ref, sinb_ref, gmix_ref, w_in_ref, gql_ref, wq_ref, gkv_ref,
                     gq_row_ref, gkr_row_ref, gqb_row_ref, gkb_row_ref,
                     gq_ref, e2q_ref, icq_ref, gb_ref, e2b_ref, icb_ref, wkv_ref, gk_row_ref,
                     qa_ref, lat_ref, kr_ref, krt_ref, ka_ref, va_ref, qb_ref, kb_ref, vb_ref, kbt_ref, vbt_ref):
    n_sub = x_ref.shape[1] // sub
    for si in range(n_sub):
        rows = slice(si * sub, (si + 1) * sub)
        x = x_ref[0, rows, :]
        ada_rows = rows if ada_ref.shape[1] > 1 else slice(None)
        shift = ada_ref[0, ada_rows, 0:D_MODEL]
        scale = ada_ref[0, ada_rows, D_MODEL:2 * D_MODEL]
        h = x * _row_rms(x, 1.0 / D_MODEL) * (gmix_ref[...] * (1.0 + scale)) + shift
        z = _dot(h.astype(BF16), w_in_ref[...])

        cos_t = cos_ref[rows, :]
        sin_a = sina_ref[rows, :]
        sin_b = sinb_ref[rows, :]

        c_kv = z[:, OFF_CKV:OFF_CKV + KV_LORA]
        lat = c_kv * _row_rms(c_kv, 1.0 / KV_LORA) * gkv_ref[...]
        lat_ref[0, rows, :] = lat

        kp = z[:, OFF_KPE:OFF_KPE + LANES]
        kr = _rope_block(kp * _row_rms(kp, 1.0 / ROPE_DIM) * gkr_row_ref[...], cos_t, sin_a, sin_b)
        kr_ref[0, rows, :] = pltpu.roll(kr, LANES - ROPE_LANE0, 1)[:, 0:ROPE_DIM]
        krt_ref[0, :, rows] = kr.T[ROPE_LANE0:ROPE_LANE0 + ROPE_DIM, :]

        def store(hd, k_blk, v_blk):
            ka_ref[0, rows, hd * HEAD_BLOCK:(hd + 1) * HEAD_BLOCK] = k_blk
            va_ref[0, rows, hd * HEAD_BLOCK:(hd + 1) * HEAD_BLOCK] = v_blk

        _expand_kv(lat, kr, wkv_ref[...], gk_row_ref[...], store)

        zq = z[:, OFF_QB:OFF_QB + B_COLS]
        qb_ref[0, rows, :] = (zq * _group_rms(zq, gb_ref, e2b_ref, icb_ref)
                              * (gqb_row_ref[...] * (HEAD_DIM_B ** -0.5 * LOG2E))).astype(BF16)
        zk = z[:, OFF_KB:OFF_KB + B_COLS]
        k_b = zk * _group_rms(zk, gb_ref, e2b_ref, icb_ref) * gkb_row_ref[...]
        v_b = z[:, OFF_VB:OFF_VB + B_COLS]
        kb_ref[0, rows, :] = k_b.astype(BF16)
        vb_ref[0, rows, :] = v_b.astype(BF16)

        c_q = z[:, 0:Q_LORA]
        cqn = c_q * _row_rms(c_q, 1.0 / Q_LORA) * gql_ref[...]
        q_raw = _dot(cqn.astype(BF16), wq_ref[...])
        r_full = _group_rms(q_raw, gq_ref, e2q_ref, icq_ref)
        q_gain = gq_row_ref[...] * (QK_DIM_A ** -0.5 * LOG2E)
        cos_b, sin_a_b, sin_b_b = cos_t.astype(BF16), sin_a.astype(BF16), sin_b.astype(BF16)
        for hd in range(N_HEADS_A):
            hb = slice(hd * HEAD_BLOCK, (hd + 1) * HEAD_BLOCK)
            qn = (q_raw[:, hb] * r_full[:, hb] * q_gain).astype(BF16)
            qa_ref[0, rows, hb] = _rope_block(qn, cos_b, sin_a_b, sin_b_b)

        if si == n_sub - 1:
            @pl.when(pl.program_id(1) == pl.num_programs(1) - 1)
            def _():
                kbt_ref[0] = k_b.T if tail_feat_major else k_b
                vbt_ref[0] = v_b.T if tail_feat_major else v_b


def _mixer_in_call(x, ada, tables, consts, weights, tm, sub, tail_feat_major):
    nb, sb, _ = x.shape
    nj = sb // tm
    keep = sub
    assert tm % sub == 0 and sb % tm == 0
    ada_rows = ada.shape[1]
    if ada_rows == 1:
        ada_spec = pl.BlockSpec((1, 1, 6 * D_MODEL), lambda b, j: (b, 0, 0))
    else:
        ada_spec = pl.BlockSpec((1, tm, 6 * D_MODEL), lambda b, j: (b, j, 0))
    tab_spec = pl.BlockSpec((tm, LANES), lambda b, j: (j, 0))
    tok = lambda c: pl.BlockSpec((1, tm, c), lambda b, j: (b, j, 0))
    if tail_feat_major:
        tail = pl.BlockSpec((1, B_COLS, keep), lambda b, j: (b, 0, 0))
        tail_shape = jax.ShapeDtypeStruct((nb, B_COLS, keep), F32)
    else:
        tail = pl.BlockSpec((1, keep, B_COLS), lambda b, j: (b, 0, 0))
        tail_shape = jax.ShapeDtypeStruct((nb, keep, B_COLS), F32)
    const_in = [weights["g_mix"], weights["w_in_mix"], weights["g_q_lora"], weights["w_q_up"], weights["g_kv_lora"],
                weights["gq_row"], weights["gkr_row"], weights["gqb_row"], weights["gkb_row"],
                consts["g_q"], consts["e2_q"], consts["ic_q"], consts["g_b"], consts["e2_b"], consts["ic_b"],
                weights["w_kv_up"], weights["gk_row"]]
    out_shape = [jax.ShapeDtypeStruct((nb, sb, QA_COLS), BF16),
                 jax.ShapeDtypeStruct((nb, sb, KV_LORA), F32),
                 jax.ShapeDtypeStruct((nb, sb, ROPE_DIM), F32),
                 jax.ShapeDtypeStruct((nb, ROPE_DIM, sb), F32),
                 jax.ShapeDtypeStruct((nb, sb, QA_COLS), BF16),
                 jax.ShapeDtypeStruct((nb, sb, QA_COLS), BF16),
                 jax.ShapeDtypeStruct((nb, sb, B_COLS), BF16),
                 jax.ShapeDtypeStruct((nb, sb, B_COLS), BF16),
                 jax.ShapeDtypeStruct((nb, sb, B_COLS), BF16),
                 tail_shape, tail_shape]
    rope_t = pl.BlockSpec((1, ROPE_DIM, tm), lambda b, j: (b, 0, j))
    return pl.pallas_call(
        functools.partial(_mixer_in_kernel, sub, tail_feat_major),
        grid=(nb, nj),
        in_specs=[tok(D_MODEL), ada_spec, tab_spec, tab_spec, tab_spec] + [_const_spec(a.shape) for a in const_in],
        out_specs=[tok(QA_COLS), tok(KV_LORA), tok(ROPE_DIM), rope_t, tok(QA_COLS), tok(QA_COLS),
                   tok(B_COLS), tok(B_COLS), tok(B_COLS), tail, tail],
        out_shape=out_shape,
        compiler_params=_cparams(2),
        name="mixer_in",
    )(x, ada, tables[0], tables[1], tables[2], *const_in)


def _expand_kv(lat, slot, wkv, gk_row, store):
    tm = lat.shape[0]
    kv = _dot(lat.astype(BF16), wkv)
    nope = lax.broadcasted_iota(jnp.int32, (tm, LANES), 1) < NOPE_DIM
    for hd in range(wkv.shape[1] // HEAD_BLOCK):
        blk = kv[:, hd * HEAD_BLOCK:(hd + 1) * HEAD_BLOCK]
        ssq = jnp.sum(jnp.where(nope, blk * blk, 0.0), axis=-1, keepdims=True)
        r = lax.rsqrt(ssq * (1.0 / NOPE_DIM) + EPS)
        store(hd, jnp.where(nope, blk * r * gk_row, slot).astype(BF16), jnp.where(nope, 1.0, blk).astype(BF16))


def _normalize_heads(acc0, acc1, lane):
    o0 = acc0 / pltpu.roll(acc0, V_DIM_A, 1)
    o1 = acc1 / pltpu.roll(acc1, V_DIM_A, 1)
    return jnp.where(lane < V_DIM_A, pltpu.roll(o0, V_DIM_A, 1), o1)


def _mla_prompt_kernel(tq, q_ref, k_ref, v_ref, o_ref):
    seq = q_ref.shape[1]
    n_heads = q_ref.shape[2] // HEAD_BLOCK
    lane = lax.broadcasted_iota(jnp.int32, (tq, LANES), 1)
    row_c = lax.broadcasted_iota(jnp.int32, (tq, tq), 0) // CHUNK
    col_c = lax.broadcasted_iota(jnp.int32, (tq, tq), 1) // CHUNK
    diag_ok = row_c >= col_c
    for qi in reversed(range(seq // tq)):
        r0 = qi * tq
        accs = []
        for hd in range(n_heads):
            hb = slice(hd * HEAD_BLOCK, (hd + 1) * HEAD_BLOCK)
            q = q_ref[0, r0:r0 + tq, hb]
            s_d = jnp.where(diag_ok, _dot_nt(q, k_ref[0, r0:r0 + tq, hb]), NEG_INF)
            m = jnp.max(s_d, axis=-1, keepdims=True)
            if qi > 0:
                s_f = _dot_nt(q, k_ref[0, 0:r0, hb])
                m = jnp.maximum(m, jnp.max(s_f, axis=-1, keepdims=True))
            acc = _dot(jnp.exp2(s_d - m).astype(BF16), v_ref[0, r0:r0 + tq, hb])
            if qi > 0:
                acc = acc + _dot(jnp.exp2(s_f - m).astype(BF16), v_ref[0, 0:r0, hb])
            accs.append(acc)
        for pr in range(n_heads // 2):
            o_ref[0, r0:r0 + tq, pr * LANES:(pr + 1) * LANES] = _normalize_heads(
                accs[2 * pr], accs[2 * pr + 1], lane).astype(BF16)


def _mla_prompt_call(q_a, k_a, v_a, tq, heads_per_step):
    nb, seq, _ = q_a.shape
    n_groups = N_HEADS_A // heads_per_step
    group = pl.BlockSpec((1, seq, heads_per_step * HEAD_BLOCK), lambda b, g: (b, 0, g))
    return pl.pallas_call(
        functools.partial(_mla_prompt_kernel, tq),
        grid=(nb, n_groups),
        in_specs=[group, group, group],
        out_specs=pl.BlockSpec((1, seq, heads_per_step * V_DIM_A), lambda b, g: (b, 0, g)),
        out_shape=jax.ShapeDtypeStruct((nb, seq, VA_COLS), BF16),
        compiler_params=_cparams(2),
        name="mla_prompt",
    )(q_a, k_a, v_a)


def _mla_sample_kernel(q_ref, lat_ref, krt_ref, kn_ref, vn_ref, wkv_ref, wk_ref, gk_row_ref, gsum_ref, o_ref):
    rows = q_ref.shape[1]
    past = lat_ref.shape[1]
    n_stack = N_HEADS_A * rows
    lane = lax.broadcasted_iota(jnp.int32, (rows, LANES), 1)
    lat_b = lat_ref[0].astype(BF16)

    kn_raw = _dot(lat_b, wk_ref[...])
    ssq = _dot((kn_raw * kn_raw).astype(BF16), gsum_ref[...])
    r_t = lax.rsqrt(ssq * (1.0 / NOPE_DIM) + EPS).T

    q_blocks = [q_ref[0, :, hd * HEAD_BLOCK:(hd + 1) * HEAD_BLOCK] for hd in range(N_HEADS_A)]
    q_abs = jnp.concatenate(
        [_dot_nt((q_blocks[hd].astype(F32) * gk_row_ref[...]).astype(BF16), wkv_ref[:, hd * HEAD_BLOCK:(hd + 1) * HEAD_BLOCK])
         for hd in range(N_HEADS_A)], axis=0).astype(BF16)
    r_rows = jnp.concatenate([jnp.broadcast_to(r_t[hd:hd + 1, :], (rows, past)) for hd in range(N_HEADS_A)], axis=0)
    krt_pad = jnp.concatenate([jnp.zeros((ROPE_LANE0, past), F32), krt_ref[0],
                               jnp.zeros((LANES - ROPE_LANE0 - ROPE_DIM, past), F32)], axis=0).astype(BF16)
    q_stack = jnp.concatenate(q_blocks, axis=0)
    s_c = _dot_nt(q_abs, lat_b) * r_rows + _dot(q_stack, krt_pad)

    s_n = jnp.concatenate(
        [jnp.where(lane < rows, _dot_nt(q_blocks[hd], _pad_rows(kn_ref[0, :, hd * HEAD_BLOCK:(hd + 1) * HEAD_BLOCK], LANES)),
                   NEG_INF) for hd in range(N_HEADS_A)], axis=0)
    m = jnp.maximum(jnp.max(s_c, axis=-1, keepdims=True), jnp.max(s_n, axis=-1, keepdims=True))
    p_c = jnp.exp2(s_c - m)
    p_n = jnp.exp2(s_n - m)
    l = jnp.sum(p_c, axis=-1, keepdims=True) + jnp.sum(p_n, axis=-1, keepdims=True)
    ctx = _dot(p_c.astype(BF16), lat_b).astype(BF16)
    p_n = p_n.astype(BF16)

    outs = []
    for hd in range(N_HEADS_A):
        hr = slice(hd * rows, (hd + 1) * rows)
        hb = slice(hd * HEAD_BLOCK, (hd + 1) * HEAD_BLOCK)
        acc = _dot(ctx[hr], wkv_ref[:, hb]) + _dot(p_n[hr], _pad_rows(vn_ref[0, :, hb], LANES))
        outs.append(acc / l[hr])
    for pr in range(N_HEADS_A // 2):
        o_ref[0, :, pr * LANES:(pr + 1) * LANES] = jnp.where(
            lane < V_DIM_A, pltpu.roll(outs[2 * pr], V_DIM_A, 1), outs[2 * pr + 1]).astype(BF16)


def _mla_sample_call(q_a, latent_cache, k_rope_cache_t, k_new, v_new, weights, consts):
    nb, rows, _ = q_a.shape
    past = latent_cache.shape[1]
    tok = pl.BlockSpec((1, rows, QA_COLS), lambda b: (b, 0, 0))
    const_in = [weights["w_kv_up"], weights["w_k_only"], weights["gk_row"], consts["g_b"]]
    return pl.pallas_call(
        _mla_sample_kernel,
        grid=(nb,),
        in_specs=[tok, pl.BlockSpec((1, past, KV_LORA), lambda b: (b, 0, 0)),
                  pl.BlockSpec((1, ROPE_DIM, past), lambda b: (b, 0, 0)), tok, tok]
        + [_const_spec(a.shape) for a in const_in],
        out_specs=pl.BlockSpec((1, rows, VA_COLS), lambda b: (b, 0, 0)),
        out_shape=jax.ShapeDtypeStruct((nb, rows, VA_COLS), BF16),
        compiler_params=_cparams(1),
        name="mla_sample",
    )(q_a, latent_cache, k_rope_cache_t, k_new, v_new, *const_in)


def _toeplitz_bias(g0, rows):
    far = g0[:, 0:1]
    x0 = jnp.broadcast_to(g0[:, 0:LANES], (rows, LANES))
    x1 = jnp.broadcast_to(g0[:, LANES:2 * LANES], (rows, LANES))
    row = lax.broadcasted_iota(jnp.int32, (rows, LANES), 0)
    lane = lax.broadcasted_iota(jnp.int32, (rows, LANES), 1)
    step = 1
    while step < rows:
        r0 = pltpu.roll(x0, step, 1)
        r1 = pltpu.roll(x1, step, 1)
        keep = lane >= step
        take = (row & step) != 0
        x0, x1 = jnp.where(take, jnp.where(keep, r0, r1), x0), jnp.where(take, jnp.where(keep, r1, r0), x1)
        step *= 2
    return jnp.where(lane < row, far, x0), x1, far


def _band_bias_kernel(rb_ref, bias_ref):
    hd = pl.program_id(0)
    tw0, tw1, far = _toeplitz_bias(rb_ref[pl.ds(hd, 1), :], LANES)
    far_blk = jnp.broadcast_to(far, (LANES, LANES))
    n_blk = BAND_WIN // LANES
    row_c = lax.broadcasted_iota(jnp.int32, (LANES, LANES), 0) // CHUNK
    lane = lax.broadcasted_iota(jnp.int32, (LANES, LANES), 1)
    for half in range(BAND_TQ // LANES):
        first_tw = BAND_WINDOW // LANES - 1 + half
        for cb in range(n_blk):
            blk = tw0 if cb == first_tw else (tw1 if cb == first_tw + 1 else far_blk)
            q_c = row_c + half * (LANES // CHUNK)
            col_c = (cb * LANES + lane) // CHUNK
            ok = (col_c >= q_c) & (col_c <= q_c + LEFT_CHUNKS)
            bias_ref[0, half * LANES:(half + 1) * LANES, cb * LANES:(cb + 1) * LANES] = jnp.where(ok, blk * LOG2E, NEG_INF)


def _band_bias_call(rb_rev):
    return pl.pallas_call(
        _band_bias_kernel,
        grid=(N_HEADS_B,),
        in_specs=[_const_spec(rb_rev.shape)],
        out_specs=pl.BlockSpec((1, BAND_TQ, BAND_WIN), lambda h: (h, 0, 0)),
        out_shape=jax.ShapeDtypeStruct((N_HEADS_B, BAND_TQ, BAND_WIN), F32),
        compiler_params=_cparams(1),
        name="band_bias",
    )(rb_rev)


def _split_heads(q, lane):
    zero = jnp.zeros_like(q)
    return jnp.concatenate([jnp.where(lane < HEAD_DIM_B, q, zero), jnp.where(lane >= HEAD_DIM_B, q, zero)], axis=0)


def _band_prompt_kernel(q_ref, k_ref, v_ref, bias_ref, o_ref, vext_ref):
    seq = q_ref.shape[1]
    n_pairs = q_ref.shape[2] // LANES
    lane_q = lax.broadcasted_iota(jnp.int32, (BAND_TQ, LANES), 1)
    for pr in range(n_pairs):
        vext_ref[pr, :, 0:LANES] = v_ref[0, :, pr * LANES:(pr + 1) * LANES]
        vext_ref[pr, :, LANES:2 * LANES] = jnp.ones((seq, LANES), BF16)
    for t in reversed(range(seq // BAND_TQ)):
        t0 = t * BAND_TQ
        k_lo = max(t0 - BAND_WINDOW, 0)
        w = t0 + BAND_TQ - k_lo
        for pr in range(n_pairs):
            cols = slice(pr * LANES, (pr + 1) * LANES)
            q2 = _split_heads(q_ref[0, t0:t0 + BAND_TQ, cols], lane_q)
            bias2 = jnp.concatenate([bias_ref[2 * pr, :, BAND_WIN - w:BAND_WIN],
                                     bias_ref[2 * pr + 1, :, BAND_WIN - w:BAND_WIN]], axis=0)
            s = _dot_nt(q2, k_ref[0, k_lo:t0 + BAND_TQ, cols]) + bias2
            p = jnp.exp2(s - jnp.max(s, axis=-1, keepdims=True))
            acc = _dot(p.astype(BF16), vext_ref[pr, k_lo:t0 + BAND_TQ, :])
            o2 = acc[:, 0:LANES] / acc[:, LANES:2 * LANES]
            o_ref[0, t0:t0 + BAND_TQ, cols] = jnp.where(lane_q < HEAD_DIM_B, o2[0:BAND_TQ],
                                                        o2[BAND_TQ:2 * BAND_TQ]).astype(BF16)


def _band_prompt_call(q_b, k_b, v_b, bias, pairs_per_step):
    nb, seq, _ = q_b.shape
    n_groups = N_HEADS_B // (2 * pairs_per_step)
    spec = pl.BlockSpec((1, seq, pairs_per_step * LANES), lambda b, p: (b, 0, p))
    return pl.pallas_call(
        _band_prompt_kernel,
        grid=(nb, n_groups),
        in_specs=[spec, spec, spec, pl.BlockSpec((2 * pairs_per_step, BAND_TQ, BAND_WIN), lambda b, p: (p, 0, 0),
                                                 pipeline_mode=pl.Buffered(1 if n_groups == 1 else 2))],
        out_specs=spec,
        out_shape=jax.ShapeDtypeStruct((nb, seq, B_COLS), BF16),
        scratch_shapes=[pltpu.VMEM((pairs_per_step, seq, 2 * LANES), BF16)],
        compiler_params=_cparams(2),
        name="band_prompt",
    )(q_b, k_b, v_b, bias)


def _band_sample_kernel(q_ref, kct_ref, vct_ref, kn_ref, vn_ref, bias_ref, o_ref):
    rows = q_ref.shape[1]
    n_cache = kct_ref.shape[2]
    lane = lax.broadcasted_iota(jnp.int32, (rows, LANES), 1)
    lane2 = lax.broadcasted_iota(jnp.int32, (2 * rows, LANES), 1)
    for pair in range(N_HEADS_B // 2):
        cols = slice(pair * LANES, (pair + 1) * LANES)
        q2 = _split_heads(q_ref[0, :, cols], lane)
        kct = kct_ref[0, cols, :].astype(BF16)
        vct = vct_ref[0, cols, :].astype(BF16)
        kn = _pad_rows(kn_ref[0, :, cols], LANES)
        vn = _pad_rows(vn_ref[0, :, cols], LANES)
        bias_c = jnp.concatenate([bias_ref[2 * pair, :, 0:n_cache], bias_ref[2 * pair + 1, :, 0:n_cache]], axis=0)
        bias_n = jnp.concatenate([bias_ref[2 * pair, :, n_cache:n_cache + LANES],
                                  bias_ref[2 * pair + 1, :, n_cache:n_cache + LANES]], axis=0)
        s_c = _dot(q2, kct) + bias_c
        s_n = jnp.where(lane2 < rows, _dot_nt(q2, kn) + bias_n, NEG_INF)
        m = jnp.maximum(jnp.max(s_c, axis=-1, keepdims=True), jnp.max(s_n, axis=-1, keepdims=True))
        p_c = jnp.exp2(s_c - m)
        p_n = jnp.exp2(s_n - m)
        l = jnp.sum(p_c, axis=-1, keepdims=True) + jnp.sum(p_n, axis=-1, keepdims=True)
        o2 = (_dot_nt(p_c.astype(BF16), vct) + _dot(p_n.astype(BF16), vn)) / l
        o_ref[0, :, cols] = jnp.where(lane < HEAD_DIM_B, o2[0:rows], o2[rows:2 * rows]).astype(BF16)


def _band_sample_call(q_b, k_cache_t, v_cache_t, k_new, v_new, bias):
    nb, rows, _ = q_b.shape
    n_cache = k_cache_t.shape[2]
    tok = pl.BlockSpec((1, rows, B_COLS), lambda b: (b, 0, 0))
    cache = pl.BlockSpec((1, B_COLS, n_cache), lambda b: (b, 0, 0))
    return pl.pallas_call(
        _band_sample_kernel,
        grid=(nb,),
        in_specs=[tok, cache, cache, tok, tok, pl.BlockSpec((N_HEADS_B, rows, BAND_WIN), lambda b: (0, 0, 0))],
        out_specs=tok,
        out_shape=jax.ShapeDtypeStruct((nb, rows, B_COLS), BF16),
        compiler_params=_cparams(1),
        name="band_sample",
    )(q_b, k_cache_t, v_cache_t, k_new, v_new, bias)


def _tail_rows(rows, x_ref, oa_ref, ob_ref, ada_ref, y_ref, gmix_ref, wg_ref, woa_ref, wob_ref, wout_ref, gffn_ref,
               wgate_ref, wup_ref, wdown_ref):
    x = x_ref[0, rows, :]
    ada_rows = rows if ada_ref.shape[1] > 1 else slice(None)
    ada = lambda k: ada_ref[0, ada_rows, k * D_MODEL:(k + 1) * D_MODEL]
    h = x * _row_rms(x, 1.0 / D_MODEL) * gmix_ref[...]
    h = (h * (1.0 + ada(1)) + ada(0)).astype(BF16)
    gates = jax.nn.sigmoid(_dot(h, wg_ref[...]))
    y_a = _dot(oa_ref[0, rows, :], woa_ref[...])
    y_b = _dot(ob_ref[0, rows, :], wob_ref[...])
    mixed = gates[:, 0:D_MODEL] * y_a + gates[:, D_MODEL:2 * D_MODEL] * y_b
    x1 = x + ada(2) * _dot(mixed.astype(BF16), wout_ref[...])
    h2 = x1 * _row_rms(x1, 1.0 / D_MODEL) * gffn_ref[...]
    h2 = (h2 * (1.0 + ada(4)) + ada(3)).astype(BF16)
    acc = jnp.zeros_like(x1)
    for c in range(D_FF // FF_CHUNK):
        cols = slice(c * FF_CHUNK, (c + 1) * FF_CHUNK)
        g = _dot(h2, wgate_ref[:, cols])
        u = _dot(h2, wup_ref[:, cols])
        act = (g * jax.nn.sigmoid(g) * u).astype(BF16)
        acc = acc + _dot(act, wdown_ref[cols, :])
    y_ref[0, rows, :] = x1 + ada(5) * acc


def _tail_kernel(sub, x_ref, oa_ref, ob_ref, ada_ref, xs_ref, oas_ref, obs_ref, adas_ref, *rest):
    weights, (y_ref, ys_ref) = rest[:-2], rest[-2:]
    for si in range(x_ref.shape[1] // sub):
        _tail_rows(slice(si * sub, (si + 1) * sub), x_ref, oa_ref, ob_ref, ada_ref, y_ref, *weights)

    @pl.when((pl.program_id(0) == 0) & (pl.program_id(1) == 0))
    def _():
        _tail_rows(slice(None), xs_ref, oas_ref, obs_ref, adas_ref, ys_ref, *weights)


def _tail_call(x, o_a, o_b, ada, xs, o_a_s, o_b_s, ada_s, weights, tm, sub):
    nb, sb, _ = x.shape
    rows_s = xs.shape[1]
    tok = lambda c: pl.BlockSpec((1, tm, c), lambda b, j: (b, j, 0))
    whole = lambda c: pl.BlockSpec((1, rows_s, c), lambda b, j: (0, 0, 0))
    const_in = [weights["g_mix"], weights["w_in_gate"], weights["w_o_a"], weights["w_o_b"], weights["w_out"],
                weights["g_ffn"], weights["w_gate"], weights["w_up"], weights["w_down"]]
    return pl.pallas_call(
        functools.partial(_tail_kernel, sub),
        grid=(nb, sb // tm),
        in_specs=[tok(D_MODEL), tok(VA_COLS), tok(B_COLS), pl.BlockSpec((1, 1, 6 * D_MODEL), lambda b, j: (b, 0, 0)),
                  whole(D_MODEL), whole(VA_COLS), whole(B_COLS), whole(6 * D_MODEL)]
        + [_const_spec(a.shape) for a in const_in],
        out_specs=[tok(D_MODEL), whole(D_MODEL)],
        out_shape=[jax.ShapeDtypeStruct((nb, sb, D_MODEL), F32), jax.ShapeDtypeStruct((1, rows_s, D_MODEL), F32)],
        compiler_params=_cparams(2),
        name="tail",
    )(x, o_a, o_b, ada, xs, o_a_s, o_b_s, ada_s, *const_in)


def _group_constants():
    def pack(g, inv_cnt):
        ic = np.ones((1, LANES), np.float32)
        ic[0, :len(inv_cnt)] = inv_cnt
        return jnp.asarray(g, BF16), jnp.asarray(np.concatenate([g.T, g.T], axis=0), BF16), jnp.asarray(ic)

    g_q = np.zeros((QA_COLS, LANES), np.float32)
    for hd in range(N_HEADS_A):
        g_q[hd * HEAD_BLOCK:hd * HEAD_BLOCK + NOPE_DIM, hd] = 1.0
        g_q[hd * HEAD_BLOCK + ROPE_LANE0:hd * HEAD_BLOCK + ROPE_LANE0 + ROPE_DIM, N_HEADS_A + hd] = 1.0
    g_b = np.zeros((B_COLS, LANES), np.float32)
    for hd in range(N_HEADS_B):
        g_b[hd * HEAD_DIM_B:(hd + 1) * HEAD_DIM_B, hd] = 1.0
    c = {}
    c["g_q"], c["e2_q"], c["ic_q"] = pack(g_q, [1.0 / NOPE_DIM] * N_HEADS_A + [1.0 / ROPE_DIM] * N_HEADS_A)
    c["g_b"], c["e2_b"], c["ic_b"] = pack(g_b, [1.0 / HEAD_DIM_B] * N_HEADS_B)
    return c


def _rope_tables(pos):
    inv_freq = ROPE_BASE ** (-jnp.arange(HALF_ROPE, dtype=F32) / HALF_ROPE)
    ang = pos.astype(F32)[:, None] * inv_freq[None, :]
    cos, sin = jnp.cos(ang), jnp.sin(ang)
    n = pos.shape[0]
    ones = jnp.ones((n, ROPE_LANE0), F32)
    zeros = jnp.zeros((n, ROPE_LANE0), F32)
    pad1 = jnp.ones((n, LANES - ROPE_LANE0 - ROPE_DIM), F32)
    pad0 = jnp.zeros((n, LANES - ROPE_LANE0 - ROPE_DIM), F32)
    z16 = jnp.zeros((n, HALF_ROPE), F32)
    cos_t = jnp.concatenate([ones, cos, cos, pad1], axis=1)
    sin_a = jnp.concatenate([zeros, -sin, z16, pad0], axis=1)
    sin_b = jnp.concatenate([zeros, z16, sin, pad0], axis=1)
    return cos_t, sin_a, sin_b


def _layer_weights(l, w_in, g_norm_mix, g_q_lora, w_q_up, g_kv_lora, w_kv_up, g_qn_a, g_kn_a, g_qr_a, g_kr_a,
                   g_q_b, g_k_b, w_o_a, w_o_b, w_out, g_norm_ffn, w_gate, w_up, w_down):
    w = {}
    w["w_in_mix"], w["w_in_gate"] = _w_in_prep_call(jnp.transpose(w_in[l]), 256)
    wq3 = w_q_up[l].reshape(Q_LORA, N_HEADS_A, QK_DIM_A)
    w["w_q_up"] = jnp.pad(wq3, ((0, 0), (0, 0), (0, HEAD_BLOCK - QK_DIM_A))).reshape(Q_LORA, QA_COLS).astype(BF16)
    w["w_kv_up"] = w_kv_up[l].astype(BF16)
    w["w_k_only"] = w_kv_up[l].reshape(KV_LORA, N_HEADS_A, NOPE_DIM + V_DIM_A)[..., :NOPE_DIM].reshape(
        KV_LORA, N_HEADS_A * NOPE_DIM).astype(BF16)
    zpad = jnp.zeros((HEAD_BLOCK - QK_DIM_A,), F32)
    w["gq_row"] = jnp.concatenate([g_qn_a[l], g_qr_a[l], zpad]).reshape(1, HEAD_BLOCK)
    w["gk_row"] = jnp.concatenate([g_kn_a[l], jnp.zeros((HEAD_BLOCK - NOPE_DIM,), F32)]).reshape(1, HEAD_BLOCK)
    w["gkr_row"] = jnp.concatenate([jnp.zeros((ROPE_LANE0,), F32), g_kr_a[l], zpad]).reshape(1, LANES)
    w["gqb_row"] = jnp.tile(g_q_b[l], N_HEADS_B).reshape(1, B_COLS)
    w["gkb_row"] = jnp.tile(g_k_b[l], N_HEADS_B).reshape(1, B_COLS)
    w["g_mix"] = g_norm_mix[l].reshape(1, D_MODEL)
    w["g_q_lora"] = g_q_lora[l].reshape(1, Q_LORA)
    w["g_kv_lora"] = g_kv_lora[l].reshape(1, KV_LORA)
    w["g_ffn"] = g_norm_ffn[l].reshape(1, D_MODEL)
    w["w_o_a"] = w_o_a[l].astype(BF16)
    w["w_o_b"] = w_o_b[l].astype(BF16)
    w["w_out"] = w_out[l].astype(BF16)
    w["w_gate"] = w_gate[l].astype(BF16)
    w["w_up"] = w_up[l].astype(BF16)
    w["w_down"] = w_down[l].astype(BF16)
    return w


def kernel(x_prompt, x_sample, c_prompt, c_sample, cache_kv_latent, cache_k_rope, cache_band_k, cache_band_v, w_ada, b_ada, g_norm_mix, w_in, g_q_lora, w_q_up, g_kv_lora, w_kv_up, g_qn_a, g_kn_a, g_qr_a, g_kr_a, g_q_b, g_k_b, rel_bias, w_o_a, w_o_b, w_out, g_norm_ffn, w_gate, w_up, w_down):
    depth = w_in.shape[0]
    nb, seq, _ = x_prompt.shape
    nbs, sd, _ = x_sample.shape
    past = cache_kv_latent.shape[2]
    n_buf = cache_band_k.shape[2]
    keep = min(BAND_WINDOW, seq)
    assert depth == 1 and nbs * sd == LANES and n_buf == BAND_WINDOW and seq % 512 == 0 and past % 512 == 0
    tm = 512
    rows_s = nbs * sd

    consts = _group_constants()
    tab_p = _rope_tables(jnp.arange(seq))
    tab_s = _rope_tables(past + (jnp.arange(rows_s) % sd))
    xs = x_sample.reshape(1, rows_s, D_MODEL)

    l = 0
    wts = _layer_weights(l, w_in, g_norm_mix, g_q_lora, w_q_up, g_kv_lora, w_kv_up, g_qn_a, g_kn_a, g_qr_a, g_kr_a,
                         g_q_b, g_k_b, w_o_a, w_o_b, w_out, g_norm_ffn, w_gate, w_up, w_down)
    band_bias = _band_bias_call(rel_bias[l][:, 2 * REL_CLIP:0:-1])

    ada = _ada_call(jnp.concatenate([c_prompt, c_sample], axis=0), w_ada[l], b_ada[l])
    ada_p = ada[:nb].reshape(nb, 1, 6 * D_MODEL)
    ada_s = jnp.repeat(ada[nb:], sd, axis=0).reshape(1, rows_s, 6 * D_MODEL)

    assert keep == tm
    qa, lat, _, krt, k_a, v_a, qb, kb, vb, kbt_tail, vbt_tail = _mixer_in_call(x_prompt, ada_p, tab_p, consts, wts,
                                                                                2 * tm, tm, True)
    o_a = _mla_prompt_call(qa, k_a, v_a, 512, 4)
    o_b = _band_prompt_call(qb, kb, vb, band_bias, 4)

    qa_s, lat_s, kr_s, _, kn, vn, qb_s, kb_s, vb_s, kb_s32, vb_s32 = _mixer_in_call(xs, ada_s, tab_s, consts, wts,
                                                                                    rows_s, rows_s, False)
    o_a_s = _mla_sample_call(qa_s.reshape(nbs, sd, QA_COLS), cache_kv_latent[l],
                             jnp.transpose(cache_k_rope[l], (0, 2, 1)),
                             kn.reshape(nbs, sd, QA_COLS), vn.reshape(nbs, sd, QA_COLS), wts, consts)
    feat_major = lambda c: jnp.transpose(c, (0, 2, 3, 1)).reshape(nbs, B_COLS, n_buf)
    o_b_s = _band_sample_call(qb_s.reshape(nbs, sd, B_COLS), feat_major(cache_band_k[l]), feat_major(cache_band_v[l]),
                              kb_s.reshape(nbs, sd, B_COLS), vb_s.reshape(nbs, sd, B_COLS), band_bias)

    y_p, y_s = _tail_call(x_prompt, o_a, o_b, ada_p, xs, o_a_s.reshape(1, rows_s, VA_COLS),
                          o_b_s.reshape(1, rows_s, B_COLS), ada_s, wts, tm, tm)

    tok_major = lambda t: jnp.transpose(t.reshape(nb, N_HEADS_B, HEAD_DIM_B, keep), (0, 3, 1, 2))[None]
    return (y_p, y_s.reshape(nbs, sd, D_MODEL),
            lat.reshape(1, nb, seq, KV_LORA), jnp.transpose(krt, (0, 2, 1))[None],
            tok_major(kbt_tail), tok_major(vbt_tail),
            lat_s.reshape(1, nbs, sd, KV_LORA), kr_s.reshape(1, nbs, sd, ROPE_DIM),
            kb_s32.reshape(1, nbs, sd, N_HEADS_B, HEAD_DIM_B), vb_s32.reshape(1, nbs, sd, N_HEADS_B, HEAD_DIM_B))
```

```python
import functools

import jax
import jax.numpy as jnp
import numpy as np
from jax import lax
from jax.experimental import pallas as pl
from jax.experimental.pallas import tpu as pltpu

D_MODEL = 1024
CHUNK = 64
EPS = 1e-6
NEG_INF = -1e30
N_HEADS_A = 8
NOPE_DIM = 64
ROPE_DIM = 32
HALF_ROPE = ROPE_DIM // 2
V_DIM_A = 64
QK_DIM_A = NOPE_DIM + ROPE_DIM
Q_LORA = 384
KV_LORA = 256
ROPE_BASE = 10000.0
N_HEADS_B = 8
HEAD_DIM_B = 64
LEFT_CHUNKS = 8
BAND_WINDOW = LEFT_CHUNKS * CHUNK
REL_CLIP = 128
D_FF = -(-(8 * D_MODEL) // (3 * 256)) * 256
COL_QKV_B = 3 * N_HEADS_B * HEAD_DIM_B
N_GATE_COLS = 2 * D_MODEL

LANES = 128
HEAD_BLOCK = LANES
ROPE_LANE0 = NOPE_DIM
QA_COLS = N_HEADS_A * HEAD_BLOCK
VA_COLS = N_HEADS_A * V_DIM_A
B_COLS = N_HEADS_B * HEAD_DIM_B
MIX_COLS = Q_LORA + KV_LORA + LANES + COL_QKV_B
OFF_CKV = Q_LORA
OFF_KPE = Q_LORA + KV_LORA
OFF_QB = OFF_KPE + LANES
OFF_KB = OFF_QB + B_COLS
OFF_VB = OFF_KB + B_COLS
FF_CHUNK = 256
LOG2E = 1.4426950408889634
BAND_TQ = 256
BAND_WIN = BAND_WINDOW + BAND_TQ
VMEM_LIMIT = 60 * 1024 * 1024

BF16 = jnp.bfloat16
F32 = jnp.float32


def _cparams(n_axes):
    return pltpu.CompilerParams(dimension_semantics=("arbitrary",) * n_axes, vmem_limit_bytes=VMEM_LIMIT)


def _const_spec(shape):
    nd = len(shape)
    return pl.BlockSpec(shape, lambda *_: (0,) * nd, pipeline_mode=pl.Buffered(1))


def _dot(a, b):
    return jnp.dot(a, b, preferred_element_type=F32)


def _dot_nt(a, b):
    return lax.dot_general(a, b, (((1,), (1,)), ((), ())), preferred_element_type=F32)


def _pad_rows(x, rows):
    return jnp.concatenate([x, jnp.zeros((rows - x.shape[0], x.shape[1]), x.dtype)], axis=0)


def _row_rms(x, inv_n):
    return lax.rsqrt(jnp.sum(x * x, axis=-1, keepdims=True) * inv_n + EPS)


def _group_rms(x, g_ref, e2_ref, invcnt_ref):
    s = _dot((x * x).astype(BF16), g_ref[...])
    r = lax.rsqrt(s * invcnt_ref[...] + EPS)
    r_hi = r.astype(BF16)
    r_lo = (r - r_hi.astype(F32)).astype(BF16)
    return _dot(jnp.concatenate([r_hi, r_lo], axis=1), e2_ref[...])


def _rope_block(x, cos_t, sin_a, sin_b):
    return x * cos_t + pltpu.roll(x, LANES - HALF_ROPE, 1) * sin_a + pltpu.roll(x, HALF_ROPE, 1) * sin_b


def _ada_kernel(c_ref, w_ref, b_ref, o_ref):
    c = c_ref[...]
    a = (c * jax.nn.sigmoid(c)).astype(BF16)
    o_ref[...] = _dot(a, w_ref[...].astype(BF16)) + b_ref[...]


def _ada_call(c_all, w_ada, b_ada):
    rows = c_all.shape[0]
    n_out = w_ada.shape[1]
    tn = D_MODEL
    return pl.pallas_call(
        _ada_kernel,
        grid=(n_out // tn,),
        in_specs=[pl.BlockSpec((rows, D_MODEL), lambda n: (0, 0)),
                  pl.BlockSpec((D_MODEL, tn), lambda n: (0, n)),
                  pl.BlockSpec((1, tn), lambda n: (0, n))],
        out_specs=pl.BlockSpec((rows, tn), lambda n: (0, n)),
        out_shape=jax.ShapeDtypeStruct((rows, n_out), F32),
        compiler_params=_cparams(1),
        name="ada",
    )(c_all, w_ada, b_ada.reshape(1, n_out))


def _w_in_prep_kernel(wt_ref, mix_ref, gate_ref):
    tk = wt_ref.shape[1]
    n_mix = OFF_KPE + ROPE_DIM + COL_QKV_B

    def put(dst_ref, col0, src0):
        dst_ref[:, col0:col0 + LANES] = wt_ref[src0:src0 + LANES, :].T.astype(BF16)

    for c in range(OFF_KPE // LANES):
        put(mix_ref, c * LANES, c * LANES)
    kpe = jnp.concatenate([jnp.zeros((ROPE_LANE0, tk), F32), wt_ref[OFF_KPE:OFF_KPE + ROPE_DIM, :],
                           jnp.zeros((LANES - ROPE_LANE0 - ROPE_DIM, tk), F32)], axis=0)
    mix_ref[:, OFF_KPE:OFF_QB] = kpe.T.astype(BF16)
    for c in range(COL_QKV_B // LANES):
        put(mix_ref, OFF_QB + c * LANES, OFF_KPE + ROPE_DIM + c * LANES)
    for c in range(N_GATE_COLS // LANES):
        put(gate_ref, c * LANES, n_mix + c * LANES)


def _w_in_prep_call(w_in_t, tk):
    n_cols, d_in = w_in_t.shape
    return pl.pallas_call(
        _w_in_prep_kernel,
        grid=(d_in // tk,),
        in_specs=[pl.BlockSpec((n_cols, tk), lambda i: (0, i))],
        out_specs=[pl.BlockSpec((tk, MIX_COLS), lambda i: (i, 0)), pl.BlockSpec((tk, N_GATE_COLS), lambda i: (i, 0))],
        out_shape=[jax.ShapeDtypeStruct((d_in, MIX_COLS), BF16), jax.ShapeDtypeStruct((d_in, N_GATE_COLS), BF16)],
        compiler_params=_cparams(1),
        name="w_in_prep",
    )(w_in_t)


def _mixer_in_kernel(sub, tail_feat_major,
                     x_ref, ada_ref, cos_ref, sina_ref, sinb_ref, gmix_ref, w_in_ref, gql_ref, wq_ref, gkv_ref,
                     gq_row_ref, gkr_row_ref, gqb_row_ref, gkb_row_ref,
                     gq_ref, e2q_ref, icq_ref, gb_ref, e2b_ref, icb_ref, wkv_ref, gk_row_ref,
                     qa_ref, lat_ref, kr_ref, krt_ref, ka_ref, va_ref, qb_ref, kb_ref, vb_ref, kbt_ref, vbt_ref):
    n_sub = x_ref.shape[1] // sub
    for si in range(n_sub):
        rows = slice(si * sub, (si + 1) * sub)
        x = x_ref[0, rows, :]
        ada_rows = rows if ada_ref.shape[1] > 1 else slice(None)
        shift = ada_ref[0, ada_rows, 0:D_MODEL]
        scale = ada_ref[0, ada_rows, D_MODEL:2 * D_MODEL]
        h = x * _row_rms(x, 1.0 / D_MODEL) * (gmix_ref[...] * (1.0 + scale)) + shift
        z = _dot(h.astype(BF16), w_in_ref[...])

        cos_t = cos_ref[rows, :]
        sin_a = sina_ref[rows, :]
        sin_b = sinb_ref[rows, :]

        c_kv = z[:, OFF_CKV:OFF_CKV + KV_LORA]
        lat = c_kv * _row_rms(c_kv, 1.0 / KV_LORA) * gkv_ref[...]
        lat_ref[0, rows, :] = lat

        kp = z[:, OFF_KPE:OFF_KPE + LANES]
        kr = _rope_block(kp * _row_rms(kp, 1.0 / ROPE_DIM) * gkr_row_ref[...], cos_t, sin_a, sin_b)
        kr_ref[0, rows, :] = pltpu.roll(kr, LANES - ROPE_LANE0, 1)[:, 0:ROPE_DIM]
        krt_ref[0, :, rows] = kr.T[ROPE_LANE0:ROPE_LANE0 + ROPE_DIM, :]

        def store(hd, k_blk, v_blk):
            ka_ref[0, rows, hd * HEAD_BLOCK:(hd + 1) * HEAD_BLOCK] = k_blk
            va_ref[0, rows, hd * HEAD_BLOCK:(hd + 1) * HEAD_BLOCK] = v_blk

        _expand_kv(lat, kr, wkv_ref[...], gk_row_ref[...], store)

        zq = z[:, OFF_QB:OFF_QB + B_COLS]
        qb_ref[0, rows, :] = (zq * _group_rms(zq, gb_ref, e2b_ref, icb_ref)
                              * (gqb_row_ref[...] * (HEAD_DIM_B ** -0.5 * LOG2E))).astype(BF16)
        zk = z[:, OFF_KB:OFF_KB + B_COLS]
        k_b = zk * _group_rms(zk, gb_ref, e2b_ref, icb_ref) * gkb_row_ref[...]
        v_b = z[:, OFF_VB:OFF_VB + B_COLS]
        kb_ref[0, rows, :] = k_b.astype(BF16)
        vb_ref[0, rows, :] = v_b.astype(BF16)

        c_q = z[:, 0:Q_LORA]
        cqn = c_q * _row_rms(c_q, 1.0 / Q_LORA) * gql_ref[...]
        q_raw = _dot(cqn.astype(BF16), wq_ref[...])
        r_full = _group_rms(q_raw, gq_ref, e2q_ref, icq_ref)
        q_gain = gq_row_ref[...] * (QK_DIM_A ** -0.5 * LOG2E)
        cos_b, sin_a_b, sin_b_b = cos_t.astype(BF16), sin_a.astype(BF16), sin_b.astype(BF16)
        for hd in range(N_HEADS_A):
            hb = slice(hd * HEAD_BLOCK, (hd + 1) * HEAD_BLOCK)
            qn = (q_raw[:, hb] * r_full[:, hb] * q_gain).astype(BF16)
            qa_ref[0, rows, hb] = _rope_block(qn, cos_b, sin_a_b, sin_b_b)

        if si == n_sub - 1:
            @pl.when(pl.program_id(1) == pl.num_programs(1) - 1)
            def _():
                kbt_ref[0] = k_b.T if tail_feat_major else k_b
                vbt_ref[0] = v_b.T if tail_feat_major else v_b


def _mixer_in_call(x, ada, tables, consts, weights, tm, sub, tail_feat_major):
    nb, sb, _ = x.shape
    nj = sb // tm
    keep = sub
    assert tm % sub == 0 and sb % tm == 0
    ada_rows = ada.shape[1]
    if ada_rows == 1:
        ada_spec = pl.BlockSpec((1, 1, 6 * D_MODEL), lambda b, j: (b, 0, 0))
    else:
        ada_spec = pl.BlockSpec((1, tm, 6 * D_MODEL), lambda b, j: (b, j, 0))
    tab_spec = pl.BlockSpec((tm, LANES), lambda b, j: (j, 0))
    tok = lambda c: pl.BlockSpec((1, tm, c), lambda b, j: (b, j, 0))
    if tail_feat_major:
        tail = pl.BlockSpec((1, B_COLS, keep), lambda b, j: (b, 0, 0))
        tail_shape = jax.ShapeDtypeStruct((nb, B_COLS, keep), F32)
    else:
        tail = pl.BlockSpec((1, keep, B_COLS), lambda b, j: (b, 0, 0))
        tail_shape = jax.ShapeDtypeStruct((nb, keep, B_COLS), F32)
    const_in = [weights["g_mix"], weights["w_in_mix"], weights["g_q_lora"], weights["w_q_up"], weights["g_kv_lora"],
                weights["gq_row"], weights["gkr_row"], weights["gqb_row"], weights["gkb_row"],
                consts["g_q"], consts["e2_q"], consts["ic_q"], consts["g_b"], consts["e2_b"], consts["ic_b"],
                weights["w_kv_up"], weights["gk_row"]]
    out_shape = [jax.ShapeDtypeStruct((nb, sb, QA_COLS), BF16),
                 jax.ShapeDtypeStruct((nb, sb, KV_LORA), F32),
                 jax.ShapeDtypeStruct((nb, sb, ROPE_DIM), F32),
                 jax.ShapeDtypeStruct((nb, ROPE_DIM, sb), F32),
                 jax.ShapeDtypeStruct((nb, sb, QA_COLS), BF16),
                 jax.ShapeDtypeStruct((nb, sb, QA_COLS), BF16),
                 jax.ShapeDtypeStruct((nb, sb, B_COLS), BF16),
                 jax.ShapeDtypeStruct((nb, sb, B_COLS), BF16),
                 jax.ShapeDtypeStruct((nb, sb, B_COLS), BF16),
                 tail_shape, tail_shape]
    rope_t = pl.BlockSpec((1, ROPE_DIM, tm), lambda b, j: (b, 0, j))
    return pl.pallas_call(
        functools.partial(_mixer_in_kernel, sub, tail_feat_major),
        grid=(nb, nj),
        in_specs=[tok(D_MODEL), ada_spec, tab_spec, tab_spec, tab_spec] + [_const_spec(a.shape) for a in const_in],
        out_specs=[tok(QA_COLS), tok(KV_LORA), tok(ROPE_DIM), rope_t, tok(QA_COLS), tok(QA_COLS),
                   tok(B_COLS), tok(B_COLS), tok(B_COLS), tail, tail],
        out_shape=out_shape,
        compiler_params=_cparams(2),
        name="mixer_in",
    )(x, ada, tables[0], tables[1], tables[2], *const_in)


def _expand_kv(lat, slot, wkv, gk_row, store):
    tm = lat.shape[0]
    kv = _dot(lat.astype(BF16), wkv)
    nope = lax.broadcasted_iota(jnp.int32, (tm, LANES), 1) < NOPE_DIM
    for hd in range(wkv.shape[1] // HEAD_BLOCK):
        blk = kv[:, hd * HEAD_BLOCK:(hd + 1) * HEAD_BLOCK]
        ssq = jnp.sum(jnp.where(nope, blk * blk, 0.0), axis=-1, keepdims=True)
        r = lax.rsqrt(ssq * (1.0 / NOPE_DIM) + EPS)
        store(hd, jnp.where(nope, blk * r * gk_row, slot).astype(BF16), jnp.where(nope, 1.0, blk).astype(BF16))


def _normalize_heads(acc0, acc1, lane):
    o0 = acc0 / pltpu.roll(acc0, V_DIM_A, 1)
    o1 = acc1 / pltpu.roll(acc1, V_DIM_A, 1)
    return jnp.where(lane < V_DIM_A, pltpu.roll(o0, V_DIM_A, 1), o1)


def _mla_prompt_kernel(tq, q_ref, k_ref, v_ref, o_ref):
    seq = q_ref.shape[1]
    n_heads = q_ref.shape[2] // HEAD_BLOCK
    lane = lax.broadcasted_iota(jnp.int32, (tq, LANES), 1)
    row_c = lax.broadcasted_iota(jnp.int32, (tq, tq), 0) // CHUNK
    col_c = lax.broadcasted_iota(jnp.int32, (tq, tq), 1) // CHUNK
    diag_ok = row_c >= col_c
    for qi in range(seq // tq):
        r0 = qi * tq
        scores = []
        for hd in range(n_heads):
            hb = slice(hd * HEAD_BLOCK, (hd + 1) * HEAD_BLOCK)
            q = q_ref[0, r0:r0 + tq, hb]
            s_f = _dot_nt(q, k_ref[0, 0:r0, hb]) if qi > 0 else None
            scores.append((jnp.where(diag_ok, _dot_nt(q, k_ref[0, r0:r0 + tq, hb]), NEG_INF), s_f))
        accs = []
        for hd in range(n_heads):
            hb = slice(hd * HEAD_BLOCK, (hd + 1) * HEAD_BLOCK)
            s_d, s_f = scores[hd]
            m = jnp.max(s_d, axis=-1, keepdims=True)
            if qi > 0:
                m = jnp.maximum(m, jnp.max(s_f, axis=-1, keepdims=True))
            acc = _dot(jnp.exp2(s_d - m).astype(BF16), v_ref[0, r0:r0 + tq, hb])
            if qi > 0:
                acc = acc + _dot(jnp.exp2(s_f - m).astype(BF16), v_ref[0, 0:r0, hb])
            accs.append(acc)
        for pr in range(n_heads // 2):
            o_ref[0, r0:r0 + tq, pr * LANES:(pr + 1) * LANES] = _normalize_heads(
                accs[2 * pr], accs[2 * pr + 1], lane).astype(BF16)


def _mla_prompt_call(q_a, k_a, v_a, tq, heads_per_step):
    nb, seq, _ = q_a.shape
    n_groups = N_HEADS_A // heads_per_step
    group = pl.BlockSpec((1, seq, heads_per_step * HEAD_BLOCK), lambda b, g: (b, 0, g))
    return pl.pallas_call(
        functools.partial(_mla_prompt_kernel, tq),
        grid=(nb, n_groups),
        in_specs=[group, group, group],
        out_specs=pl.BlockSpec((1, seq, heads_per_step * V_DIM_A), lambda b, g: (b, 0, g)),
        out_shape=jax.ShapeDtypeStruct((nb, seq, VA_COLS), BF16),
        compiler_params=_cparams(2),
        name="mla_prompt",
    )(q_a, k_a, v_a)


def _mla_sample_kernel(q_ref, lat_ref, krt_ref, kn_ref, vn_ref, wkv_ref, wk_ref, gk_row_ref, gsum_ref, o_ref):
    rows = q_ref.shape[1]
    past = lat_ref.shape[1]
    n_stack = N_HEADS_A * rows
    lane = lax.broadcasted_iota(jnp.int32, (rows, LANES), 1)
    lat_b = lat_ref[0].astype(BF16)

    kn_raw = _dot(lat_b, wk_ref[...])
    ssq = _dot((kn_raw * kn_raw).astype(BF16), gsum_ref[...])
    r_t = lax.rsqrt(ssq * (1.0 / NOPE_DIM) + EPS).T

    q_blocks = [q_ref[0, :, hd * HEAD_BLOCK:(hd + 1) * HEAD_BLOCK] for hd in range(N_HEADS_A)]
    q_abs = jnp.concatenate(
        [_dot_nt((q_blocks[hd].astype(F32) * gk_row_ref[...]).astype(BF16), wkv_ref[:, hd * HEAD_BLOCK:(hd + 1) * HEAD_BLOCK])
         for hd in range(N_HEADS_A)], axis=0).astype(BF16)
    r_rows = jnp.concatenate([jnp.broadcast_to(r_t[hd:hd + 1, :], (rows, past)) for hd in range(N_HEADS_A)], axis=0)
    krt_pad = jnp.concatenate([jnp.zeros((ROPE_LANE0, past), F32), krt_ref[0],
                               jnp.zeros((LANES - ROPE_LANE0 - ROPE_DIM, past), F32)], axis=0).astype(BF16)
    q_stack = jnp.concatenate(q_blocks, axis=0)
    s_c = _dot_nt(q_abs, lat_b) * r_rows + _dot(q_stack, krt_pad)

    s_n = jnp.concatenate(
        [jnp.where(lane < rows, _dot_nt(q_blocks[hd], _pad_rows(kn_ref[0, :, hd * HEAD_BLOCK:(hd + 1) * HEAD_BLOCK], LANES)),
                   NEG_INF) for hd in range(N_HEADS_A)], axis=0)
    m = jnp.maximum(jnp.max(s_c, axis=-1, keepdims=True), jnp.max(s_n, axis=-1, keepdims=True))
    p_c = jnp.exp2(s_c - m)
    p_n = jnp.exp2(s_n - m)
    l = jnp.sum(p_c, axis=-1, keepdims=True) + jnp.sum(p_n, axis=-1, keepdims=True)
    ctx = _dot(p_c.astype(BF16), lat_b).astype(BF16)
    p_n = p_n.astype(BF16)

    outs = []
    for hd in range(N_HEADS_A):
        hr = slice(hd * rows, (hd + 1) * rows)
        hb = slice(hd * HEAD_BLOCK, (hd + 1) * HEAD_BLOCK)
        acc = _dot(ctx[hr], wkv_ref[:, hb]) + _dot(p_n[hr], _pad_rows(vn_ref[0, :, hb], LANES))
        outs.append(acc / l[hr])
    for pr in range(N_HEADS_A // 2):
        o_ref[0, :, pr * LANES:(pr + 1) * LANES] = jnp.where(
            lane < V_DIM_A, pltpu.roll(outs[2 * pr], V_DIM_A, 1), outs[2 * pr + 1]).astype(BF16)


def _mla_sample_call(q_a, latent_cache, k_rope_cache_t, k_new, v_new, weights, consts):
    nb, rows, _ = q_a.shape
    past = latent_cache.shape[1]
    tok = pl.BlockSpec((1, rows, QA_COLS), lambda b: (b, 0, 0))
    const_in = [weights["w_kv_up"], weights["w_k_only"], weights["gk_row"], consts["g_b"]]
    return pl.pallas_call(
        _mla_sample_kernel,
        grid=(nb,),
        in_specs=[tok, pl.BlockSpec((1, past, KV_LORA), lambda b: (b, 0, 0)),
                  pl.BlockSpec((1, ROPE_DIM, past), lambda b: (b, 0, 0)), tok, tok]
        + [_const_spec(a.shape) for a in const_in],
        out_specs=pl.BlockSpec((1, rows, VA_COLS), lambda b: (b, 0, 0)),
        out_shape=jax.ShapeDtypeStruct((nb, rows, VA_COLS), BF16),
        compiler_params=_cparams(1),
        name="mla_sample",
    )(q_a, latent_cache, k_rope_cache_t, k_new, v_new, *const_in)


def _toeplitz_bias(g0, rows):
    far = g0[:, 0:1]
    x0 = jnp.broadcast_to(g0[:, 0:LANES], (rows, LANES))
    x1 = jnp.broadcast_to(g0[:, LANES:2 * LANES], (rows, LANES))
    row = lax.broadcasted_iota(jnp.int32, (rows, LANES), 0)
    lane = lax.broadcasted_iota(jnp.int32, (rows, LANES), 1)
    step = 1
    while step < rows:
        r0 = pltpu.roll(x0, step, 1)
        r1 = pltpu.roll(x1, step, 1)
        keep = lane >= step
        take = (row & step) != 0
        x0, x1 = jnp.where(take, jnp.where(keep, r0, r1), x0), jnp.where(take, jnp.where(keep, r1, r0), x1)
        step *= 2
    return jnp.where(lane < row, far, x0), x1, far


def _band_bias_kernel(rb_ref, bias_ref):
    hd = pl.program_id(0)
    tw0, tw1, far = _toeplitz_bias(rb_ref[pl.ds(hd, 1), :], LANES)
    far_blk = jnp.broadcast_to(far, (LANES, LANES))
    n_blk = BAND_WIN // LANES
    row_c = lax.broadcasted_iota(jnp.int32, (LANES, LANES), 0) // CHUNK
    lane = lax.broadcasted_iota(jnp.int32, (LANES, LANES), 1)
    for half in range(BAND_TQ // LANES):
        first_tw = BAND_WINDOW // LANES - 1 + half
        for cb in range(n_blk):
            blk = tw0 if cb == first_tw else (tw1 if cb == first_tw + 1 else far_blk)
            q_c = row_c + half * (LANES // CHUNK)
            col_c = (cb * LANES + lane) // CHUNK
            ok = (col_c >= q_c) & (col_c <= q_c + LEFT_CHUNKS)
            bias_ref[0, half * LANES:(half + 1) * LANES, cb * LANES:(cb + 1) * LANES] = jnp.where(ok, blk * LOG2E, NEG_INF)


def _band_bias_call(rb_rev):
    return pl.pallas_call(
        _band_bias_kernel,
        grid=(N_HEADS_B,),
        in_specs=[_const_spec(rb_rev.shape)],
        out_specs=pl.BlockSpec((1, BAND_TQ, BAND_WIN), lambda h: (h, 0, 0)),
        out_shape=jax.ShapeDtypeStruct((N_HEADS_B, BAND_TQ, BAND_WIN), F32),
        compiler_params=_cparams(1),
        name="band_bias",
    )(rb_rev)


def _split_heads(q, lane):
    zero = jnp.zeros_like(q)
    return jnp.concatenate([jnp.where(lane < HEAD_DIM_B, q, zero), jnp.where(lane >= HEAD_DIM_B, q, zero)], axis=0)


def _band_prompt_kernel(q_ref, k_ref, v_ref, bias_ref, o_ref, vext_ref):
    seq = q_ref.shape[1]
    n_pairs = q_ref.shape[2] // LANES
    lane_q = lax.broadcasted_iota(jnp.int32, (BAND_TQ, LANES), 1)
    for pr in range(n_pairs):
        vext_ref[pr, :, 0:LANES] = v_ref[0, :, pr * LANES:(pr + 1) * LANES]
        vext_ref[pr, :, LANES:2 * LANES] = jnp.ones((seq, LANES), BF16)
    for t in reversed(range(seq // BAND_TQ)):
        t0 = t * BAND_TQ
        k_lo = max(t0 - BAND_WINDOW, 0)
        w = t0 + BAND_TQ - k_lo
        for pr in range(n_pairs):
            cols = slice(pr * LANES, (pr + 1) * LANES)
            q2 = _split_heads(q_ref[0, t0:t0 + BAND_TQ, cols], lane_q)
            bias2 = jnp.concatenate([bias_ref[2 * pr, :, BAND_WIN - w:BAND_WIN],
                                     bias_ref[2 * pr + 1, :, BAND_WIN - w:BAND_WIN]], axis=0)
            s = _dot_nt(q2, k_ref[0, k_lo:t0 + BAND_TQ, cols]) + bias2
            p = jnp.exp2(s - jnp.max(s, axis=-1, keepdims=True))
            acc = _dot(p.astype(BF16), vext_ref[pr, k_lo:t0 + BAND_TQ, :])
            o2 = acc[:, 0:LANES] / acc[:, LANES:2 * LANES]
            o_ref[0, t0:t0 + BAND_TQ, cols] = jnp.where(lane_q < HEAD_DIM_B, o2[0:BAND_TQ],
                                                        o2[BAND_TQ:2 * BAND_TQ]).astype(BF16)


def _band_prompt_call(q_b, k_b, v_b, bias, pairs_per_step):
    nb, seq, _ = q_b.shape
    n_groups = N_HEADS_B // (2 * pairs_per_step)
    spec = pl.BlockSpec((1, seq, pairs_per_step * LANES), lambda b, p: (b, 0, p))
    return pl.pallas_call(
        _band_prompt_kernel,
        grid=(nb, n_groups),
        in_specs=[spec, spec, spec, pl.BlockSpec((2 * pairs_per_step, BAND_TQ, BAND_WIN), lambda b, p: (p, 0, 0),
                                                 pipeline_mode=pl.Buffered(1 if n_groups == 1 else 2))],
        out_specs=spec,
        out_shape=jax.ShapeDtypeStruct((nb, seq, B_COLS), BF16),
        scratch_shapes=[pltpu.VMEM((pairs_per_step, seq, 2 * LANES), BF16)],
        compiler_params=_cparams(2),
        name="band_prompt",
    )(q_b, k_b, v_b, bias)


def _band_sample_kernel(q_ref, kct_ref, vct_ref, kn_ref, vn_ref, bias_ref, o_ref):
    rows = q_ref.shape[1]
    n_cache = kct_ref.shape[2]
    lane = lax.broadcasted_iota(jnp.int32, (rows, LANES), 1)
    lane2 = lax.broadcasted_iota(jnp.int32, (2 * rows, LANES), 1)
    for pair in range(N_HEADS_B // 2):
        cols = slice(pair * LANES, (pair + 1) * LANES)
        q2 = _split_heads(q_ref[0, :, cols], lane)
        kct = kct_ref[0, cols, :].astype(BF16)
        vct = vct_ref[0, cols, :].astype(BF16)
        kn = _pad_rows(kn_ref[0, :, cols], LANES)
        vn = _pad_rows(vn_ref[0, :, cols], LANES)
        bias_c = jnp.concatenate([bias_ref[2 * pair, :, 0:n_cache], bias_ref[2 * pair + 1, :, 0:n_cache]], axis=0)
        bias_n = jnp.concatenate([bias_ref[2 * pair, :, n_cache:n_cache + LANES],
                                  bias_ref[2 * pair + 1, :, n_cache:n_cache + LANES]], axis=0)
        s_c = _dot(q2, kct) + bias_c
        s_n = jnp.where(lane2 < rows, _dot_nt(q2, kn) + bias_n, NEG_INF)
        m = jnp.maximum(jnp.max(s_c, axis=-1, keepdims=True), jnp.max(s_n, axis=-1, keepdims=True))
        p_c = jnp.exp2(s_c - m)
        p_n = jnp.exp2(s_n - m)
        l = jnp.sum(p_c, axis=-1, keepdims=True) + jnp.sum(p_n, axis=-1, keepdims=True)
        o2 = (_dot_nt(p_c.astype(BF16), vct) + _dot(p_n.astype(BF16), vn)) / l
        o_ref[0, :, cols] = jnp.where(lane < HEAD_DIM_B, o2[0:rows], o2[rows:2 * rows]).astype(BF16)


def _band_sample_call(q_b, k_cache_t, v_cache_t, k_new, v_new, bias):
    nb, rows, _ = q_b.shape
    n_cache = k_cache_t.shape[2]
    tok = pl.BlockSpec((1, rows, B_COLS), lambda b: (b, 0, 0))
    cache = pl.BlockSpec((1, B_COLS, n_cache), lambda b: (b, 0, 0))
    return pl.pallas_call(
        _band_sample_kernel,
        grid=(nb,),
        in_specs=[tok, cache, cache, tok, tok, pl.BlockSpec((N_HEADS_B, rows, BAND_WIN), lambda b: (0, 0, 0))],
        out_specs=tok,
        out_shape=jax.ShapeDtypeStruct((nb, rows, B_COLS), BF16),
        compiler_params=_cparams(1),
        name="band_sample",
    )(q_b, k_cache_t, v_cache_t, k_new, v_new, bias)


def _tail_rows(rows, x_ref, oa_ref, ob_ref, ada_ref, y_ref, gmix_ref, wg_ref, woa_ref, wob_ref, wout_ref, gffn_ref,
               wgate_ref, wup_ref, wdown_ref):
    x = x_ref[0, rows, :]
    ada_rows = rows if ada_ref.shape[1] > 1 else slice(None)
    ada = lambda k: ada_ref[0, ada_rows, k * D_MODEL:(k + 1) * D_MODEL]
    h = x * _row_rms(x, 1.0 / D_MODEL) * gmix_ref[...]
    h = (h * (1.0 + ada(1)) + ada(0)).astype(BF16)
    gates = jax.nn.sigmoid(_dot(h, wg_ref[...]))
    y_a = _dot(oa_ref[0, rows, :], woa_ref[...])
    y_b = _dot(ob_ref[0, rows, :], wob_ref[...])
    mixed = gates[:, 0:D_MODEL] * y_a + gates[:, D_MODEL:2 * D_MODEL] * y_b
    x1 = x + ada(2) * _dot(mixed.astype(BF16), wout_ref[...])
    h2 = x1 * _row_rms(x1, 1.0 / D_MODEL) * gffn_ref[...]
    h2 = (h2 * (1.0 + ada(4)) + ada(3)).astype(BF16)
    acc = jnp.zeros_like(x1)
    for c in range(D_FF // FF_CHUNK):
        cols = slice(c * FF_CHUNK, (c + 1) * FF_CHUNK)
        g = _dot(h2, wgate_ref[:, cols])
        u = _dot(h2, wup_ref[:, cols])
        act = (g * jax.nn.sigmoid(g) * u).astype(BF16)
        acc = acc + _dot(act, wdown_ref[cols, :])
    y_ref[0, rows, :] = x1 + ada(5) * acc


def _tail_kernel(sub, x_ref, oa_ref, ob_ref, ada_ref, xs_ref, oas_ref, obs_ref, adas_ref, *rest):
    weights, (y_ref, ys_ref) = rest[:-2], rest[-2:]
    for si in range(x_ref.shape[1] // sub):
        _tail_rows(slice(si * sub, (si + 1) * sub), x_ref, oa_ref, ob_ref, ada_ref, y_ref, *weights)

    @pl.when((pl.program_id(0) == 0) & (pl.program_id(1) == 0))
    def _():
        _tail_rows(slice(None), xs_ref, oas_ref, obs_ref, adas_ref, ys_ref, *weights)


def _tail_call(x, o_a, o_b, ada, xs, o_a_s, o_b_s, ada_s, weights, tm, sub):
    nb, sb, _ = x.shape
    rows_s = xs.shape[1]
    tok = lambda c: pl.BlockSpec((1, tm, c), lambda b, j: (b, j, 0))
    whole = lambda c: pl.BlockSpec((1, rows_s, c), lambda b, j: (0, 0, 0))
    const_in = [weights["g_mix"], weights["w_in_gate"], weights["w_o_a"], weights["w_o_b"], weights["w_out"],
                weights["g_ffn"], weights["w_gate"], weights["w_up"], weights["w_down"]]
    return pl.pallas_call(
        functools.partial(_tail_kernel, sub),
        grid=(nb, sb // tm),
        in_specs=[tok(D_MODEL), tok(VA_COLS), tok(B_COLS), pl.BlockSpec((1, 1, 6 * D_MODEL), lambda b, j: (b, 0, 0)),
                  whole(D_MODEL), whole(VA_COLS), whole(B_COLS), whole(6 * D_MODEL)]
        + [_const_spec(a.shape) for a in const_in],
        out_specs=[tok(D_MODEL), whole(D_MODEL)],
        out_shape=[jax.ShapeDtypeStruct((nb, sb, D_MODEL), F32), jax.ShapeDtypeStruct((1, rows_s, D_MODEL), F32)],
        compiler_params=_cparams(2),
        name="tail",
    )(x, o_a, o_b, ada, xs, o_a_s, o_b_s, ada_s, *const_in)


def _group_constants():
    def pack(g, inv_cnt):
        ic = np.ones((1, LANES), np.float32)
        ic[0, :len(inv_cnt)] = inv_cnt
        return jnp.asarray(g, BF16), jnp.asarray(np.concatenate([g.T, g.T], axis=0), BF16), jnp.asarray(ic)

    g_q = np.zeros((QA_COLS, LANES), np.float32)
    for hd in range(N_HEADS_A):
        g_q[hd * HEAD_BLOCK:hd * HEAD_BLOCK + NOPE_DIM, hd] = 1.0
        g_q[hd * HEAD_BLOCK + ROPE_LANE0:hd * HEAD_BLOCK + ROPE_LANE0 + ROPE_DIM, N_HEADS_A + hd] = 1.0
    g_b = np.zeros((B_COLS, LANES), np.float32)
    for hd in range(N_HEADS_B):
        g_b[hd * HEAD_DIM_B:(hd + 1) * HEAD_DIM_B, hd] = 1.0
    c = {}
    c["g_q"], c["e2_q"], c["ic_q"] = pack(g_q, [1.0 / NOPE_DIM] * N_HEADS_A + [1.0 / ROPE_DIM] * N_HEADS_A)
    c["g_b"], c["e2_b"], c["ic_b"] = pack(g_b, [1.0 / HEAD_DIM_B] * N_HEADS_B)
    return c


def _rope_tables(pos):
    inv_freq = ROPE_BASE ** (-jnp.arange(HALF_ROPE, dtype=F32) / HALF_ROPE)
    ang = pos.astype(F32)[:, None] * inv_freq[None, :]
    cos, sin = jnp.cos(ang), jnp.sin(ang)
    n = pos.shape[0]
    ones = jnp.ones((n, ROPE_LANE0), F32)
    zeros = jnp.zeros((n, ROPE_LANE0), F32)
    pad1 = jnp.ones((n, LANES - ROPE_LANE0 - ROPE_DIM), F32)
    pad0 = jnp.zeros((n, LANES - ROPE_LANE0 - ROPE_DIM), F32)
    z16 = jnp.zeros((n, HALF_ROPE), F32)
    cos_t = jnp.concatenate([ones, cos, cos, pad1], axis=1)
    sin_a = jnp.concatenate([zeros, -sin, z16, pad0], axis=1)
    sin_b = jnp.concatenate([zeros, z16, sin, pad0], axis=1)
    return cos_t, sin_a, sin_b


def _layer_weights(l, w_in, g_norm_mix, g_q_lora, w_q_up, g_kv_lora, w_kv_up, g_qn_a, g_kn_a, g_qr_a, g_kr_a,
                   g_q_b, g_k_b, w_o_a, w_o_b, w_out, g_norm_ffn, w_gate, w_up, w_down):
    w = {}
    w["w_in_mix"], w["w_in_gate"] = _w_in_prep_call(jnp.transpose(w_in[l]), 256)
    wq3 = w_q_up[l].reshape(Q_LORA, N_HEADS_A, QK_DIM_A)
    w["w_q_up"] = jnp.pad(wq3, ((0, 0), (0, 0), (0, HEAD_BLOCK - QK_DIM_A))).reshape(Q_LORA, QA_COLS).astype(BF16)
    w["w_kv_up"] = w_kv_up[l].astype(BF16)
    w["w_k_only"] = w_kv_up[l].reshape(KV_LORA, N_HEADS_A, NOPE_DIM + V_DIM_A)[..., :NOPE_DIM].reshape(
        KV_LORA, N_HEADS_A * NOPE_DIM).astype(BF16)
    zpad = jnp.zeros((HEAD_BLOCK - QK_DIM_A,), F32)
    w["gq_row"] = jnp.concatenate([g_qn_a[l], g_qr_a[l], zpad]).reshape(1, HEAD_BLOCK)
    w["gk_row"] = jnp.concatenate([g_kn_a[l], jnp.zeros((HEAD_BLOCK - NOPE_DIM,), F32)]).reshape(1, HEAD_BLOCK)
    w["gkr_row"] = jnp.concatenate([jnp.zeros((ROPE_LANE0,), F32), g_kr_a[l], zpad]).reshape(1, LANES)
    w["gqb_row"] = jnp.tile(g_q_b[l], N_HEADS_B).reshape(1, B_COLS)
    w["gkb_row"] = jnp.tile(g_k_b[l], N_HEADS_B).reshape(1, B_COLS)
    w["g_mix"] = g_norm_mix[l].reshape(1, D_MODEL)
    w["g_q_lora"] = g_q_lora[l].reshape(1, Q_LORA)
    w["g_kv_lora"] = g_kv_lora[l].reshape(1, KV_LORA)
    w["g_ffn"] = g_norm_ffn[l].reshape(1, D_MODEL)
    w["w_o_a"] = w_o_a[l].astype(BF16)
    w["w_o_b"] = w_o_b[l].astype(BF16)
    w["w_out"] = w_out[l].astype(BF16)
    w["w_gate"] = w_gate[l].astype(BF16)
    w["w_up"] = w_up[l].astype(BF16)
    w["w_down"] = w_down[l].astype(BF16)
    return w


def kernel(x_prompt, x_sample, c_prompt, c_sample, cache_kv_latent, cache_k_rope, cache_band_k, cache_band_v, w_ada, b_ada, g_norm_mix, w_in, g_q_lora, w_q_up, g_kv_lora, w_kv_up, g_qn_a, g_kn_a, g_qr_a, g_kr_a, g_q_b, g_k_b, rel_bias, w_o_a, w_o_b, w_out, g_norm_ffn, w_gate, w_up, w_down):
    depth = w_in.shape[0]
    nb, seq, _ = x_prompt.shape
    nbs, sd, _ = x_sample.shape
    past = cache_kv_latent.shape[2]
    n_buf = cache_band_k.shape[2]
    keep = min(BAND_WINDOW, seq)
    assert depth == 1 and nbs * sd == LANES and n_buf == BAND_WINDOW and seq % 512 == 0 and past % 512 == 0
    tm = 512
    rows_s = nbs * sd

    consts = _group_constants()
    tab_p = _rope_tables(jnp.arange(seq))
    tab_s = _rope_tables(past + (jnp.arange(rows_s) % sd))
    xs = x_sample.reshape(1, rows_s, D_MODEL)

    l = 0
    wts = _layer_weights(l, w_in, g_norm_mix, g_q_lora, w_q_up, g_kv_lora, w_kv_up, g_qn_a, g_kn_a, g_qr_a, g_kr_a,
                         g_q_b, g_k_b, w_o_a, w_o_b, w_out, g_norm_ffn, w_gate, w_up, w_down)
    band_bias = _band_bias_call(rel_bias[l][:, 2 * REL_CLIP:0:-1])

    ada = _ada_call(jnp.concatenate([c_prompt, c_sample], axis=0), w_ada[l], b_ada[l])
    ada_p = ada[:nb].reshape(nb, 1, 6 * D_MODEL)
    ada_s = jnp.repeat(ada[nb:], sd, axis=0).reshape(1, rows_s, 6 * D_MODEL)

    assert keep == tm
    qa, lat, _, krt, k_a, v_a, qb, kb, vb, kbt_tail, vbt_tail = _mixer_in_call(x_prompt, ada_p, tab_p, consts, wts,
                                                                                2 * tm, tm, True)
    o_a = _mla_prompt_call(qa, k_a, v_a, 512, 4)
    o_b = _band_prompt_call(qb, kb, vb, band_bias, 4)

    qa_s, lat_s, kr_s, _, kn, vn, qb_s, kb_s, vb_s, kb_s32, vb_s32 = _mixer_in_call(xs, ada_s, tab_s, consts, wts,
                                                                                    rows_s, rows_s, False)
    o_a_s = _mla_sample_call(qa_s.reshape(nbs, sd, QA_COLS), cache_kv_latent[l],
                             jnp.transpose(cache_k_rope[l], (0, 2, 1)),
                             kn.reshape(nbs, sd, QA_COLS), vn.reshape(nbs, sd, QA_COLS), wts, consts)
    feat_major = lambda c: jnp.transpose(c, (0, 2, 3, 1)).reshape(nbs, B_COLS, n_buf)
    o_b_s = _band_sample_call(qb_s.reshape(nbs, sd, B_COLS), feat_major(cache_band_k[l]), feat_major(cache_band_v[l]),
                              kb_s.reshape(nbs, sd, B_COLS), vb_s.reshape(nbs, sd, B_COLS), band_bias)

    y_p, y_s = _tail_call(x_prompt, o_a, o_b, ada_p, xs, o_a_s.reshape(1, rows_s, VA_COLS),
                          o_b_s.reshape(1, rows_s, B_COLS), ada_s, wts, tm, tm)

    tok_major = lambda t: jnp.transpose(t.reshape(nb, N_HEADS_B, HEAD_DIM_B, keep), (0, 3, 1, 2))[None]
    return (y_p, y_s.reshape(nbs, sd, D_MODEL),
            lat.reshape(1, nb, seq, KV_LORA), jnp.transpose(krt, (0, 2, 1))[None],
            tok_major(kbt_tail), tok_major(vbt_tail),
            lat_s.reshape(1, nbs, sd, KV_LORA), kr_s.reshape(1, nbs, sd, ROPE_DIM),
            kb_s32.reshape(1, nbs, sd, N_HEADS_B, HEAD_DIM_B), vb_s32.reshape(1, nbs, sd, N_HEADS_B, HEAD_DIM_B))
```
